```python
import math
import jax, jax.numpy as jnp
from jax import lax
import numpy as np

D_MODEL = 2048
BATCH = 1
SEQ = 16384
DEPTH = 2

CHUNK = 64
N_MIXERS = 2
N_A_LAYERS = (DEPTH + 1) // 2
N_B_LAYERS = DEPTH // 2
D_RNN = 2560
LRU_HEADS = 16
LRU_HEAD_DIM = D_RNN // LRU_HEADS
CONV_WIDTH = 4
LRU_C = 8.0
D_SSM = 1024
SSM_GROUP = 16
SSM_GROUPS = D_SSM // SSM_GROUP
SSM_STATE = 64
DT_MIN = 1e-3
DT_MAX = 1e-1
N_GROUPS = 4
EXPERTS_PER_GROUP = 8
N_EXPERTS = N_GROUPS * EXPERTS_PER_GROUP
TOP_K = 2
D_EXPERT = 512
MOE_BLOCK = 128
EPS = 1e-6

kernel_name = "hybrid_rglru_s5_hmoe_encoder"


def rms_norm(x, g):
    x32 = x.astype(jnp.float32)
    y = x32 * lax.rsqrt(jnp.mean(x32 * x32, axis=-1, keepdims=True) + EPS) * g.astype(jnp.float32)
    return y.astype(x.dtype)


def real_combine(e1, e2):
    a1, b1 = e1
    a2, b2 = e2
    return a1 * a2, a2 * b1 + b2


def cplx_combine(e1, e2):
    a1r, a1i, b1r, b1i = e1
    a2r, a2i, b2r, b2i = e2
    ar = a2r * a1r - a2i * a1i
    ai = a2r * a1i + a2i * a1r
    br = a2r * b1r - a2i * b1i + b2r
    bi = a2r * b1i + a2i * b1r + b2i
    return ar, ai, br, bi


def rglru_block(x, w_in, conv_w, conv_b, wa, ba, wx, bx, lru_L, w_out):
    b, s, _ = x.shape
    gate_br, rec = jnp.split(x @ w_in, 2, axis=-1)
    rec_pad = jnp.pad(rec, ((0, 0), (CONV_WIDTH - 1, 0), (0, 0)))
    conv = conv_b
    for k in range(CONV_WIDTH):
        conv = conv + rec_pad[:, k:k + s] * conv_w[k]
    xh = conv.reshape(b, s, LRU_HEADS, LRU_HEAD_DIM)
    r = jax.nn.sigmoid(jnp.einsum('bshi,hij->bshj', xh, wa) + ba).reshape(b, s, D_RNN)
    i = jax.nn.sigmoid(jnp.einsum('bshi,hij->bshj', xh, wx) + bx).reshape(b, s, D_RNN)
    log_a = (-LRU_C * r.astype(jnp.float32)) * jax.nn.softplus(-lru_L.astype(jnp.float32))
    a = jnp.exp(log_a)
    bt = jnp.sqrt(-jnp.expm1(2.0 * log_a)) * (i * conv).astype(jnp.float32)
    _, h = lax.associative_scan(real_combine, (a, bt), axis=1)
    y = h.astype(x.dtype) * jax.nn.gelu(gate_br)
    return y @ w_out


def s5_block(x, w_in, lam_re, lam_im, log_dt, b_re, b_im, c_re, c_im, d_skip, w_glu):
    b, s, _ = x.shape
    nc = s // CHUNK
    f32 = jnp.float32
    u = (x @ w_in).astype(f32)
    lre = jnp.minimum(lam_re.astype(f32), -1e-4)
    lim = lam_im.astype(f32)
    dt = jnp.exp(log_dt.astype(f32))[:, None]
    mag = jnp.exp(lre * dt)
    abr = mag * jnp.cos(lim * dt)
    abi = mag * jnp.sin(lim * dt)
    den = lre * lre + lim * lim
    kr = ((abr - 1.0) * lre + abi * lim) / den
    ki = (abi * lre - (abr - 1.0) * lim) / den
    br, bi = b_re.astype(f32), b_im.astype(f32)
    bbar_r = kr[..., None] * br - ki[..., None] * bi
    bbar_i = kr[..., None] * bi + ki[..., None] * br
    ug = u.reshape(b, nc, CHUNK, SSM_GROUPS, SSM_GROUP)
    bur = jnp.einsum('bncgk,gpk->bncgp', ug, bbar_r)
    bui = jnp.einsum('bncgk,gpk->bncgp', ug, bbar_i)
    shp = bur.shape
    elems = (jnp.broadcast_to(abr, shp), jnp.broadcast_to(abi, shp), bur, bui)
    pr, pi, loc_r, loc_i = lax.associative_scan(cplx_combine, elems, axis=2)
    summary = (pr[:, :, -1], pi[:, :, -1], loc_r[:, :, -1], loc_i[:, :, -1])
    _, _, sr, si = lax.associative_scan(cplx_combine, summary, axis=1)
    pad = ((0, 0), (1, 0), (0, 0), (0, 0))
    prev_r = jnp.pad(sr[:, :-1], pad)[:, :, None]
    prev_i = jnp.pad(si[:, :-1], pad)[:, :, None]
    xr = loc_r + pr * prev_r - pi * prev_i
    xi = loc_i + pr * prev_i + pi * prev_r
    y = jnp.einsum('bncgp,gkp->bncgk', xr, c_re) - jnp.einsum('bncgp,gkp->bncgk', xi, c_im)
    y = y.reshape(b, s, D_SSM) + d_skip * u
    z = jax.nn.gelu(y).astype(x.dtype)
    v, g = jnp.split(z @ w_glu, 2, axis=-1)
    return v * jax.nn.sigmoid(g)


def hier_moe(x, wg, bg, we, be, w1, w3, w2):
    b, s, d = x.shape
    n_tok = b * s
    xt = x.reshape(n_tok, d)
    g_logits = (xt @ wg).astype(jnp.float32) + bg
    g_prob = jax.nn.softmax(g_logits, axis=-1)
    g_idx = jnp.argmax(g_logits, axis=-1).astype(jnp.int32)
    p_group = jnp.take_along_axis(g_prob, g_idx[:, None], axis=1)
    e_logits = jnp.einsum('td,gde->tge', xt, we).astype(jnp.float32) + be
    e_logits = jnp.take_along_axis(e_logits, g_idx[:, None, None], axis=1)[:, 0]
    top_v, top_i = lax.top_k(e_logits, TOP_K)
    gate = jax.nn.softmax(top_v, axis=-1) * p_group
    e_idx = g_idx[:, None] * EXPERTS_PER_GROUP + top_i.astype(jnp.int32)
    n_assign = n_tok * TOP_K
    n_blocks = (n_assign + N_EXPERTS * (MOE_BLOCK - 1) + MOE_BLOCK - 1) // MOE_BLOCK
    flat_e = e_idx.reshape(-1)
    flat_t = jnp.repeat(jnp.arange(n_tok, dtype=jnp.int32), TOP_K)
    flat_w = gate.reshape(-1)
    order = jnp.argsort(flat_e)
    se, st, sw = flat_e[order], flat_t[order], flat_w[order]
    counts = jnp.bincount(flat_e, length=N_EXPERTS).astype(jnp.int32)
    starts = jnp.cumsum(counts) - counts
    pcounts = (counts + MOE_BLOCK - 1) // MOE_BLOCK * MOE_BLOCK
    pends = jnp.cumsum(pcounts)
    pstarts = pends - pcounts
    dest = pstarts[se] + jnp.arange(n_assign, dtype=jnp.int32) - starts[se]
    n_rows = n_blocks * MOE_BLOCK
    buf_t = jnp.zeros((n_rows,), jnp.int32).at[dest].set(st)
    buf_w = jnp.zeros((n_rows,), jnp.float32).at[dest].set(sw)
    blk_e = jnp.minimum(jnp.searchsorted(pends, jnp.arange(n_blocks, dtype=jnp.int32) * MOE_BLOCK, side='right'), N_EXPERTS - 1)

    def expert_block(args):
        tok, wt, e = args
        xb = xt[tok]
        h = jax.nn.silu(xb @ w1[e]) * (xb @ w3[e])
        return (h @ w2[e]) * wt[:, None].astype(xt.dtype)

    out = lax.map(expert_block, (buf_t.reshape(n_blocks, MOE_BLOCK), buf_w.reshape(n_blocks, MOE_BLOCK), blk_e))
    y = jnp.zeros_like(xt).at[buf_t].add(out.reshape(n_rows, d))
    return y.reshape(b, s, d)


def setup_inputs(seed: int = 0) -> dict:
    key = jax.random.key(seed)
    ks = iter(jax.random.split(key, 40))

    def nrm(shape, scale):
        return jax.random.normal(next(ks), shape, jnp.float32) * scale

    x = nrm((BATCH, SEQ, D_MODEL), 1.0)
    mix_norm = 1.0 + nrm((DEPTH, D_MODEL), 0.02)
    ffn_norm = 1.0 + nrm((DEPTH, D_MODEL), 0.02)
    final_norm = 1.0 + nrm((D_MODEL,), 0.02)
    na, nb = N_A_LAYERS, N_B_LAYERS
    a_w_in = nrm((na, D_MODEL, 2 * D_RNN), D_MODEL ** -0.5)
    a_conv_w = nrm((na, CONV_WIDTH, D_RNN), CONV_WIDTH ** -0.5)
    a_conv_b = nrm((na, D_RNN), 0.01)
    a_gate_a_w = nrm((na, LRU_HEADS, LRU_HEAD_DIM, LRU_HEAD_DIM), LRU_HEAD_DIM ** -0.5)
    a_gate_a_b = nrm((na, LRU_HEADS, LRU_HEAD_DIM), 0.01)
    a_gate_x_w = nrm((na, LRU_HEADS, LRU_HEAD_DIM, LRU_HEAD_DIM), LRU_HEAD_DIM ** -0.5)
    a_gate_x_b = nrm((na, LRU_HEADS, LRU_HEAD_DIM), 0.01)
    u = jax.random.uniform(next(ks), (na, D_RNN), jnp.float32, 0.9, 0.999)
    sg = u ** (1.0 / LRU_C)
    a_lru_L = jnp.log(sg) - jnp.log1p(-sg)
    a_w_out = nrm((na, D_RNN, D_MODEL), D_RNN ** -0.5)
    b_w_in = nrm((nb, D_MODEL, D_SSM), D_MODEL ** -0.5)
    b_lam_re = -0.5 + nrm((nb, SSM_GROUPS, SSM_STATE), 0.01)
    b_lam_im = jnp.pi * jnp.arange(SSM_STATE, dtype=jnp.float32) + nrm((nb, SSM_GROUPS, SSM_STATE), 0.01)
    b_log_dt = jax.random.uniform(next(ks), (nb, SSM_GROUPS), jnp.float32, math.log(DT_MIN), math.log(DT_MAX))
    b_b_re = nrm((nb, SSM_GROUPS, SSM_STATE, SSM_GROUP), (2 * SSM_GROUP) ** -0.5)
    b_b_im = nrm((nb, SSM_GROUPS, SSM_STATE, SSM_GROUP), (2 * SSM_GROUP) ** -0.5)
    b_c_re = nrm((nb, SSM_GROUPS, SSM_GROUP, SSM_STATE), SSM_STATE ** -0.5)
    b_c_im = nrm((nb, SSM_GROUPS, SSM_GROUP, SSM_STATE), SSM_STATE ** -0.5)
    b_d = nrm((nb, D_SSM), 1.0)
    b_w_glu = nrm((nb, D_SSM, 2 * D_MODEL), D_SSM ** -0.5)
    moe_wg = nrm((DEPTH, D_MODEL, N_GROUPS), D_MODEL ** -0.5)
    moe_bg = nrm((DEPTH, N_GROUPS), 0.01)
    moe_we = nrm((DEPTH, N_GROUPS, D_MODEL, EXPERTS_PER_GROUP), D_MODEL ** -0.5)
    moe_be = nrm((DEPTH, N_GROUPS, EXPERTS_PER_GROUP), 0.01)
    moe_w1 = nrm((DEPTH, N_EXPERTS, D_MODEL, D_EXPERT), D_MODEL ** -0.5)
    moe_w3 = nrm((DEPTH, N_EXPERTS, D_MODEL, D_EXPERT), D_MODEL ** -0.5)
    moe_w2 = nrm((DEPTH, N_EXPERTS, D_EXPERT, D_MODEL), D_EXPERT ** -0.5)
    return {"x": x, "mix_norm": mix_norm, "ffn_norm": ffn_norm, "final_norm": final_norm,
            "a_w_in": a_w_in, "a_conv_w": a_conv_w, "a_conv_b": a_conv_b,
            "a_gate_a_w": a_gate_a_w, "a_gate_a_b": a_gate_a_b, "a_gate_x_w": a_gate_x_w,
            "a_gate_x_b": a_gate_x_b, "a_lru_L": a_lru_L, "a_w_out": a_w_out,
            "b_w_in": b_w_in, "b_lam_re": b_lam_re, "b_lam_im": b_lam_im, "b_log_dt": b_log_dt,
            "b_b_re": b_b_re, "b_b_im": b_b_im, "b_c_re": b_c_re, "b_c_im": b_c_im,
            "b_d": b_d, "b_w_glu": b_w_glu,
            "moe_wg": moe_wg, "moe_bg": moe_bg, "moe_we": moe_we, "moe_be": moe_be,
            "moe_w1": moe_w1, "moe_w3": moe_w3, "moe_w2": moe_w2}


def reference(x, mix_norm, ffn_norm, final_norm,
              a_w_in, a_conv_w, a_conv_b, a_gate_a_w, a_gate_a_b, a_gate_x_w, a_gate_x_b, a_lru_L, a_w_out,
              b_w_in, b_lam_re, b_lam_im, b_log_dt, b_b_re, b_b_im, b_c_re, b_c_im, b_d, b_w_glu,
              moe_wg, moe_bg, moe_we, moe_be, moe_w1, moe_w3, moe_w2):
    for layer in range(DEPTH):
        h = rms_norm(x, mix_norm[layer])
        j = layer // N_MIXERS
        if layer % N_MIXERS == 0:
            h = rglru_block(h, a_w_in[j], a_conv_w[j], a_conv_b[j], a_gate_a_w[j], a_gate_a_b[j],
                            a_gate_x_w[j], a_gate_x_b[j], a_lru_L[j], a_w_out[j])
        else:
            h = s5_block(h, b_w_in[j], b_lam_re[j], b_lam_im[j], b_log_dt[j], b_b_re[j], b_b_im[j],
                         b_c_re[j], b_c_im[j], b_d[j], b_w_glu[j])
        x = x + h
        x = x + hier_moe(rms_norm(x, ffn_norm[layer]), moe_wg[layer], moe_bg[layer], moe_we[layer],
                         moe_be[layer], moe_w1[layer], moe_w3[layer], moe_w2[layer])
    return rms_norm(x, final_norm)
```

```python
import functools
import math

import jax
import jax.numpy as jnp
from jax import lax
from jax.experimental import pallas as pl
from jax.experimental.pallas import tpu as pltpu

F32 = jnp.float32
BF16 = jnp.bfloat16
I32 = jnp.int32

EPS = 1e-6
LRU_C = 8.0
CONV_WIDTH = 4
TOP_K = 2

LANES = 128
SUBLANES = 8
VMEM_LIMIT = 56 * 1024 * 1024

SCAN_TILE = 256
IN_TILE_M = 1024
IN_TILE_N = 512
ROUTE_TILE = 256
GMM_BLOCK = 256
ROUTE_LANES = 128


def _cparams(sem):
    return pltpu.CompilerParams(dimension_semantics=sem, vmem_limit_bytes=VMEM_LIMIT)


def _const_spec(shape):
    nd = len(shape)
    return pl.BlockSpec(shape, lambda *_: (0,) * nd, pipeline_mode=pl.Buffered(1))


def _rms(x, g):
    ms = jnp.mean(x * x, axis=-1, keepdims=True)
    return x * lax.rsqrt(ms + EPS) * g


def _perm_matrix(ts, seg_len, inverse):
    r = lax.broadcasted_iota(I32, (ts, ts), 1 if inverse else 0)
    n = lax.broadcasted_iota(I32, (ts, ts), 0 if inverse else 1)
    return (n == (r & (SUBLANES - 1)) * seg_len + (r >> 3)).astype(BF16)


def _sublane_id(shape):
    return lax.broadcasted_iota(I32, shape, 0) & (SUBLANES - 1)


def _lru_in_kernel(x_ref, g_ref, w_ref, o_ref, xp_ref, *, ts):
    @pl.when(pl.program_id(1) == 0)
    def _():
        p = _perm_matrix(ts, ts // SUBLANES, False)
        for s in range(x_ref.shape[0] // ts):
            rows = slice(s * ts, (s + 1) * ts)
            xn = _rms(x_ref[rows, :], g_ref[...]).astype(BF16)
            xp_ref[rows, :] = jnp.dot(p, xn, preferred_element_type=F32).astype(BF16)

    o_ref[...] = jnp.dot(xp_ref[...], w_ref[...], preferred_element_type=F32)


def _lru_rec_kernel(gate_ref, rec_ref, x_ref, cw_ref, cb_ref, wg_ref, ba_ref, bx_ref, lam_ref,
                    wout_ref, o_ref, ext_ref, a_ref, b_ref, tail_ref, hcar_ref, *, chunk):
    ts, c = rec_ref.shape
    seg_len = ts // SUBLANES
    halo = (CONV_WIDTH - 1) * SUBLANES
    nblk, bw, _ = wg_ref.shape

    @pl.when(pl.program_id(0) == 0)
    def _():
        tail_ref[...] = jnp.zeros_like(tail_ref)
        hcar_ref[...] = jnp.zeros_like(hcar_ref)

    tail = rec_ref[ts - halo:ts, :]
    sub = _sublane_id((halo, c))
    ext_ref[0:halo, :] = jnp.where(sub == 0, pltpu.roll(tail_ref[...], halo - (SUBLANES - 1), 0),
                                   pltpu.roll(tail, 1, 0))
    ext_ref[halo:halo + ts, :] = rec_ref[...]
    tail_ref[...] = tail

    z = -lam_ref[...]
    softplus = jnp.maximum(z, 0.0) + jnp.log1p(jnp.exp(-jnp.abs(z)))

    for blk in range(nblk):
        cs = slice(blk * bw, (blk + 1) * bw)
        conv = cb_ref[:, cs] + cw_ref[0:1, cs] * ext_ref[0:ts, cs]
        for k in range(1, CONV_WIDTH):
            conv = conv + cw_ref[k:k + 1, cs] * ext_ref[k * SUBLANES:k * SUBLANES + ts, cs]
        g = jnp.dot(conv.astype(BF16), wg_ref[blk], preferred_element_type=F32)
        r = jax.nn.sigmoid(g[:, :bw] + ba_ref[:, cs])
        ig = jax.nn.sigmoid(g[:, bw:] + bx_ref[:, cs])
        log_a = (-LRU_C * r) * softplus[:, cs]
        a_ref[:, cs] = jnp.exp(log_a)
        th = jnp.tanh(log_a)
        b_ref[:, cs] = jnp.sqrt(-2.0 * th / (1.0 - th)) * (ig * conv)

    sub8 = _sublane_id((SUBLANES, chunk))
    for ch in range(c // chunk):
        cs = slice(ch * chunk, (ch + 1) * chunk)

        def step(j, carry, cs=cs):
            h, pp = carry
            rows = pl.ds(pl.multiple_of(j * SUBLANES, SUBLANES), SUBLANES)
            a = a_ref[rows, cs]
            h = a * h + b_ref[rows, cs]
            pp = a * pp
            b_ref[rows, cs] = h
            a_ref[rows, cs] = pp
            return h, pp

        hc, pc = lax.fori_loop(0, seg_len, step,
                               (jnp.zeros((SUBLANES, chunk), F32), jnp.ones((SUBLANES, chunk), F32)))
        for d in (1, 2, 4):
            keep = sub8 >= d
            hs = jnp.where(keep, pltpu.roll(hc, d, 0), 0.0)
            ps = jnp.where(keep, pltpu.roll(pc, d, 0), 1.0)
            hc = pc * hs + hc
            pc = pc * ps
        cin = hcar_ref[:, cs]
        end = hc + pc * cin
        seg_in = jnp.where(sub8 >= 1, pltpu.roll(end, 1, 0), cin)
        hcar_ref[:, cs] = end[SUBLANES - 1:SUBLANES, :]

        def fix(j, carry, cs=cs, seg_in=seg_in):
            rows = pl.ds(pl.multiple_of(j * SUBLANES, SUBLANES), SUBLANES)
            b_ref[rows, cs] = b_ref[rows, cs] + a_ref[rows, cs] * seg_in
            return carry

        lax.fori_loop(0, seg_len, fix, 0)

    y = (b_ref[...] * jax.nn.gelu(gate_ref[...])).astype(BF16)
    y = jnp.dot(_perm_matrix(ts, seg_len, True), y, preferred_element_type=F32).astype(BF16)
    o_ref[...] = x_ref[...] + jnp.dot(y, wout_ref[...], preferred_element_type=F32)


def _block_diag(w, per_block):
    h, d, _ = w.shape
    nb = h // per_block
    eye = jnp.eye(per_block, dtype=w.dtype)
    m = jnp.einsum('bhij,hk->bhikj', w.reshape(nb, per_block, d, d), eye)
    return m.reshape(nb, per_block * d, per_block * d)


def _rglru_layer(x, norm_g, w_in, conv_w, conv_b, wa, ba, wx, bx, lru_l, w_out):
    t, d = x.shape
    c = w_out.shape[0]
    heads, hd, _ = wa.shape
    per_block = LANES // math.gcd(hd, LANES)
    ts = SCAN_TILE
    tm = min(IN_TILE_M, t)
    tn = IN_TILE_N

    proj = pl.pallas_call(
        functools.partial(_lru_in_kernel, ts=ts),
        grid=(t // tm, (2 * c) // tn),
        in_specs=[pl.BlockSpec((tm, d), lambda i, j: (i, 0)),
                  pl.BlockSpec((1, d), lambda i, j: (0, 0)),
                  pl.BlockSpec((d, tn), lambda i, j: (0, j))],
        out_specs=pl.BlockSpec((tm, tn), lambda i, j: (i, j)),
        out_shape=jax.ShapeDtypeStruct((t, 2 * c), F32),
        scratch_shapes=[pltpu.VMEM((tm, d), BF16)],
        compiler_params=_cparams(("parallel", "arbitrary")),
        name="lru_in",
    )(x, norm_g.reshape(1, d), w_in.astype(BF16))

    wg = jnp.concatenate([_block_diag(wa, per_block), _block_diag(wx, per_block)], axis=-1).astype(BF16)
    nblk, bw, _ = wg.shape
    halo = (CONV_WIDTH - 1) * SUBLANES
    row = lambda v: v.reshape(1, c)
    return pl.pallas_call(
        functools.partial(_lru_rec_kernel, chunk=bw * 2),
        grid=(t // ts,),
        in_specs=[pl.BlockSpec((ts, c), lambda i: (i, 0)),
                  pl.BlockSpec((ts, c), lambda i: (i, 1)),
                  pl.BlockSpec((ts, d), lambda i: (i, 0)),
                  _const_spec((CONV_WIDTH, c)), _const_spec((1, c)),
                  _const_spec((nblk, bw, 2 * bw)),
                  _const_spec((1, c)), _const_spec((1, c)), _const_spec((1, c)),
                  _const_spec((c, d))],
        out_specs=pl.BlockSpec((ts, d), lambda i: (i, 0)),
        out_shape=jax.ShapeDtypeStruct((t, d), F32),
        scratch_shapes=[pltpu.VMEM((ts + halo, c), F32), pltpu.VMEM((ts, c), F32),
                        pltpu.VMEM((ts, c), F32), pltpu.VMEM((halo, c), F32),
                        pltpu.VMEM((1, c), F32)],
        compiler_params=_cparams(("arbitrary",)),
        name="lru_rec",
    )(proj, proj, x, conv_w, row(conv_b), wg, row(ba), row(bx), row(lru_l), w_out.astype(BF16))


def _s5_disc_kernel(lre_ref, lim_ref, dt_ref, bre_ref, bim_ref, abr_ref, abi_ref, bbr_ref, bbi_ref):
    lre = jnp.minimum(lre_ref[...], -1e-4)
    lim = lim_ref[...]
    dt = jnp.exp(dt_ref[...])
    mag = jnp.exp(lre * dt)
    abr = mag * jnp.cos(lim * dt)
    abi = mag * jnp.sin(lim * dt)
    den = lre * lre + lim * lim
    kr = ((abr - 1.0) * lre + abi * lim) / den
    ki = (abi * lre - (abr - 1.0) * lim) / den
    abr_ref[...] = abr
    abi_ref[...] = abi
    bbr_ref[...] = kr * bre_ref[...] - ki * bim_ref[...]
    bbi_ref[...] = kr * bim_ref[...] + ki * bre_ref[...]


def _s5_kernel(x_ref, g_ref, win_ref, br_ref, bi_ref, cr_ref, ci_ref, d_ref, abr_ref, abi_ref,
               wglu_ref, o_ref, u_ref, z_ref, xr_ref, xi_ref, apr_ref, api_ref, asr_ref, asi_ref,
               carr_ref, cari_ref, *, chunk):
    ts, d_model = x_ref.shape
    seg_len = ts // SUBLANES
    ns = xr_ref.shape[1]
    nb_in, kin, nin = br_ref.shape
    nb_out, kout, nout = cr_ref.shape

    @pl.when(pl.program_id(0) == 0)
    def _():
        carr_ref[...] = jnp.zeros_like(carr_ref)
        cari_ref[...] = jnp.zeros_like(cari_ref)
        ar, ai = abr_ref[...], abi_ref[...]

        def powers(j, cur):
            pr, pi = cur
            apr_ref[pl.ds(j, 1), :] = pr
            api_ref[pl.ds(j, 1), :] = pi
            return pr * ar - pi * ai, pr * ai + pi * ar

        lax.fori_loop(0, seg_len, powers, (ar, ai))
        lr, li = apr_ref[seg_len - 1:seg_len, :], api_ref[seg_len - 1:seg_len, :]

        def seg_powers(s, cur):
            pr, pi = cur
            asr_ref[pl.ds(s, 1), :] = pr
            asi_ref[pl.ds(s, 1), :] = pi
            return pr * lr - pi * li, pr * li + pi * lr

        lax.fori_loop(0, SUBLANES, seg_powers, (lr, li))

    x = x_ref[...]
    xn = _rms(x, g_ref[...]).astype(BF16)
    xp = jnp.dot(_perm_matrix(ts, seg_len, False), xn, preferred_element_type=F32).astype(BF16)
    u = jnp.dot(xp, win_ref[...], preferred_element_type=F32)
    u_ref[...] = u
    ub = u.astype(BF16)
    for blk in range(nb_in):
        uc = ub[:, blk * kin:(blk + 1) * kin]
        xr_ref[:, blk * nin:(blk + 1) * nin] = jnp.dot(uc, br_ref[blk], preferred_element_type=F32)
        xi_ref[:, blk * nin:(blk + 1) * nin] = jnp.dot(uc, bi_ref[blk], preferred_element_type=F32)

    sub8 = _sublane_id((SUBLANES, chunk))
    for ch in range(ns // chunk):
        cs = slice(ch * chunk, (ch + 1) * chunk)
        ar = jnp.broadcast_to(abr_ref[:, cs], (SUBLANES, chunk))
        ai = jnp.broadcast_to(abi_ref[:, cs], (SUBLANES, chunk))

        def step(j, carry, cs=cs, ar=ar, ai=ai):
            sr, si = carry
            rows = pl.ds(pl.multiple_of(j * SUBLANES, SUBLANES), SUBLANES)
            nr = ar * sr - ai * si + xr_ref[rows, cs]
            ni = ar * si + ai * sr + xi_ref[rows, cs]
            xr_ref[rows, cs] = nr
            xi_ref[rows, cs] = ni
            return nr, ni

        zero = jnp.zeros((SUBLANES, chunk), F32)
        er, ei = lax.fori_loop(0, seg_len, step, (zero, zero))
        for d in (1, 2, 4):
            keep = sub8 >= d
            pr, pi = asr_ref[d - 1:d, cs], asi_ref[d - 1:d, cs]
            sr = jnp.where(keep, pltpu.roll(er, d, 0), 0.0)
            si = jnp.where(keep, pltpu.roll(ei, d, 0), 0.0)
            er, ei = er + pr * sr - pi * si, ei + pr * si + pi * sr
        cr, ci = carr_ref[:, cs], cari_ref[:, cs]
        pr, pi = asr_ref[:, cs], asi_ref[:, cs]
        er, ei = er + pr * cr - pi * ci, ei + pr * ci + pi * cr
        inr = jnp.where(sub8 >= 1, pltpu.roll(er, 1, 0), cr)
        ini = jnp.where(sub8 >= 1, pltpu.roll(ei, 1, 0), ci)
        carr_ref[:, cs] = er[SUBLANES - 1:SUBLANES, :]
        cari_ref[:, cs] = ei[SUBLANES - 1:SUBLANES, :]

        def fix(j, carry, cs=cs, inr=inr, ini=ini):
            rows = pl.ds(pl.multiple_of(j * SUBLANES, SUBLANES), SUBLANES)
            pr, pi = apr_ref[pl.ds(j, 1), cs], api_ref[pl.ds(j, 1), cs]
            xr_ref[rows, cs] = xr_ref[rows, cs] + pr * inr - pi * ini
            xi_ref[rows, cs] = xi_ref[rows, cs] + pr * ini + pi * inr
            return carry

        lax.fori_loop(0, seg_len, fix, 0)

    for blk in range(nb_out):
        ks = slice(blk * kout, (blk + 1) * kout)
        ys = slice(blk * nout, (blk + 1) * nout)
        y = jnp.dot(xr_ref[:, ks].astype(BF16), cr_ref[blk], preferred_element_type=F32)
        y = y + jnp.dot(xi_ref[:, ks].astype(BF16), ci_ref[blk], preferred_element_type=F32)
        y = y + d_ref[:, ys] * u_ref[:, ys]
        z_ref[:, ys] = jax.nn.gelu(y).astype(BF16)

    zn = jnp.dot(_perm_matrix(ts, seg_len, True), z_ref[...], preferred_element_type=F32).astype(BF16)
    vg = jnp.dot(zn, wglu_ref[...], preferred_element_type=F32)
    o_ref[...] = x + vg[:, :d_model] * jax.nn.sigmoid(vg[:, d_model:])


def _s5_layer(x, norm_g, w_in, lam_re, lam_im, log_dt, b_re, b_im, c_re, c_im, d_skip, w_glu):
    t, d = x.shape
    groups, p, k = b_re.shape
    d_ssm = groups * k
    ns = groups * p
    ts = SCAN_TILE
    seg_len = ts // SUBLANES

    col = lambda v: v.reshape(ns, 1).astype(F32)
    log_dt_col = jnp.repeat(log_dt, p)
    rb = 512
    abr, abi, bbr, bbi = pl.pallas_call(
        _s5_disc_kernel,
        grid=(ns // rb,),
        in_specs=[pl.BlockSpec((rb, 1), lambda i: (i, 0))] * 3 + [pl.BlockSpec((rb, k), lambda i: (i, 0))] * 2,
        out_specs=[pl.BlockSpec((rb, 1), lambda i: (i, 0))] * 2 + [pl.BlockSpec((rb, k), lambda i: (i, 0))] * 2,
        out_shape=[jax.ShapeDtypeStruct((ns, 1), F32)] * 2 + [jax.ShapeDtypeStruct((ns, k), F32)] * 2,
        compiler_params=_cparams(("parallel",)),
        name="s5_disc",
    )(col(lam_re), col(lam_im), col(log_dt_col), b_re.reshape(ns, k).astype(F32), b_im.reshape(ns, k).astype(F32))

    gb = 256 // k
    nb = groups // gb
    eye = jnp.eye(gb, dtype=F32)

    def in_blocks(bb):
        m = jnp.einsum('cgpk,gh->cgkhp', bb.reshape(nb, gb, p, k), eye)
        return m.reshape(nb, gb * k, gb * p).astype(BF16)

    def out_blocks(cc):
        m = jnp.einsum('cgkp,gh->cgphk', cc.reshape(nb, gb, k, p).astype(F32), eye)
        return m.reshape(nb, gb * p, gb * k).astype(BF16)

    chunk = 512
    return pl.pallas_call(
        functools.partial(_s5_kernel, chunk=chunk),
        grid=(t // ts,),
        in_specs=[pl.BlockSpec((ts, d), lambda i: (i, 0)),
                  _const_spec((1, d)), _const_spec((d, d_ssm)),
                  _const_spec((nb, gb * k, gb * p)), _const_spec((nb, gb * k, gb * p)),
                  _const_spec((nb, gb * p, gb * k)), _const_spec((nb, gb * p, gb * k)),
                  _const_spec((1, d_ssm)), _const_spec((1, ns)), _const_spec((1, ns)),
                  _const_spec((d_ssm, 2 * d))],
        out_specs=pl.BlockSpec((ts, d), lambda i: (i, 0)),
        out_shape=jax.ShapeDtypeStruct((t, d), F32),
        scratch_shapes=[pltpu.VMEM((ts, d_ssm), F32), pltpu.VMEM((ts, d_ssm), BF16),
                        pltpu.VMEM((ts, ns), F32), pltpu.VMEM((ts, ns), F32),
                        pltpu.VMEM((seg_len, ns), F32), pltpu.VMEM((seg_len, ns), F32),
                        pltpu.VMEM((SUBLANES, ns), F32), pltpu.VMEM((SUBLANES, ns), F32),
                        pltpu.VMEM((1, ns), F32), pltpu.VMEM((1, ns), F32)],
        compiler_params=_cparams(("arbitrary",)),
        name="s5_mix",
    )(x, norm_g.reshape(1, d), w_in.astype(BF16), in_blocks(bbr), in_blocks(bbi),
      out_blocks(c_re), out_blocks(-c_im), d_skip.reshape(1, d_ssm).astype(F32),
      abr.reshape(1, ns), abi.reshape(1, ns), w_glu.astype(BF16))


def _route_kernel(x_ref, g_ref, wr_ref, br_ref, slab_ref, cnt_ref, base_ref, *, n_groups, epg):
    @pl.when(pl.program_id(0) == 0)
    def _():
        base_ref[...] = jnp.zeros_like(base_ref)

    tr = x_ref.shape[0]
    xn = _rms(x_ref[...], g_ref[...])
    logits = jnp.dot(xn, wr_ref[...], preferred_element_type=F32,
                     precision=lax.Precision.HIGHEST) + br_ref[...]
    lane = lax.broadcasted_iota(I32, logits.shape, 1)
    neg = jnp.float32(-3.0e38)

    def first_argmax(v):
        m = jnp.max(v, axis=1, keepdims=True)
        return m, jnp.min(jnp.where(v == m, lane, ROUTE_LANES), axis=1, keepdims=True)

    gmask = lane < n_groups
    gmax, gidx = first_argmax(jnp.where(gmask, logits, neg))
    p_group = 1.0 / jnp.sum(jnp.where(gmask, jnp.exp(logits - gmax), 0.0), axis=1, keepdims=True)
    lo = n_groups + gidx * epg
    el = jnp.where((lane >= lo) & (lane < lo + epg), logits, neg)
    m1, i1 = first_argmax(el)
    m2, i2 = first_argmax(jnp.where(lane == i1, neg, el))
    t2 = jnp.exp(m2 - m1)
    w1 = p_group / (1.0 + t2)
    w2 = p_group * t2 / (1.0 + t2)

    hit1, hit2 = lane == i1, lane == i2
    onehot = (hit1 | hit2).astype(BF16)
    r = lax.broadcasted_iota(I32, (tr, tr), 0)
    c = lax.broadcasted_iota(I32, (tr, tr), 1)
    before = jnp.dot((c < r).astype(BF16), onehot, preferred_element_type=F32) + base_ref[0:1, :]
    rank1 = jnp.sum(jnp.where(hit1, before, 0.0), axis=1, keepdims=True)
    rank2 = jnp.sum(jnp.where(hit2, before, 0.0), axis=1, keepdims=True)
    total = base_ref[0:1, :] + jnp.sum(onehot.astype(F32), axis=0, keepdims=True)
    base_ref[...] = jnp.broadcast_to(total, base_ref.shape)
    cnt_ref[...] = jnp.broadcast_to(total, cnt_ref.shape)

    vals = (w1, w2, (i1 - n_groups).astype(F32), (i2 - n_groups).astype(F32), rank1, rank2)
    slab = jnp.zeros(logits.shape, F32)
    for k, v in enumerate(vals):
        slab = jnp.where(lane == k, v, slab)
    slab_ref[...] = slab


def _dispatch_kernel(dest_ref, x_ref, g_ref, xs_ref, xn_ref, sem):
    tr = x_ref.shape[0]
    base = pl.program_id(0) * tr
    xn_ref[...] = _rms(x_ref[...], g_ref[...])

    def row_copy(r, k, slot):
        return pltpu.make_async_copy(xn_ref.at[pl.ds(r, 1)], xs_ref.at[pl.ds(slot, 1)], sem.at[k])

    def issue(r, carry):
        for k in range(TOP_K):
            row_copy(r, k, dest_ref[(base + r) * TOP_K + k]).start()
        return carry

    def drain(r, carry):
        for k in range(TOP_K):
            row_copy(r, k, 0).wait()
        return carry

    lax.fori_loop(0, tr, issue, 0)
    lax.fori_loop(0, tr, drain, 0)


def _gmm_kernel(blk_ref, exp_ref, lo_ref, hi_ref, xs_ref, w1_ref, w3_ref, w2_ref, o_ref,
                w1b_ref, w3b_ref, w2b_ref):
    n = pl.program_id(0)
    prev = jnp.maximum(n - 1, 0)
    new_expert = (n == 0) | (exp_ref[n] != exp_ref[prev])
    new_block = (n == 0) | (blk_ref[n] != blk_ref[prev])

    @pl.when(new_expert)
    def _():
        w1b_ref[...] = w1_ref[...].astype(BF16)
        w3b_ref[...] = w3_ref[...].astype(BF16)
        w2b_ref[...] = w2_ref[...].astype(BF16)

    xb = xs_ref[...].astype(BF16)
    h = jax.nn.silu(jnp.dot(xb, w1b_ref[...], preferred_element_type=F32))
    h = (h * jnp.dot(xb, w3b_ref[...], preferred_element_type=F32)).astype(BF16)
    out = jnp.dot(h, w2b_ref[...], preferred_element_type=F32)
    row = lax.broadcasted_iota(I32, (out.shape[0], 1), 0)
    out = jnp.where((row >= lo_ref[n]) & (row < hi_ref[n]), out, 0.0)

    @pl.when(new_block)
    def _():
        o_ref[...] = out

    @pl.when(jnp.logical_not(new_block))
    def _():
        o_ref[...] = o_ref[...] + out


def _combine_kernel(dest_ref, x_ref, slab_ref, g_ref, eo_ref, o_ref, buf_ref, sem, *, final_norm):
    tr = x_ref.shape[0]
    base = pl.program_id(0) * tr

    def row_copy(r, k, slot):
        return pltpu.make_async_copy(eo_ref.at[pl.ds(slot, 1)], buf_ref.at[k, pl.ds(r, 1)], sem.at[k])

    def issue(r, carry):
        for k in range(TOP_K):
            row_copy(r, k, dest_ref[(base + r) * TOP_K + k]).start()
        return carry

    def drain(r, carry):
        for k in range(TOP_K):
            row_copy(r, k, 0).wait()
        return carry

    lax.fori_loop(0, tr, issue, 0)
    lax.fori_loop(0, tr, drain, 0)
    w = slab_ref[...]
    y = x_ref[...] + w[:, 0:1] * buf_ref[0] + w[:, 1:2] * buf_ref[1]
    if final_norm:
        y = _rms(y, g_ref[...])
    o_ref[...] = y


def _gmm_schedule(counts, n_rows, block):
    n_exp = counts.shape[0]
    n_blocks = n_rows // block
    n_items = n_blocks + n_exp - 1
    ends = jnp.cumsum(counts)
    starts = ends - counts
    first_blk = starts // block
    last_blk = jnp.maximum(ends - 1, 0) // block
    per_exp = jnp.where(counts > 0, last_blk - first_blk + 1, 0)
    item_end = jnp.cumsum(per_exp)
    item_start = item_end - per_exp
    n = jnp.arange(n_items, dtype=I32)
    valid = n < item_end[-1]
    e = jnp.minimum(jnp.searchsorted(item_end, n, side='right'), n_exp - 1).astype(I32)
    blk = first_blk[e] + (n - item_start[e])
    lo = jnp.maximum(starts[e], blk * block) - blk * block
    hi = jnp.minimum(ends[e], (blk + 1) * block) - blk * block
    last = jnp.maximum(item_end[-1] - 1, 0)
    blk = jnp.where(valid, blk, blk[last]).astype(I32)
    e = jnp.where(valid, e, e[last]).astype(I32)
    lo = jnp.where(valid, lo, 0).astype(I32)
    hi = jnp.where(valid, hi, 0).astype(I32)
    return starts, blk, e, lo, hi


def _moe_layer(x, norm_g, wg, bg, we, be, w1, w3, w2, final_g):
    t, d = x.shape
    n_groups, _, epg = we.shape
    n_exp = n_groups * epg
    f = w1.shape[-1]
    tr = ROUTE_TILE
    n_rows = t * TOP_K
    g_row = norm_g.reshape(1, d)

    wr = jnp.concatenate([wg, jnp.transpose(we, (1, 0, 2)).reshape(d, n_exp)], axis=1)
    wr = jnp.pad(wr, ((0, 0), (0, ROUTE_LANES - wr.shape[1]))).astype(F32)
    br = jnp.pad(jnp.concatenate([bg, be.reshape(-1)]), (0, ROUTE_LANES - n_groups - n_exp))

    slab, cnt = pl.pallas_call(
        functools.partial(_route_kernel, n_groups=n_groups, epg=epg),
        grid=(t // tr,),
        in_specs=[pl.BlockSpec((tr, d), lambda i: (i, 0)), _const_spec((1, d)),
                  _const_spec((d, ROUTE_LANES)), _const_spec((1, ROUTE_LANES))],
        out_specs=[pl.BlockSpec((tr, ROUTE_LANES), lambda i: (i, 0)),
                   pl.BlockSpec((SUBLANES, ROUTE_LANES), lambda i: (0, 0))],
        out_shape=[jax.ShapeDtypeStruct((t, ROUTE_LANES), F32),
                   jax.ShapeDtypeStruct((SUBLANES, ROUTE_LANES), F32)],
        scratch_shapes=[pltpu.VMEM((SUBLANES, ROUTE_LANES), F32)],
        compiler_params=_cparams(("arbitrary",)),
        name="moe_route",
    )(x, g_row, wr, br.reshape(1, ROUTE_LANES).astype(F32))

    counts = cnt[0, n_groups:n_groups + n_exp].astype(I32)
    starts, item_blk, item_exp, item_lo, item_hi = _gmm_schedule(counts, n_rows, GMM_BLOCK)
    e_idx = slab[:, 2:2 + TOP_K].astype(I32)
    dest = (starts[e_idx] + slab[:, 4:4 + TOP_K].astype(I32)).reshape(-1)

    xs = pl.pallas_call(
        _dispatch_kernel,
        grid_spec=pltpu.PrefetchScalarGridSpec(
            num_scalar_prefetch=1, grid=(t // tr,),
            in_specs=[pl.BlockSpec((tr, d), lambda i, dst: (i, 0)),
                      pl.BlockSpec((1, d), lambda i, dst: (0, 0))],
            out_specs=pl.BlockSpec(memory_space=pl.ANY),
            scratch_shapes=[pltpu.VMEM((tr, d), F32), pltpu.SemaphoreType.DMA((TOP_K,))]),
        out_shape=jax.ShapeDtypeStruct((n_rows, d), F32),
        compiler_params=_cparams(("arbitrary",)),
        name="moe_dispatch",
    )(dest, x, g_row)

    n_items = item_blk.shape[0]
    w_spec = lambda shape: pl.BlockSpec((None,) + shape, lambda n, blk, ex, lo, hi: (ex[n], 0, 0))
    eo = pl.pallas_call(
        _gmm_kernel,
        grid_spec=pltpu.PrefetchScalarGridSpec(
            num_scalar_prefetch=4, grid=(n_items,),
            in_specs=[pl.BlockSpec((GMM_BLOCK, d), lambda n, blk, ex, lo, hi: (blk[n], 0)),
                      w_spec((d, f)), w_spec((d, f)), w_spec((f, d))],
            out_specs=pl.BlockSpec((GMM_BLOCK, d), lambda n, blk, ex, lo, hi: (blk[n], 0)),
            scratch_shapes=[pltpu.VMEM((d, f), BF16), pltpu.VMEM((d, f), BF16), pltpu.VMEM((f, d), BF16)]),
        out_shape=jax.ShapeDtypeStruct((n_rows, d), F32),
        compiler_params=_cparams(("arbitrary",)),
        name="moe_gmm",
    )(item_blk, item_exp, item_lo, item_hi, xs, w1, w3, w2)

    final_norm = final_g is not None
    g_out = (final_g if final_norm else norm_g).reshape(1, d)
    return pl.pallas_call(
        functools.partial(_combine_kernel, final_norm=final_norm),
        grid_spec=pltpu.PrefetchScalarGridSpec(
            num_scalar_prefetch=1, grid=(t // tr,),
            in_specs=[pl.BlockSpec((tr, d), lambda i, dst: (i, 0)),
                      pl.BlockSpec((tr, ROUTE_LANES), lambda i, dst: (i, 0)),
                      pl.BlockSpec((1, d), lambda i, dst: (0, 0)),
                      pl.BlockSpec(memory_space=pl.ANY)],
            out_specs=pl.BlockSpec((tr, d), lambda i, dst: (i, 0)),
            scratch_shapes=[pltpu.VMEM((TOP_K, tr, d), F32), pltpu.SemaphoreType.DMA((TOP_K,))]),
        out_shape=jax.ShapeDtypeStruct((t, d), F32),
        compiler_params=_cparams(("arbitrary",)),
        name="moe_combine",
    )(dest, x, slab, g_out, eo)


def kernel(x, mix_norm, ffn_norm, final_norm, a_w_in, a_conv_w, a_conv_b, a_gate_a_w, a_gate_a_b, a_gate_x_w, a_gate_x_b, a_lru_L, a_w_out, b_w_in, b_lam_re, b_lam_im, b_log_dt, b_b_re, b_b_im, b_c_re, b_c_im, b_d, b_w_glu, moe_wg, moe_bg, moe_we, moe_be, moe_w1, moe_w3, moe_w2):
    batch, seq, d = x.shape
    depth = mix_norm.shape[0]
    n_mixers = 2
    outs = []
    for b in range(batch):
        h = x[b]
        for layer in range(depth):
            j = layer // n_mixers
            if layer % n_mixers == 0:
                h = _rglru_layer(h, mix_norm[layer], a_w_in[j], a_conv_w[j], a_conv_b[j],
                                 a_gate_a_w[j], a_gate_a_b[j], a_gate_x_w[j], a_gate_x_b[j],
                                 a_lru_L[j], a_w_out[j])
            else:
                h = _s5_layer(h, mix_norm[layer], b_w_in[j], b_lam_re[j], b_lam_im[j], b_log_dt[j],
                              b_b_re[j], b_b_im[j], b_c_re[j], b_c_im[j], b_d[j], b_w_glu[j])
            h = _moe_layer(h, ffn_norm[layer], moe_wg[layer], moe_bg[layer], moe_we[layer],
                           moe_be[layer], moe_w1[layer], moe_w3[layer], moe_w2[layer],
                           final_norm if layer == depth - 1 else None)
        outs.append(h)
    return jnp.stack(outs)
```

```python
import functools
import math

import jax
import jax.numpy as jnp
from jax import lax
from jax.experimental import pallas as pl
from jax.experimental.pallas import tpu as pltpu

F32 = jnp.float32
BF16 = jnp.bfloat16
I32 = jnp.int32

EPS = 1e-6
LRU_C = 8.0
CONV_WIDTH = 4
TOP_K = 2

LANES = 128
SUBLANES = 8
VMEM_LIMIT = 56 * 1024 * 1024

SCAN_TILE = 256
IN_TILE_M = 1024
IN_TILE_N = 512
ROUTE_TILE = 256
GMM_BLOCK = 256
ROUTE_LANES = 128
ISSUE_UNROLL = 8


def _cparams(sem):
    return pltpu.CompilerParams(dimension_semantics=sem, vmem_limit_bytes=VMEM_LIMIT)


def _const_spec(shape):
    nd = len(shape)
    return pl.BlockSpec(shape, lambda *_: (0,) * nd, pipeline_mode=pl.Buffered(1))


def _rms(x, g):
    ms = jnp.mean(x * x, axis=-1, keepdims=True)
    return x * lax.rsqrt(ms + EPS) * g


def _perm_matrix(ts, seg_len, inverse):
    r = lax.broadcasted_iota(I32, (ts, ts), 1 if inverse else 0)
    n = lax.broadcasted_iota(I32, (ts, ts), 0 if inverse else 1)
    return (n == (r & (SUBLANES - 1)) * seg_len + (r >> 3)).astype(BF16)


def _sublane_id(shape):
    return lax.broadcasted_iota(I32, shape, 0) & (SUBLANES - 1)


def _lru_in_kernel(x_ref, g_ref, w_ref, o_ref, xp_ref, *, ts):
    @pl.when(pl.program_id(1) == 0)
    def _():
        p = _perm_matrix(ts, ts // SUBLANES, False)
        for s in range(x_ref.shape[0] // ts):
            rows = slice(s * ts, (s + 1) * ts)
            xn = _rms(x_ref[rows, :], g_ref[...]).astype(BF16)
            xp_ref[rows, :] = jnp.dot(p, xn, preferred_element_type=F32).astype(BF16)

    o_ref[...] = jnp.dot(xp_ref[...], w_ref[...], preferred_element_type=F32)


def _lru_rec_kernel(gate_ref, rec_ref, x_ref, cw_ref, cb_ref, wg_ref, ba_ref, bx_ref, lam_ref,
                    wout_ref, o_ref, ext_ref, a_ref, b_ref, tail_ref, hcar_ref, *, chunk):
    ts, c = rec_ref.shape
    seg_len = ts // SUBLANES
    halo = (CONV_WIDTH - 1) * SUBLANES
    nblk, bw, _ = wg_ref.shape

    @pl.when(pl.program_id(0) == 0)
    def _():
        tail_ref[...] = jnp.zeros_like(tail_ref)
        hcar_ref[...] = jnp.zeros_like(hcar_ref)

    tail = rec_ref[ts - halo:ts, :]
    sub = _sublane_id((halo, c))
    ext_ref[0:halo, :] = jnp.where(sub == 0, pltpu.roll(tail_ref[...], halo - (SUBLANES - 1), 0),
                                   pltpu.roll(tail, 1, 0))
    ext_ref[halo:halo + ts, :] = rec_ref[...]
    tail_ref[...] = tail

    z = -lam_ref[...]
    softplus = jnp.maximum(z, 0.0) + jnp.log1p(jnp.exp(-jnp.abs(z)))

    for blk in range(nblk):
        cs = slice(blk * bw, (blk + 1) * bw)
        conv = cb_ref[:, cs] + cw_ref[0:1, cs] * ext_ref[0:ts, cs]
        for k in range(1, CONV_WIDTH):
            conv = conv + cw_ref[k:k + 1, cs] * ext_ref[k * SUBLANES:k * SUBLANES + ts, cs]
        g = jnp.dot(conv.astype(BF16), wg_ref[blk], preferred_element_type=F32)
        r = jax.nn.sigmoid(g[:, :bw] + ba_ref[:, cs])
        ig = jax.nn.sigmoid(g[:, bw:] + bx_ref[:, cs])
        log_a = (-LRU_C * r) * softplus[:, cs]
        a_ref[:, cs] = jnp.exp(log_a)
        th = jnp.tanh(log_a)
        b_ref[:, cs] = jnp.sqrt(-2.0 * th / (1.0 - th)) * (ig * conv)

    sub8 = _sublane_id((SUBLANES, chunk))
    for ch in range(c // chunk):
        cs = slice(ch * chunk, (ch + 1) * chunk)

        def step(j, carry, cs=cs):
            h, pp = carry
            rows = pl.ds(pl.multiple_of(j * SUBLANES, SUBLANES), SUBLANES)
            a = a_ref[rows, cs]
            h = a * h + b_ref[rows, cs]
            pp = a * pp
            b_ref[rows, cs] = h
            a_ref[rows, cs] = pp
            return h, pp

        hc, pc = lax.fori_loop(0, seg_len, step,
                               (jnp.zeros((SUBLANES, chunk), F32), jnp.ones((SUBLANES, chunk), F32)))
        for d in (1, 2, 4):
            keep = sub8 >= d
            hs = jnp.where(keep, pltpu.roll(hc, d, 0), 0.0)
            ps = jnp.where(keep, pltpu.roll(pc, d, 0), 1.0)
            hc = pc * hs + hc
            pc = pc * ps
        cin = hcar_ref[:, cs]
        end = hc + pc * cin
        seg_in = jnp.where(sub8 >= 1, pltpu.roll(end, 1, 0), cin)
        hcar_ref[:, cs] = end[SUBLANES - 1:SUBLANES, :]

        def fix(j, carry, cs=cs, seg_in=seg_in):
            rows = pl.ds(pl.multiple_of(j * SUBLANES, SUBLANES), SUBLANES)
            b_ref[rows, cs] = b_ref[rows, cs] + a_ref[rows, cs] * seg_in
            return carry

        lax.fori_loop(0, seg_len, fix, 0)

    y = (b_ref[...] * jax.nn.gelu(gate_ref[...])).astype(BF16)
    y = jnp.dot(_perm_matrix(ts, seg_len, True), y, preferred_element_type=F32).astype(BF16)
    o_ref[...] = x_ref[...] + jnp.dot(y, wout_ref[...], preferred_element_type=F32)


def _block_diag(w, per_block):
    h, d, _ = w.shape
    nb = h // per_block
    eye = jnp.eye(per_block, dtype=w.dtype)
    m = jnp.einsum('bhij,hk->bhikj', w.reshape(nb, per_block, d, d), eye)
    return m.reshape(nb, per_block * d, per_block * d)


def _rglru_layer(x, norm_g, w_in, conv_w, conv_b, wa, ba, wx, bx, lru_l, w_out):
    t, d = x.shape
    c = w_out.shape[0]
    heads, hd, _ = wa.shape
    per_block = LANES // math.gcd(hd, LANES)
    ts = SCAN_TILE
    tm = min(IN_TILE_M, t)
    tn = IN_TILE_N

    proj = pl.pallas_call(
        functools.partial(_lru_in_kernel, ts=ts),
        grid=(t // tm, (2 * c) // tn),
        in_specs=[pl.BlockSpec((tm, d), lambda i, j: (i, 0)),
                  pl.BlockSpec((1, d), lambda i, j: (0, 0)),
                  pl.BlockSpec((d, tn), lambda i, j: (0, j))],
        out_specs=pl.BlockSpec((tm, tn), lambda i, j: (i, j)),
        out_shape=jax.ShapeDtypeStruct((t, 2 * c), F32),
        scratch_shapes=[pltpu.VMEM((tm, d), BF16)],
        compiler_params=_cparams(("parallel", "arbitrary")),
        name="lru_in",
    )(x, norm_g.reshape(1, d), w_in.astype(BF16))

    wg = jnp.concatenate([_block_diag(wa, per_block), _block_diag(wx, per_block)], axis=-1).astype(BF16)
    nblk, bw, _ = wg.shape
    halo = (CONV_WIDTH - 1) * SUBLANES
    row = lambda v: v.reshape(1, c)
    return pl.pallas_call(
        functools.partial(_lru_rec_kernel, chunk=bw * 2),
        grid=(t // ts,),
        in_specs=[pl.BlockSpec((ts, c), lambda i: (i, 0)),
                  pl.BlockSpec((ts, c), lambda i: (i, 1)),
                  pl.BlockSpec((ts, d), lambda i: (i, 0)),
                  _const_spec((CONV_WIDTH, c)), _const_spec((1, c)),
                  _const_spec((nblk, bw, 2 * bw)),
                  _const_spec((1, c)), _const_spec((1, c)), _const_spec((1, c)),
                  _const_spec((c, d))],
        out_specs=pl.BlockSpec((ts, d), lambda i: (i, 0)),
        out_shape=jax.ShapeDtypeStruct((t, d), F32),
        scratch_shapes=[pltpu.VMEM((ts + halo, c), F32), pltpu.VMEM((ts, c), F32),
                        pltpu.VMEM((ts, c), F32), pltpu.VMEM((halo, c), F32),
                        pltpu.VMEM((1, c), F32)],
        compiler_params=_cparams(("arbitrary",)),
        name="lru_rec",
    )(proj, proj, x, conv_w, row(conv_b), wg, row(ba), row(bx), row(lru_l), w_out.astype(BF16))


def _s5_disc_kernel(lre_ref, lim_ref, dt_ref, bre_ref, bim_ref, abr_ref, abi_ref, bbr_ref, bbi_ref):
    lre = jnp.minimum(lre_ref[...], -1e-4)
    lim = lim_ref[...]
    dt = jnp.exp(dt_ref[...])
    mag = jnp.exp(lre * dt)
    abr = mag * jnp.cos(lim * dt)
    abi = mag * jnp.sin(lim * dt)
    den = lre * lre + lim * lim
    kr = ((abr - 1.0) * lre + abi * lim) / den
    ki = (abi * lre - (abr - 1.0) * lim) / den
    abr_ref[...] = abr
    abi_ref[...] = abi
    bbr_ref[...] = kr * bre_ref[...] - ki * bim_ref[...]
    bbi_ref[...] = kr * bim_ref[...] + ki * bre_ref[...]


def _s5_kernel(x_ref, g_ref, win_ref, br_ref, bi_ref, cr_ref, ci_ref, d_ref, abr_ref, abi_ref,
               wglu_ref, o_ref, u_ref, z_ref, xr_ref, xi_ref, apr_ref, api_ref, asr_ref, asi_ref,
               carr_ref, cari_ref, *, chunk):
    ts, d_model = x_ref.shape
    seg_len = ts // SUBLANES
    ns = xr_ref.shape[1]
    nb_in, kin, nin = br_ref.shape
    nb_out, kout, nout = cr_ref.shape

    @pl.when(pl.program_id(0) == 0)
    def _():
        carr_ref[...] = jnp.zeros_like(carr_ref)
        cari_ref[...] = jnp.zeros_like(cari_ref)
        ar, ai = abr_ref[...], abi_ref[...]

        def powers(j, cur):
            pr, pi = cur
            apr_ref[pl.ds(j, 1), :] = pr
            api_ref[pl.ds(j, 1), :] = pi
            return pr * ar - pi * ai, pr * ai + pi * ar

        lax.fori_loop(0, seg_len, powers, (ar, ai))
        lr, li = apr_ref[seg_len - 1:seg_len, :], api_ref[seg_len - 1:seg_len, :]

        def seg_powers(s, cur):
            pr, pi = cur
            asr_ref[pl.ds(s, 1), :] = pr
            asi_ref[pl.ds(s, 1), :] = pi
            return pr * lr - pi * li, pr * li + pi * lr

        lax.fori_loop(0, SUBLANES, seg_powers, (lr, li))

    x = x_ref[...]
    xn = _rms(x, g_ref[...]).astype(BF16)
    xp = jnp.dot(_perm_matrix(ts, seg_len, False), xn, preferred_element_type=F32).astype(BF16)
    u = jnp.dot(xp, win_ref[...], preferred_element_type=F32)
    u_ref[...] = u
    ub = u.astype(BF16)
    for blk in range(nb_in):
        uc = ub[:, blk * kin:(blk + 1) * kin]
        xr_ref[:, blk * nin:(blk + 1) * nin] = jnp.dot(uc, br_ref[blk], preferred_element_type=F32)
        xi_ref[:, blk * nin:(blk + 1) * nin] = jnp.dot(uc, bi_ref[blk], preferred_element_type=F32)

    sub8 = _sublane_id((SUBLANES, chunk))
    for ch in range(ns // chunk):
        cs = slice(ch * chunk, (ch + 1) * chunk)
        ar = jnp.broadcast_to(abr_ref[:, cs], (SUBLANES, chunk))
        ai = jnp.broadcast_to(abi_ref[:, cs], (SUBLANES, chunk))

        def step(j, carry, cs=cs, ar=ar, ai=ai):
            sr, si = carry
            rows = pl.ds(pl.multiple_of(j * SUBLANES, SUBLANES), SUBLANES)
            nr = ar * sr - ai * si + xr_ref[rows, cs]
            ni = ar * si + ai * sr + xi_ref[rows, cs]
            xr_ref[rows, cs] = nr
            xi_ref[rows, cs] = ni
            return nr, ni

        zero = jnp.zeros((SUBLANES, chunk), F32)
        er, ei = lax.fori_loop(0, seg_len, step, (zero, zero))
        for d in (1, 2, 4):
            keep = sub8 >= d
            pr, pi = asr_ref[d - 1:d, cs], asi_ref[d - 1:d, cs]
            sr = jnp.where(keep, pltpu.roll(er, d, 0), 0.0)
            si = jnp.where(keep, pltpu.roll(ei, d, 0), 0.0)
            er, ei = er + pr * sr - pi * si, ei + pr * si + pi * sr
        cr, ci = carr_ref[:, cs], cari_ref[:, cs]
        pr, pi = asr_ref[:, cs], asi_ref[:, cs]
        er, ei = er + pr * cr - pi * ci, ei + pr * ci + pi * cr
        inr = jnp.where(sub8 >= 1, pltpu.roll(er, 1, 0), cr)
        ini = jnp.where(sub8 >= 1, pltpu.roll(ei, 1, 0), ci)
        carr_ref[:, cs] = er[SUBLANES - 1:SUBLANES, :]
        cari_ref[:, cs] = ei[SUBLANES - 1:SUBLANES, :]

        def fix(j, carry, cs=cs, inr=inr, ini=ini):
            rows = pl.ds(pl.multiple_of(j * SUBLANES, SUBLANES), SUBLANES)
            pr, pi = apr_ref[pl.ds(j, 1), cs], api_ref[pl.ds(j, 1), cs]
            xr_ref[rows, cs] = xr_ref[rows, cs] + pr * inr - pi * ini
            xi_ref[rows, cs] = xi_ref[rows, cs] + pr * ini + pi * inr
            return carry

        lax.fori_loop(0, seg_len, fix, 0)

    for blk in range(nb_out):
        ks = slice(blk * kout, (blk + 1) * kout)
        ys = slice(blk * nout, (blk + 1) * nout)
        y = jnp.dot(xr_ref[:, ks].astype(BF16), cr_ref[blk], preferred_element_type=F32)
        y = y + jnp.dot(xi_ref[:, ks].astype(BF16), ci_ref[blk], preferred_element_type=F32)
        y = y + d_ref[:, ys] * u_ref[:, ys]
        z_ref[:, ys] = jax.nn.gelu(y).astype(BF16)

    zn = jnp.dot(_perm_matrix(ts, seg_len, True), z_ref[...], preferred_element_type=F32).astype(BF16)
    vg = jnp.dot(zn, wglu_ref[...], preferred_element_type=F32)
    o_ref[...] = x + vg[:, :d_model] * jax.nn.sigmoid(vg[:, d_model:])


def _s5_layer(x, norm_g, w_in, lam_re, lam_im, log_dt, b_re, b_im, c_re, c_im, d_skip, w_glu):
    t, d = x.shape
    groups, p, k = b_re.shape
    d_ssm = groups * k
    ns = groups * p
    ts = SCAN_TILE
    seg_len = ts // SUBLANES

    col = lambda v: v.reshape(ns, 1).astype(F32)
    log_dt_col = jnp.repeat(log_dt, p)
    rb = 512
    abr, abi, bbr, bbi = pl.pallas_call(
        _s5_disc_kernel,
        grid=(ns // rb,),
        in_specs=[pl.BlockSpec((rb, 1), lambda i: (i, 0))] * 3 + [pl.BlockSpec((rb, k), lambda i: (i, 0))] * 2,
        out_specs=[pl.BlockSpec((rb, 1), lambda i: (i, 0))] * 2 + [pl.BlockSpec((rb, k), lambda i: (i, 0))] * 2,
        out_shape=[jax.ShapeDtypeStruct((ns, 1), F32)] * 2 + [jax.ShapeDtypeStruct((ns, k), F32)] * 2,
        compiler_params=_cparams(("parallel",)),
        name="s5_disc",
    )(col(lam_re), col(lam_im), col(log_dt_col), b_re.reshape(ns, k).astype(F32), b_im.reshape(ns, k).astype(F32))

    gb = 256 // k
    nb = groups // gb
    eye = jnp.eye(gb, dtype=F32)

    def in_blocks(bb):
        m = jnp.einsum('cgpk,gh->cgkhp', bb.reshape(nb, gb, p, k), eye)
        return m.reshape(nb, gb * k, gb * p).astype(BF16)

    def out_blocks(cc):
        m = jnp.einsum('cgkp,gh->cgphk', cc.reshape(nb, gb, k, p).astype(F32), eye)
        return m.reshape(nb, gb * p, gb * k).astype(BF16)

    chunk = 512
    return pl.pallas_call(
        functools.partial(_s5_kernel, chunk=chunk),
        grid=(t // ts,),
        in_specs=[pl.BlockSpec((ts, d), lambda i: (i, 0)),
                  _const_spec((1, d)), _const_spec((d, d_ssm)),
                  _const_spec((nb, gb * k, gb * p)), _const_spec((nb, gb * k, gb * p)),
                  _const_spec((nb, gb * p, gb * k)), _const_spec((nb, gb * p, gb * k)),
                  _const_spec((1, d_ssm)), _const_spec((1, ns)), _const_spec((1, ns)),
                  _const_spec((d_ssm, 2 * d))],
        out_specs=pl.BlockSpec((ts, d), lambda i: (i, 0)),
        out_shape=jax.ShapeDtypeStruct((t, d), F32),
        scratch_shapes=[pltpu.VMEM((ts, d_ssm), F32), pltpu.VMEM((ts, d_ssm), BF16),
                        pltpu.VMEM((ts, ns), F32), pltpu.VMEM((ts, ns), F32),
                        pltpu.VMEM((seg_len, ns), F32), pltpu.VMEM((seg_len, ns), F32),
                        pltpu.VMEM((SUBLANES, ns), F32), pltpu.VMEM((SUBLANES, ns), F32),
                        pltpu.VMEM((1, ns), F32), pltpu.VMEM((1, ns), F32)],
        compiler_params=_cparams(("arbitrary",)),
        name="s5_mix",
    )(x, norm_g.reshape(1, d), w_in.astype(BF16), in_blocks(bbr), in_blocks(bbi),
      out_blocks(c_re), out_blocks(-c_im), d_skip.reshape(1, d_ssm).astype(F32),
      abr.reshape(1, ns), abi.reshape(1, ns), w_glu.astype(BF16))


def _route_kernel(x_ref, g_ref, wr_ref, br_ref, slab_ref, cnt_ref, base_ref, *, n_groups, epg):
    @pl.when(pl.program_id(0) == 0)
    def _():
        base_ref[...] = jnp.zeros_like(base_ref)

    tr = x_ref.shape[0]
    xn = _rms(x_ref[...], g_ref[...])
    x_hi = xn.astype(BF16)
    x_lo = (xn - x_hi.astype(F32)).astype(BF16)
    hi = jnp.dot(x_hi, wr_ref[...], preferred_element_type=F32)
    lo = jnp.dot(x_lo, wr_ref[:, :ROUTE_LANES], preferred_element_type=F32)
    logits = hi[:, :ROUTE_LANES] + (hi[:, ROUTE_LANES:] + lo) + br_ref[...]
    lane = lax.broadcasted_iota(I32, logits.shape, 1)
    neg = jnp.float32(-3.0e38)

    def first_argmax(v):
        m = jnp.max(v, axis=1, keepdims=True)
        return m, jnp.min(jnp.where(v == m, lane, ROUTE_LANES), axis=1, keepdims=True)

    gmask = lane < n_groups
    gmax, gidx = first_argmax(jnp.where(gmask, logits, neg))
    p_group = 1.0 / jnp.sum(jnp.where(gmask, jnp.exp(logits - gmax), 0.0), axis=1, keepdims=True)
    lo = n_groups + gidx * epg
    el = jnp.where((lane >= lo) & (lane < lo + epg), logits, neg)
    m1, i1 = first_argmax(el)
    m2, i2 = first_argmax(jnp.where(lane == i1, neg, el))
    t2 = jnp.exp(m2 - m1)
    w1 = p_group / (1.0 + t2)
    w2 = p_group * t2 / (1.0 + t2)

    hit1, hit2 = lane == i1, lane == i2
    onehot = (hit1 | hit2).astype(BF16)
    r = lax.broadcasted_iota(I32, (tr, tr), 0)
    c = lax.broadcasted_iota(I32, (tr, tr), 1)
    before = jnp.dot((c < r).astype(BF16), onehot, preferred_element_type=F32) + base_ref[0:1, :]
    rank1 = jnp.sum(jnp.where(hit1, before, 0.0), axis=1, keepdims=True)
    rank2 = jnp.sum(jnp.where(hit2, before, 0.0), axis=1, keepdims=True)
    total = base_ref[0:1, :] + jnp.sum(onehot.astype(F32), axis=0, keepdims=True)
    base_ref[...] = jnp.broadcast_to(total, base_ref.shape)
    cnt_ref[...] = jnp.broadcast_to(total, cnt_ref.shape)

    vals = (w1, w2, (i1 - n_groups).astype(F32), (i2 - n_groups).astype(F32), rank1, rank2)
    slab = jnp.zeros(logits.shape, F32)
    for k, v in enumerate(vals):
        slab = jnp.where(lane == k, v, slab)
    slab_ref[...] = slab


def _dispatch_kernel(dest_ref, x_ref, g_ref, xs_ref, xn_ref, sem):
    tr = x_ref.shape[0]
    base = pl.program_id(0) * tr
    xn_ref[...] = _rms(x_ref[...], g_ref[...])

    def row_copy(r, k, slot):
        return pltpu.make_async_copy(xn_ref.at[pl.ds(r, 1)], xs_ref.at[pl.ds(slot, 1)], sem.at[k])

    def issue(r, carry):
        for k in range(TOP_K):
            row_copy(r, k, dest_ref[(base + r) * TOP_K + k]).start()
        return carry

    lax.fori_loop(0, tr, issue, 0, unroll=ISSUE_UNROLL)
    for k in range(TOP_K):
        pltpu.make_async_copy(xn_ref, xs_ref.at[pl.ds(0, tr)], sem.at[k]).wait()


def _gmm_kernel(blk_ref, exp_ref, lo_ref, hi_ref, xs_ref, w1_ref, w3_ref, w2_ref, o_ref,
                w1b_ref, w3b_ref, w2b_ref):
    n = pl.program_id(0)
    prev = jnp.maximum(n - 1, 0)
    new_expert = (n == 0) | (exp_ref[n] != exp_ref[prev])
    new_block = (n == 0) | (blk_ref[n] != blk_ref[prev])

    @pl.when(new_expert)
    def _():
        w1b_ref[...] = w1_ref[...].astype(BF16)
        w3b_ref[...] = w3_ref[...].astype(BF16)
        w2b_ref[...] = w2_ref[...].astype(BF16)

    xb = xs_ref[...].astype(BF16)
    h = jax.nn.silu(jnp.dot(xb, w1b_ref[...], preferred_element_type=F32))
    h = (h * jnp.dot(xb, w3b_ref[...], preferred_element_type=F32)).astype(BF16)
    out = jnp.dot(h, w2b_ref[...], preferred_element_type=F32)
    row = lax.broadcasted_iota(I32, (out.shape[0], 1), 0)
    out = jnp.where((row >= lo_ref[n]) & (row < hi_ref[n]), out, 0.0)

    @pl.when(new_block)
    def _():
        o_ref[...] = out

    @pl.when(jnp.logical_not(new_block))
    def _():
        o_ref[...] = o_ref[...] + out


def _combine_kernel(dest_ref, x_ref, slab_ref, g_ref, eo_ref, o_ref, buf_ref, sem, *, final_norm):
    tr = x_ref.shape[0]
    base = pl.program_id(0) * tr

    def row_copy(r, k, slot):
        return pltpu.make_async_copy(eo_ref.at[pl.ds(slot, 1)], buf_ref.at[k, pl.ds(r, 1)], sem.at[k])

    def issue(r, carry):
        for k in range(TOP_K):
            row_copy(r, k, dest_ref[(base + r) * TOP_K + k]).start()
        return carry

    lax.fori_loop(0, tr, issue, 0, unroll=ISSUE_UNROLL)
    for k in range(TOP_K):
        pltpu.make_async_copy(eo_ref.at[pl.ds(0, tr)], buf_ref.at[k], sem.at[k]).wait()
    w = slab_ref[...]
    y = x_ref[...] + w[:, 0:1] * buf_ref[0] + w[:, 1:2] * buf_ref[1]
    if final_norm:
        y = _rms(y, g_ref[...])
    o_ref[...] = y


def _gmm_schedule(counts, n_rows, block):
    n_exp = counts.shape[0]
    n_blocks = n_rows // block
    n_items = n_blocks + n_exp - 1
    ends = jnp.cumsum(counts)
    starts = ends - counts
    first_blk = starts // block
    last_blk = jnp.maximum(ends - 1, 0) // block
    per_exp = jnp.where(counts > 0, last_blk - first_blk + 1, 0)
    item_end = jnp.cumsum(per_exp)
    item_start = item_end - per_exp
    n = jnp.arange(n_items, dtype=I32)
    valid = n < item_end[-1]
    e = jnp.minimum(jnp.sum(item_end[None, :] <= n[:, None], axis=1), n_exp - 1).astype(I32)
    blk = first_blk[e] + (n - item_start[e])
    lo = jnp.maximum(starts[e], blk * block) - blk * block
    hi = jnp.minimum(ends[e], (blk + 1) * block) - blk * block
    last = jnp.maximum(item_end[-1] - 1, 0)
    blk = jnp.where(valid, blk, blk[last]).astype(I32)
    e = jnp.where(valid, e, e[last]).astype(I32)
    lo = jnp.where(valid, lo, 0).astype(I32)
    hi = jnp.where(valid, hi, 0).astype(I32)
    return starts, blk, e, lo, hi


def _moe_layer(x, norm_g, wg, bg, we, be, w1, w3, w2, layer, final_g):
    t, d = x.shape
    n_groups, _, epg = we.shape
    n_exp = n_groups * epg
    f = w1.shape[-1]
    tr = ROUTE_TILE
    n_rows = t * TOP_K
    g_row = norm_g.reshape(1, d)

    wr = jnp.concatenate([wg, jnp.transpose(we, (1, 0, 2)).reshape(d, n_exp)], axis=1)
    wr = jnp.pad(wr, ((0, 0), (0, ROUTE_LANES - wr.shape[1]))).astype(F32)
    wr_hi = wr.astype(BF16)
    wr = jnp.concatenate([wr_hi, (wr - wr_hi.astype(F32)).astype(BF16)], axis=1)
    br = jnp.pad(jnp.concatenate([bg, be.reshape(-1)]), (0, ROUTE_LANES - n_groups - n_exp))

    slab, cnt = pl.pallas_call(
        functools.partial(_route_kernel, n_groups=n_groups, epg=epg),
        grid=(t // tr,),
        in_specs=[pl.BlockSpec((tr, d), lambda i: (i, 0)), _const_spec((1, d)),
                  _const_spec((d, 2 * ROUTE_LANES)), _const_spec((1, ROUTE_LANES))],
        out_specs=[pl.BlockSpec((tr, ROUTE_LANES), lambda i: (i, 0)),
                   pl.BlockSpec((SUBLANES, ROUTE_LANES), lambda i: (0, 0))],
        out_shape=[jax.ShapeDtypeStruct((t, ROUTE_LANES), F32),
                   jax.ShapeDtypeStruct((SUBLANES, ROUTE_LANES), F32)],
        scratch_shapes=[pltpu.VMEM((SUBLANES, ROUTE_LANES), F32)],
        compiler_params=_cparams(("arbitrary",)),
        name="moe_route",
    )(x, g_row, wr, br.reshape(1, ROUTE_LANES).astype(F32))

    counts = cnt[0, n_groups:n_groups + n_exp].astype(I32)
    starts, item_blk, item_exp, item_lo, item_hi = _gmm_schedule(counts, n_rows, GMM_BLOCK)
    e_idx = slab[:, 2:2 + TOP_K].astype(I32)
    start_of = jnp.sum(jnp.where(e_idx[..., None] == jnp.arange(n_exp, dtype=I32), starts, 0), axis=-1)
    dest = (start_of + slab[:, 4:4 + TOP_K].astype(I32)).reshape(-1)

    xs = pl.pallas_call(
        _dispatch_kernel,
        grid_spec=pltpu.PrefetchScalarGridSpec(
            num_scalar_prefetch=1, grid=(t // tr,),
            in_specs=[pl.BlockSpec((tr, d), lambda i, dst: (i, 0)),
                      pl.BlockSpec((1, d), lambda i, dst: (0, 0))],
            out_specs=pl.BlockSpec(memory_space=pl.ANY),
            scratch_shapes=[pltpu.VMEM((tr, d), F32), pltpu.SemaphoreType.DMA((TOP_K,))]),
        out_shape=jax.ShapeDtypeStruct((n_rows, d), F32),
        compiler_params=_cparams(("arbitrary",)),
        name="moe_dispatch",
    )(dest, x, g_row)

    n_items = item_blk.shape[0]
    w_spec = lambda shape: pl.BlockSpec((None, None) + shape,
                                        lambda n, blk, ex, lo, hi: (layer, ex[n], 0, 0))
    eo = pl.pallas_call(
        _gmm_kernel,
        grid_spec=pltpu.PrefetchScalarGridSpec(
            num_scalar_prefetch=4, grid=(n_items,),
            in_specs=[pl.BlockSpec((GMM_BLOCK, d), lambda n, blk, ex, lo, hi: (blk[n], 0)),
                      w_spec((d, f)), w_spec((d, f)), w_spec((f, d))],
            out_specs=pl.BlockSpec((GMM_BLOCK, d), lambda n, blk, ex, lo, hi: (blk[n], 0)),
            scratch_shapes=[pltpu.VMEM((d, f), BF16), pltpu.VMEM((d, f), BF16), pltpu.VMEM((f, d), BF16)]),
        out_shape=jax.ShapeDtypeStruct((n_rows, d), F32),
        compiler_params=_cparams(("arbitrary",)),
        name="moe_gmm",
    )(item_blk, item_exp, item_lo, item_hi, xs, w1, w3, w2)

    final_norm = final_g is not None
    g_out = (final_g if final_norm else norm_g).reshape(1, d)
    return pl.pallas_call(
        functools.partial(_combine_kernel, final_norm=final_norm),
        grid_spec=pltpu.PrefetchScalarGridSpec(
            num_scalar_prefetch=1, grid=(t // tr,),
            in_specs=[pl.BlockSpec((tr, d), lambda i, dst: (i, 0)),
                      pl.BlockSpec((tr, ROUTE_LANES), lambda i, dst: (i, 0)),
                      pl.BlockSpec((1, d), lambda i, dst: (0, 0)),
                      pl.BlockSpec(memory_space=pl.ANY)],
            out_specs=pl.BlockSpec((tr, d), lambda i, dst: (i, 0)),
            scratch_shapes=[pltpu.VMEM((TOP_K, tr, d), F32), pltpu.SemaphoreType.DMA((TOP_K,))]),
        out_shape=jax.ShapeDtypeStruct((t, d), F32),
        compiler_params=_cparams(("arbitrary",)),
        name="moe_combine",
    )(dest, x, slab, g_out, eo)


def kernel(x, mix_norm, ffn_norm, final_norm, a_w_in, a_conv_w, a_conv_b, a_gate_a_w, a_gate_a_b, a_gate_x_w, a_gate_x_b, a_lru_L, a_w_out, b_w_in, b_lam_re, b_lam_im, b_log_dt, b_b_re, b_b_im, b_c_re, b_c_im, b_d, b_w_glu, moe_wg, moe_bg, moe_we, moe_be, moe_w1, moe_w3, moe_w2):
    batch, seq, d = x.shape
    depth = mix_norm.shape[0]
    n_mixers = 2
    outs = []
    for b in range(batch):
        h = x[b]
        for layer in range(depth):
            j = layer // n_mixers
            if layer % n_mixers == 0:
                h = _rglru_layer(h, mix_norm[layer], a_w_in[j], a_conv_w[j], a_conv_b[j],
                                 a_gate_a_w[j], a_gate_a_b[j], a_gate_x_w[j], a_gate_x_b[j],
                                 a_lru_L[j], a_w_out[j])
            else:
                h = _s5_layer(h, mix_norm[layer], b_w_in[j], b_lam_re[j], b_lam_im[j], b_log_dt[j],
                              b_b_re[j], b_b_im[j], b_c_re[j], b_c_im[j], b_d[j], b_w_glu[j])
            h = _moe_layer(h, ffn_norm[layer], moe_wg[layer], moe_bg[layer], moe_we[layer],
                           moe_be[layer], moe_w1, moe_w3, moe_w2, layer,
                           final_norm if layer == depth - 1 else None)
        outs.append(h)
    return jnp.stack(outs)
```

```python
import functools
import math

import jax
import jax.numpy as jnp
from jax import lax
from jax.experimental import pallas as pl
from jax.experimental.pallas import tpu as pltpu

F32 = jnp.float32
BF16 = jnp.bfloat16
I32 = jnp.int32

EPS = 1e-6
LRU_C = 8.0
CONV_WIDTH = 4
TOP_K = 2

LANES = 128
SUBLANES = 8
VMEM_LIMIT = 56 * 1024 * 1024

SCAN_TILE = 256
IN_TILE_M = 1024
IN_TILE_N = 512
ROUTE_TILE = 256
GMM_BLOCK = 256
ROUTE_LANES = 128


def _cparams(sem):
    return pltpu.CompilerParams(dimension_semantics=sem, vmem_limit_bytes=VMEM_LIMIT)


def _const_spec(shape):
    nd = len(shape)
    return pl.BlockSpec(shape, lambda *_: (0,) * nd, pipeline_mode=pl.Buffered(1))


def _rms(x, g):
    ms = jnp.mean(x * x, axis=-1, keepdims=True)
    return x * lax.rsqrt(ms + EPS) * g


def _perm_matrix(ts, seg_len, inverse):
    r = lax.broadcasted_iota(I32, (ts, ts), 1 if inverse else 0)
    n = lax.broadcasted_iota(I32, (ts, ts), 0 if inverse else 1)
    return (n == (r & (SUBLANES - 1)) * seg_len + (r >> 3)).astype(BF16)


def _sublane_id(shape):
    return lax.broadcasted_iota(I32, shape, 0) & (SUBLANES - 1)


def _lru_in_kernel(x_ref, g_ref, w_ref, o_ref, xp_ref, *, ts):
    @pl.when(pl.program_id(1) == 0)
    def _():
        p = _perm_matrix(ts, ts // SUBLANES, False)
        for s in range(x_ref.shape[0] // ts):
            rows = slice(s * ts, (s + 1) * ts)
            xn = _rms(x_ref[rows, :], g_ref[...]).astype(BF16)
            xp_ref[rows, :] = jnp.dot(p, xn, preferred_element_type=F32).astype(BF16)

    o_ref[...] = jnp.dot(xp_ref[...], w_ref[...], preferred_element_type=F32)


def _lru_rec_kernel(gate_ref, rec_ref, x_ref, cw_ref, cb_ref, wg_ref, ba_ref, bx_ref, lam_ref,
                    wout_ref, o_ref, ext_ref, a_ref, b_ref, tail_ref, hcar_ref, *, chunk):
    ts, c = rec_ref.shape
    seg_len = ts // SUBLANES
    halo = (CONV_WIDTH - 1) * SUBLANES
    nblk, bw, _ = wg_ref.shape

    @pl.when(pl.program_id(0) == 0)
    def _():
        tail_ref[...] = jnp.zeros_like(tail_ref)
        hcar_ref[...] = jnp.zeros_like(hcar_ref)

    tail = rec_ref[ts - halo:ts, :]
    sub = _sublane_id((halo, c))
    ext_ref[0:halo, :] = jnp.where(sub == 0, pltpu.roll(tail_ref[...], halo - (SUBLANES - 1), 0),
                                   pltpu.roll(tail, 1, 0))
    ext_ref[halo:halo + ts, :] = rec_ref[...]
    tail_ref[...] = tail

    z = -lam_ref[...]
    softplus = jnp.maximum(z, 0.0) + jnp.log1p(jnp.exp(-jnp.abs(z)))

    for blk in range(nblk):
        cs = slice(blk * bw, (blk + 1) * bw)
        conv = cb_ref[:, cs] + cw_ref[0:1, cs] * ext_ref[0:ts, cs]
        for k in range(1, CONV_WIDTH):
            conv = conv + cw_ref[k:k + 1, cs] * ext_ref[k * SUBLANES:k * SUBLANES + ts, cs]
        g = jnp.dot(conv.astype(BF16), wg_ref[blk], preferred_element_type=F32)
        r = jax.nn.sigmoid(g[:, :bw] + ba_ref[:, cs])
        ig = jax.nn.sigmoid(g[:, bw:] + bx_ref[:, cs])
        log_a = (-LRU_C * r) * softplus[:, cs]
        a_ref[:, cs] = jnp.exp(log_a)
        th = jnp.tanh(log_a)
        b_ref[:, cs] = jnp.sqrt(-2.0 * th / (1.0 - th)) * (ig * conv)

    sub8 = _sublane_id((SUBLANES, chunk))
    for ch in range(c // chunk):
        cs = slice(ch * chunk, (ch + 1) * chunk)

        def step(j, carry, cs=cs):
            h, pp = carry
            rows = pl.ds(pl.multiple_of(j * SUBLANES, SUBLANES), SUBLANES)
            a = a_ref[rows, cs]
            h = a * h + b_ref[rows, cs]
            pp = a * pp
            b_ref[rows, cs] = h
            a_ref[rows, cs] = pp
            return h, pp

        hc, pc = lax.fori_loop(0, seg_len, step,
                               (jnp.zeros((SUBLANES, chunk), F32), jnp.ones((SUBLANES, chunk), F32)))
        for d in (1, 2, 4):
            keep = sub8 >= d
            hs = jnp.where(keep, pltpu.roll(hc, d, 0), 0.0)
            ps = jnp.where(keep, pltpu.roll(pc, d, 0), 1.0)
            hc = pc * hs + hc
            pc = pc * ps
        cin = hcar_ref[:, cs]
        end = hc + pc * cin
        seg_in = jnp.where(sub8 >= 1, pltpu.roll(end, 1, 0), cin)
        hcar_ref[:, cs] = end[SUBLANES - 1:SUBLANES, :]

        def fix(j, carry, cs=cs, seg_in=seg_in):
            rows = pl.ds(pl.multiple_of(j * SUBLANES, SUBLANES), SUBLANES)
            b_ref[rows, cs] = b_ref[rows, cs] + a_ref[rows, cs] * seg_in
            return carry

        lax.fori_loop(0, seg_len, fix, 0)

    y = (b_ref[...] * jax.nn.gelu(gate_ref[...])).astype(BF16)
    y = jnp.dot(_perm_matrix(ts, seg_len, True), y, preferred_element_type=F32).astype(BF16)
    o_ref[...] = x_ref[...] + jnp.dot(y, wout_ref[...], preferred_element_type=F32)


def _block_diag(w, per_block):
    h, d, _ = w.shape
    nb = h // per_block
    eye = jnp.eye(per_block, dtype=w.dtype)
    m = jnp.einsum('bhij,hk->bhikj', w.reshape(nb, per_block, d, d), eye)
    return m.reshape(nb, per_block * d, per_block * d)


def _rglru_layer(x, norm_g, w_in, conv_w, conv_b, wa, ba, wx, bx, lru_l, w_out):
    t, d = x.shape
    c = w_out.shape[0]
    heads, hd, _ = wa.shape
    per_block = LANES // math.gcd(hd, LANES)
    ts = SCAN_TILE
    tm = min(IN_TILE_M, t)
    tn = IN_TILE_N

    proj = pl.pallas_call(
        functools.partial(_lru_in_kernel, ts=ts),
        grid=(t // tm, (2 * c) // tn),
        in_specs=[pl.BlockSpec((tm, d), lambda i, j: (i, 0)),
                  pl.BlockSpec((1, d), lambda i, j: (0, 0)),
                  pl.BlockSpec((d, tn), lambda i, j: (0, j))],
        out_specs=pl.BlockSpec((tm, tn), lambda i, j: (i, j)),
        out_shape=jax.ShapeDtypeStruct((t, 2 * c), F32),
        scratch_shapes=[pltpu.VMEM((tm, d), BF16)],
        compiler_params=_cparams(("parallel", "arbitrary")),
        name="lru_in",
    )(x, norm_g.reshape(1, d), w_in.astype(BF16))

    wg = jnp.concatenate([_block_diag(wa, per_block), _block_diag(wx, per_block)], axis=-1).astype(BF16)
    nblk, bw, _ = wg.shape
    halo = (CONV_WIDTH - 1) * SUBLANES
    row = lambda v: v.reshape(1, c)
    return pl.pallas_call(
        functools.partial(_lru_rec_kernel, chunk=bw * 2),
        grid=(t // ts,),
        in_specs=[pl.BlockSpec((ts, c), lambda i: (i, 0)),
                  pl.BlockSpec((ts, c), lambda i: (i, 1)),
                  pl.BlockSpec((ts, d), lambda i: (i, 0)),
                  _const_spec((CONV_WIDTH, c)), _const_spec((1, c)),
                  _const_spec((nblk, bw, 2 * bw)),
                  _const_spec((1, c)), _const_spec((1, c)), _const_spec((1, c)),
                  _const_spec((c, d))],
        out_specs=pl.BlockSpec((ts, d), lambda i: (i, 0)),
        out_shape=jax.ShapeDtypeStruct((t, d), F32),
        scratch_shapes=[pltpu.VMEM((ts + halo, c), F32), pltpu.VMEM((ts, c), F32),
                        pltpu.VMEM((ts, c), F32), pltpu.VMEM((halo, c), F32),
                        pltpu.VMEM((1, c), F32)],
        compiler_params=_cparams(("arbitrary",)),
        name="lru_rec",
    )(proj, proj, x, conv_w, row(conv_b), wg, row(ba), row(bx), row(lru_l), w_out.astype(BF16))


def _s5_disc_kernel(lre_ref, lim_ref, dt_ref, bre_ref, bim_ref, abr_ref, abi_ref, bbr_ref, bbi_ref):
    lre = jnp.minimum(lre_ref[...], -1e-4)
    lim = lim_ref[...]
    dt = jnp.exp(dt_ref[...])
    mag = jnp.exp(lre * dt)
    abr = mag * jnp.cos(lim * dt)
    abi = mag * jnp.sin(lim * dt)
    den = lre * lre + lim * lim
    kr = ((abr - 1.0) * lre + abi * lim) / den
    ki = (abi * lre - (abr - 1.0) * lim) / den
    abr_ref[...] = abr
    abi_ref[...] = abi
    bbr_ref[...] = kr * bre_ref[...] - ki * bim_ref[...]
    bbi_ref[...] = kr * bim_ref[...] + ki * bre_ref[...]


def _s5_kernel(dest_ref, x_ref, slab_ref, eo_ref, g_ref, win_ref, br_ref, bi_ref, cr_ref, ci_ref,
               d_ref, abr_ref, abi_ref, wglu_ref, o_ref, u_ref, z_ref, xr_ref, xi_ref, apr_ref,
               api_ref, asr_ref, asi_ref, carr_ref, cari_ref, buf_ref, sem, *, chunk):
    ts, d_model = x_ref.shape
    tile = pl.program_id(0)
    last = pl.num_programs(0) - 1
    slot = tile & 1
    seg_len = ts // SUBLANES
    ns = xr_ref.shape[1]
    nb_in, kin, nin = br_ref.shape
    nb_out, kout, nout = cr_ref.shape

    @pl.when(tile == 0)
    def _():
        _combine_start(dest_ref, eo_ref, buf_ref, sem, 0, 0)
        carr_ref[...] = jnp.zeros_like(carr_ref)
        cari_ref[...] = jnp.zeros_like(cari_ref)
        ar, ai = abr_ref[...], abi_ref[...]

        def powers(j, cur):
            pr, pi = cur
            apr_ref[pl.ds(j, 1), :] = pr
            api_ref[pl.ds(j, 1), :] = pi
            return pr * ar - pi * ai, pr * ai + pi * ar

        lax.fori_loop(0, seg_len, powers, (ar, ai))
        lr, li = apr_ref[seg_len - 1:seg_len, :], api_ref[seg_len - 1:seg_len, :]

        def seg_powers(s, cur):
            pr, pi = cur
            asr_ref[pl.ds(s, 1), :] = pr
            asi_ref[pl.ds(s, 1), :] = pi
            return pr * lr - pi * li, pr * li + pi * lr

        lax.fori_loop(0, SUBLANES, seg_powers, (lr, li))

    _combine_wait(eo_ref, buf_ref, sem, slot)
    x = _combined_tile(x_ref[...], slab_ref[...], buf_ref, slot)
    o_ref[...] = x
    _combine_start(dest_ref, eo_ref, buf_ref, sem, jnp.minimum(tile + 1, last), 1 - slot)
    xn = _rms(x, g_ref[...]).astype(BF16)
    xp = jnp.dot(_perm_matrix(ts, seg_len, False), xn, preferred_element_type=F32).astype(BF16)
    u = jnp.dot(xp, win_ref[...], preferred_element_type=F32)
    u_ref[...] = u
    ub = u.astype(BF16)
    for blk in range(nb_in):
        uc = ub[:, blk * kin:(blk + 1) * kin]
        xr_ref[:, blk * nin:(blk + 1) * nin] = jnp.dot(uc, br_ref[blk], preferred_element_type=F32)
        xi_ref[:, blk * nin:(blk + 1) * nin] = jnp.dot(uc, bi_ref[blk], preferred_element_type=F32)

    sub8 = _sublane_id((SUBLANES, chunk))
    for ch in range(ns // chunk):
        cs = slice(ch * chunk, (ch + 1) * chunk)
        ar = jnp.broadcast_to(abr_ref[:, cs], (SUBLANES, chunk))
        ai = jnp.broadcast_to(abi_ref[:, cs], (SUBLANES, chunk))

        def step(j, carry, cs=cs, ar=ar, ai=ai):
            sr, si = carry
            rows = pl.ds(pl.multiple_of(j * SUBLANES, SUBLANES), SUBLANES)
            nr = ar * sr - ai * si + xr_ref[rows, cs]
            ni = ar * si + ai * sr + xi_ref[rows, cs]
            xr_ref[rows, cs] = nr
            xi_ref[rows, cs] = ni
            return nr, ni

        zero = jnp.zeros((SUBLANES, chunk), F32)
        er, ei = lax.fori_loop(0, seg_len, step, (zero, zero))
        for d in (1, 2, 4):
            keep = sub8 >= d
            pr, pi = asr_ref[d - 1:d, cs], asi_ref[d - 1:d, cs]
            sr = jnp.where(keep, pltpu.roll(er, d, 0), 0.0)
            si = jnp.where(keep, pltpu.roll(ei, d, 0), 0.0)
            er, ei = er + pr * sr - pi * si, ei + pr * si + pi * sr
        cr, ci = carr_ref[:, cs], cari_ref[:, cs]
        pr, pi = asr_ref[:, cs], asi_ref[:, cs]
        er, ei = er + pr * cr - pi * ci, ei + pr * ci + pi * cr
        inr = jnp.where(sub8 >= 1, pltpu.roll(er, 1, 0), cr)
        ini = jnp.where(sub8 >= 1, pltpu.roll(ei, 1, 0), ci)
        carr_ref[:, cs] = er[SUBLANES - 1:SUBLANES, :]
        cari_ref[:, cs] = ei[SUBLANES - 1:SUBLANES, :]

        def fix(j, carry, cs=cs, inr=inr, ini=ini):
            rows = pl.ds(pl.multiple_of(j * SUBLANES, SUBLANES), SUBLANES)
            pr, pi = apr_ref[pl.ds(j, 1), cs], api_ref[pl.ds(j, 1), cs]
            xr_ref[rows, cs] = xr_ref[rows, cs] + pr * inr - pi * ini
            xi_ref[rows, cs] = xi_ref[rows, cs] + pr * ini + pi * inr
            return carry

        lax.fori_loop(0, seg_len, fix, 0)

    for blk in range(nb_out):
        ks = slice(blk * kout, (blk + 1) * kout)
        ys = slice(blk * nout, (blk + 1) * nout)
        y = jnp.dot(xr_ref[:, ks].astype(BF16), cr_ref[blk], preferred_element_type=F32)
        y = y + jnp.dot(xi_ref[:, ks].astype(BF16), ci_ref[blk], preferred_element_type=F32)
        y = y + d_ref[:, ys] * u_ref[:, ys]
        z_ref[:, ys] = jax.nn.gelu(y).astype(BF16)

    zn = jnp.dot(_perm_matrix(ts, seg_len, True), z_ref[...], preferred_element_type=F32).astype(BF16)
    vg = jnp.dot(zn, wglu_ref[...], preferred_element_type=F32)
    o_ref[...] = o_ref[...] + vg[:, :d_model] * jax.nn.sigmoid(vg[:, d_model:])

    @pl.when(tile == last)
    def _():
        _combine_wait(eo_ref, buf_ref, sem, 1 - slot)


def _s5_layer(pending, norm_g, w_in, lam_re, lam_im, log_dt, b_re, b_im, c_re, c_im, d_skip, w_glu):
    x, slab, dest, eo = pending
    t, d = x.shape
    groups, p, k = b_re.shape
    d_ssm = groups * k
    ns = groups * p
    ts = SCAN_TILE
    seg_len = ts // SUBLANES

    col = lambda v: v.reshape(ns, 1).astype(F32)
    log_dt_col = jnp.repeat(log_dt, p)
    rb = 512
    abr, abi, bbr, bbi = pl.pallas_call(
        _s5_disc_kernel,
        grid=(ns // rb,),
        in_specs=[pl.BlockSpec((rb, 1), lambda i: (i, 0))] * 3 + [pl.BlockSpec((rb, k), lambda i: (i, 0))] * 2,
        out_specs=[pl.BlockSpec((rb, 1), lambda i: (i, 0))] * 2 + [pl.BlockSpec((rb, k), lambda i: (i, 0))] * 2,
        out_shape=[jax.ShapeDtypeStruct((ns, 1), F32)] * 2 + [jax.ShapeDtypeStruct((ns, k), F32)] * 2,
        compiler_params=_cparams(("parallel",)),
        name="s5_disc",
    )(col(lam_re), col(lam_im), col(log_dt_col), b_re.reshape(ns, k).astype(F32), b_im.reshape(ns, k).astype(F32))

    gb = 256 // k
    nb = groups // gb
    eye = jnp.eye(gb, dtype=F32)

    def in_blocks(bb):
        m = jnp.einsum('cgpk,gh->cgkhp', bb.reshape(nb, gb, p, k), eye)
        return m.reshape(nb, gb * k, gb * p).astype(BF16)

    def out_blocks(cc):
        m = jnp.einsum('cgkp,gh->cgphk', cc.reshape(nb, gb, k, p).astype(F32), eye)
        return m.reshape(nb, gb * p, gb * k).astype(BF16)

    chunk = 512
    return pl.pallas_call(
        functools.partial(_s5_kernel, chunk=chunk),
        grid_spec=pltpu.PrefetchScalarGridSpec(
            num_scalar_prefetch=1, grid=(t // ts,),
            in_specs=[pl.BlockSpec((ts, d), lambda i, dst: (i, 0)),
                      pl.BlockSpec((ts, ROUTE_LANES), lambda i, dst: (i, 0)),
                      pl.BlockSpec(memory_space=pl.ANY),
                      _const_spec((1, d)), _const_spec((d, d_ssm)),
                      _const_spec((nb, gb * k, gb * p)), _const_spec((nb, gb * k, gb * p)),
                      _const_spec((nb, gb * p, gb * k)), _const_spec((nb, gb * p, gb * k)),
                      _const_spec((1, d_ssm)), _const_spec((1, ns)), _const_spec((1, ns)),
                      _const_spec((d_ssm, 2 * d))],
            out_specs=pl.BlockSpec((ts, d), lambda i, dst: (i, 0)),
            scratch_shapes=[pltpu.VMEM((ts, d_ssm), F32), pltpu.VMEM((ts, d_ssm), BF16),
                            pltpu.VMEM((ts, ns), F32), pltpu.VMEM((ts, ns), F32),
                            pltpu.VMEM((seg_len, ns), F32), pltpu.VMEM((seg_len, ns), F32),
                            pltpu.VMEM((SUBLANES, ns), F32), pltpu.VMEM((SUBLANES, ns), F32),
                            pltpu.VMEM((1, ns), F32), pltpu.VMEM((1, ns), F32),
                            pltpu.VMEM((2, TOP_K, ts, d), F32), pltpu.SemaphoreType.DMA((2, TOP_K))]),
        out_shape=jax.ShapeDtypeStruct((t, d), F32),
        compiler_params=_cparams(("arbitrary",)),
        name="s5_mix",
    )(dest, x, slab, eo, norm_g.reshape(1, d), w_in.astype(BF16), in_blocks(bbr), in_blocks(bbi),
      out_blocks(c_re), out_blocks(-c_im), d_skip.reshape(1, d_ssm).astype(F32),
      abr.reshape(1, ns), abi.reshape(1, ns), w_glu.astype(BF16))


def _route_kernel(x_ref, g_ref, wr_ref, br_ref, slab_ref, cnt_ref, base_ref, *, n_groups, epg):
    @pl.when(pl.program_id(0) == 0)
    def _():
        base_ref[...] = jnp.zeros_like(base_ref)

    tr = x_ref.shape[0]
    xn = _rms(x_ref[...], g_ref[...])
    x_hi = xn.astype(BF16)
    x_lo = (xn - x_hi.astype(F32)).astype(BF16)
    hi = jnp.dot(x_hi, wr_ref[...], preferred_element_type=F32)
    lo = jnp.dot(x_lo, wr_ref[:, :ROUTE_LANES], preferred_element_type=F32)
    logits = hi[:, :ROUTE_LANES] + (hi[:, ROUTE_LANES:] + lo) + br_ref[...]
    lane = lax.broadcasted_iota(I32, logits.shape, 1)
    neg = jnp.float32(-3.0e38)

    def first_argmax(v):
        m = jnp.max(v, axis=1, keepdims=True)
        return m, jnp.min(jnp.where(v == m, lane, ROUTE_LANES), axis=1, keepdims=True)

    gmask = lane < n_groups
    gmax, gidx = first_argmax(jnp.where(gmask, logits, neg))
    p_group = 1.0 / jnp.sum(jnp.where(gmask, jnp.exp(logits - gmax), 0.0), axis=1, keepdims=True)
    lo = n_groups + gidx * epg
    el = jnp.where((lane >= lo) & (lane < lo + epg), logits, neg)
    m1, i1 = first_argmax(el)
    m2, i2 = first_argmax(jnp.where(lane == i1, neg, el))
    t2 = jnp.exp(m2 - m1)
    w1 = p_group / (1.0 + t2)
    w2 = p_group * t2 / (1.0 + t2)

    hit1, hit2 = lane == i1, lane == i2
    onehot = (hit1 | hit2).astype(BF16)
    r = lax.broadcasted_iota(I32, (tr, tr), 0)
    c = lax.broadcasted_iota(I32, (tr, tr), 1)
    before = jnp.dot((c < r).astype(BF16), onehot, preferred_element_type=F32) + base_ref[0:1, :]
    rank1 = jnp.sum(jnp.where(hit1, before, 0.0), axis=1, keepdims=True)
    rank2 = jnp.sum(jnp.where(hit2, before, 0.0), axis=1, keepdims=True)
    total = base_ref[0:1, :] + jnp.sum(onehot.astype(F32), axis=0, keepdims=True)
    base_ref[...] = jnp.broadcast_to(total, base_ref.shape)
    cnt_ref[...] = jnp.broadcast_to(total, cnt_ref.shape)

    vals = (w1, w2, (i1 - n_groups).astype(F32), (i2 - n_groups).astype(F32), rank1, rank2)
    slab = jnp.zeros(logits.shape, F32)
    for k, v in enumerate(vals):
        slab = jnp.where(lane == k, v, slab)
    slab_ref[...] = slab


def _row_gather_start(src_ref, dst_ref, sem, index_of, rows):
    for r in range(rows):
        pltpu.make_async_copy(src_ref.at[pl.ds(index_of(r), 1)], dst_ref.at[pl.ds(r, 1)], sem).start()


def _row_gather_wait(src_ref, dst_ref, sem):
    pltpu.make_async_copy(src_ref.at[pl.ds(0, dst_ref.shape[0])], dst_ref, sem).wait()


def _gmm_kernel(blk_ref, exp_ref, lo_ref, hi_ref, tok_ref, x_ref, g_ref, w1_ref, w3_ref, w2_ref,
                o_ref, xbuf_ref, w1b_ref, w3b_ref, w2b_ref, sem):
    n = pl.program_id(0)
    last = pl.num_programs(0) - 1
    prev = jnp.maximum(n - 1, 0)
    nxt = jnp.minimum(n + 1, last)
    slot = n & 1
    rows = xbuf_ref.shape[1]
    new_expert = (n == 0) | (exp_ref[n] != exp_ref[prev])
    new_block = (n == 0) | (blk_ref[n] != blk_ref[prev])

    def gather(item, into):
        base = blk_ref[item] * rows
        _row_gather_start(x_ref, xbuf_ref.at[into], sem.at[into], lambda r: tok_ref[base + r], rows)

    @pl.when(n == 0)
    def _():
        gather(0, 0)

    @pl.when(new_expert)
    def _():
        w1b_ref[...] = w1_ref[...].astype(BF16)
        w3b_ref[...] = w3_ref[...].astype(BF16)
        w2b_ref[...] = w2_ref[...].astype(BF16)

    _row_gather_wait(x_ref, xbuf_ref.at[slot], sem.at[slot])
    xb = _rms(xbuf_ref[slot], g_ref[...]).astype(BF16)
    gather(nxt, 1 - slot)
    h = jax.nn.silu(jnp.dot(xb, w1b_ref[...], preferred_element_type=F32))
    h = (h * jnp.dot(xb, w3b_ref[...], preferred_element_type=F32)).astype(BF16)
    out = jnp.dot(h, w2b_ref[...], preferred_element_type=F32)
    row = lax.broadcasted_iota(I32, (out.shape[0], 1), 0)
    out = jnp.where((row >= lo_ref[n]) & (row < hi_ref[n]), out, 0.0)

    @pl.when(new_block)
    def _():
        o_ref[...] = out

    @pl.when(jnp.logical_not(new_block))
    def _():
        o_ref[...] = o_ref[...] + out

    @pl.when(n == last)
    def _():
        _row_gather_wait(x_ref, xbuf_ref.at[1 - slot], sem.at[1 - slot])


def _combine_start(dest_ref, eo_ref, buf_ref, sem, tile, into):
    rows = buf_ref.shape[2]
    base = tile * (rows * TOP_K)
    for k in range(TOP_K):
        _row_gather_start(eo_ref, buf_ref.at[into, k], sem.at[into, k],
                          lambda r, k=k: dest_ref[base + r * TOP_K + k], rows)


def _combine_wait(eo_ref, buf_ref, sem, slot):
    for k in range(TOP_K):
        _row_gather_wait(eo_ref, buf_ref.at[slot, k], sem.at[slot, k])


def _combined_tile(x, slab, buf_ref, slot):
    return x + slab[:, 0:1] * buf_ref[slot, 0] + slab[:, 1:2] * buf_ref[slot, 1]


def _combine_kernel(dest_ref, x_ref, slab_ref, g_ref, eo_ref, o_ref, buf_ref, sem):
    i = pl.program_id(0)
    last = pl.num_programs(0) - 1
    slot = i & 1

    @pl.when(i == 0)
    def _():
        _combine_start(dest_ref, eo_ref, buf_ref, sem, 0, 0)

    _combine_wait(eo_ref, buf_ref, sem, slot)
    y = _combined_tile(x_ref[...], slab_ref[...], buf_ref, slot)
    _combine_start(dest_ref, eo_ref, buf_ref, sem, jnp.minimum(i + 1, last), 1 - slot)
    o_ref[...] = _rms(y, g_ref[...])

    @pl.when(i == last)
    def _():
        _combine_wait(eo_ref, buf_ref, sem, 1 - slot)


def _gmm_schedule(counts, n_rows, block):
    n_exp = counts.shape[0]
    n_blocks = n_rows // block
    n_items = n_blocks + n_exp - 1
    ends = jnp.cumsum(counts)
    starts = ends - counts
    first_blk = starts // block
    last_blk = jnp.maximum(ends - 1, 0) // block
    per_exp = jnp.where(counts > 0, last_blk - first_blk + 1, 0)
    item_end = jnp.cumsum(per_exp)
    item_start = item_end - per_exp
    n = jnp.arange(n_items, dtype=I32)
    valid = n < item_end[-1]
    e = jnp.minimum(jnp.sum(item_end[None, :] <= n[:, None], axis=1), n_exp - 1).astype(I32)
    blk = first_blk[e] + (n - item_start[e])
    lo = jnp.maximum(starts[e], blk * block) - blk * block
    hi = jnp.minimum(ends[e], (blk + 1) * block) - blk * block
    last = jnp.maximum(item_end[-1] - 1, 0)
    blk = jnp.where(valid, blk, blk[last]).astype(I32)
    e = jnp.where(valid, e, e[last]).astype(I32)
    lo = jnp.where(valid, lo, 0).astype(I32)
    hi = jnp.where(valid, hi, 0).astype(I32)
    return starts, blk, e, lo, hi


def _moe_layer(x, norm_g, wg, bg, we, be, w1, w3, w2, layer):
    t, d = x.shape
    n_groups, _, epg = we.shape
    n_exp = n_groups * epg
    f = w1.shape[-1]
    tr = ROUTE_TILE
    n_rows = t * TOP_K
    g_row = norm_g.reshape(1, d)

    wr = jnp.concatenate([wg, jnp.transpose(we, (1, 0, 2)).reshape(d, n_exp)], axis=1)
    wr = jnp.pad(wr, ((0, 0), (0, ROUTE_LANES - wr.shape[1]))).astype(F32)
    wr_hi = wr.astype(BF16)
    wr = jnp.concatenate([wr_hi, (wr - wr_hi.astype(F32)).astype(BF16)], axis=1)
    br = jnp.pad(jnp.concatenate([bg, be.reshape(-1)]), (0, ROUTE_LANES - n_groups - n_exp))

    slab, cnt = pl.pallas_call(
        functools.partial(_route_kernel, n_groups=n_groups, epg=epg),
        grid=(t // tr,),
        in_specs=[pl.BlockSpec((tr, d), lambda i: (i, 0)), _const_spec((1, d)),
                  _const_spec((d, 2 * ROUTE_LANES)), _const_spec((1, ROUTE_LANES))],
        out_specs=[pl.BlockSpec((tr, ROUTE_LANES), lambda i: (i, 0)),
                   pl.BlockSpec((SUBLANES, ROUTE_LANES), lambda i: (0, 0))],
        out_shape=[jax.ShapeDtypeStruct((t, ROUTE_LANES), F32),
                   jax.ShapeDtypeStruct((SUBLANES, ROUTE_LANES), F32)],
        scratch_shapes=[pltpu.VMEM((SUBLANES, ROUTE_LANES), F32)],
        compiler_params=_cparams(("arbitrary",)),
        name="moe_route",
    )(x, g_row, wr, br.reshape(1, ROUTE_LANES).astype(F32))

    counts = cnt[0, n_groups:n_groups + n_exp].astype(I32)
    starts, item_blk, item_exp, item_lo, item_hi = _gmm_schedule(counts, n_rows, GMM_BLOCK)
    e_idx = slab[:, 2:2 + TOP_K].astype(I32)
    start_of = jnp.sum(jnp.where(e_idx[..., None] == jnp.arange(n_exp, dtype=I32), starts, 0), axis=-1)
    dest = (start_of + slab[:, 4:4 + TOP_K].astype(I32)).reshape(-1)
    tok = jnp.zeros((n_rows,), I32).at[dest].set(jnp.arange(n_rows, dtype=I32) // TOP_K,
                                                 unique_indices=True)

    n_items = item_blk.shape[0]
    w_spec = lambda shape: pl.BlockSpec((None, None) + shape,
                                        lambda n, blk, ex, lo, hi, tk: (layer, ex[n], 0, 0))
    eo = pl.pallas_call(
        _gmm_kernel,
        grid_spec=pltpu.PrefetchScalarGridSpec(
            num_scalar_prefetch=5, grid=(n_items,),
            in_specs=[pl.BlockSpec(memory_space=pl.ANY),
                      pl.BlockSpec((1, d), lambda n, blk, ex, lo, hi, tk: (0, 0)),
                      w_spec((d, f)), w_spec((d, f)), w_spec((f, d))],
            out_specs=pl.BlockSpec((GMM_BLOCK, d), lambda n, blk, ex, lo, hi, tk: (blk[n], 0)),
            scratch_shapes=[pltpu.VMEM((2, GMM_BLOCK, d), F32),
                            pltpu.VMEM((d, f), BF16), pltpu.VMEM((d, f), BF16), pltpu.VMEM((f, d), BF16),
                            pltpu.SemaphoreType.DMA((2,))]),
        out_shape=jax.ShapeDtypeStruct((n_rows, d), F32),
        compiler_params=_cparams(("arbitrary",)),
        name="moe_gmm",
    )(item_blk, item_exp, item_lo, item_hi, tok, x, g_row, w1, w3, w2)
    return x, slab, dest, eo


def _moe_combine_norm(pending, final_g):
    x, slab, dest, eo = pending
    t, d = x.shape
    tr = ROUTE_TILE
    return pl.pallas_call(
        _combine_kernel,
        grid_spec=pltpu.PrefetchScalarGridSpec(
            num_scalar_prefetch=1, grid=(t // tr,),
            in_specs=[pl.BlockSpec((tr, d), lambda i, dst: (i, 0)),
                      pl.BlockSpec((tr, ROUTE_LANES), lambda i, dst: (i, 0)),
                      pl.BlockSpec((1, d), lambda i, dst: (0, 0)),
                      pl.BlockSpec(memory_space=pl.ANY)],
            out_specs=pl.BlockSpec((tr, d), lambda i, dst: (i, 0)),
            scratch_shapes=[pltpu.VMEM((2, TOP_K, tr, d), F32), pltpu.SemaphoreType.DMA((2, TOP_K))]),
        out_shape=jax.ShapeDtypeStruct((t, d), F32),
        compiler_params=_cparams(("arbitrary",)),
        name="moe_combine",
    )(dest, x, slab, final_g.reshape(1, d), eo)


def kernel(x, mix_norm, ffn_norm, final_norm, a_w_in, a_conv_w, a_conv_b, a_gate_a_w, a_gate_a_b, a_gate_x_w, a_gate_x_b, a_lru_L, a_w_out, b_w_in, b_lam_re, b_lam_im, b_log_dt, b_b_re, b_b_im, b_c_re, b_c_im, b_d, b_w_glu, moe_wg, moe_bg, moe_we, moe_be, moe_w1, moe_w3, moe_w2):
    batch, seq, d = x.shape
    depth = mix_norm.shape[0]
    assert depth == 2, "the combine of each MoE layer is fused into the S5 mixer / final norm"
    outs = []
    for b in range(batch):
        h = _rglru_layer(x[b], mix_norm[0], a_w_in[0], a_conv_w[0], a_conv_b[0], a_gate_a_w[0],
                         a_gate_a_b[0], a_gate_x_w[0], a_gate_x_b[0], a_lru_L[0], a_w_out[0])
        pending = _moe_layer(h, ffn_norm[0], moe_wg[0], moe_bg[0], moe_we[0], moe_be[0],
                             moe_w1, moe_w3, moe_w2, 0)
        h = _s5_layer(pending, mix_norm[1], b_w_in[0], b_lam_re[0], b_lam_im[0], b_log_dt[0],
                      b_b_re[0], b_b_im[0], b_c_re[0], b_c_im[0], b_d[0], b_w_glu[0])
        pending = _moe_layer(h, ffn_norm[1], moe_wg[1], moe_bg[1], moe_we[1], moe_be[1],
                             moe_w1, moe_w3, moe_w2, 1)
        outs.append(_moe_combine_norm(pending, final_norm))
    return jnp.stack(outs)
```

```python
import functools
import math

import jax
import jax.numpy as jnp
from jax import lax
from jax.experimental import pallas as pl
from jax.experimental.pallas import tpu as pltpu

F32 = jnp.float32
BF16 = jnp.bfloat16
I32 = jnp.int32

EPS = 1e-6
LRU_C = 8.0
CONV_WIDTH = 4
TOP_K = 2

LANES = 128
SUBLANES = 8
VMEM_LIMIT = 56 * 1024 * 1024

SCAN_TILE = 256
IN_TILE_M = 1024
IN_TILE_N = 512
ROUTE_TILE = 256
GMM_BLOCK = 256
GMM_AHEAD = 2
SCAN_UNROLL = 2
ROUTE_LANES = 128


def _cparams(sem):
    return pltpu.CompilerParams(dimension_semantics=sem, vmem_limit_bytes=VMEM_LIMIT)


def _const_spec(shape):
    nd = len(shape)
    return pl.BlockSpec(shape, lambda *_: (0,) * nd, pipeline_mode=pl.Buffered(1))


def _rms(x, g):
    ms = jnp.mean(x * x, axis=-1, keepdims=True)
    return x * lax.rsqrt(ms + EPS) * g


def _perm_matrix(ts, seg_len, inverse):
    r = lax.broadcasted_iota(I32, (ts, ts), 1 if inverse else 0)
    n = lax.broadcasted_iota(I32, (ts, ts), 0 if inverse else 1)
    return (n == (r & (SUBLANES - 1)) * seg_len + (r >> 3)).astype(BF16)


def _sublane_id(shape):
    return lax.broadcasted_iota(I32, shape, 0) & (SUBLANES - 1)


def _lru_in_kernel(x_ref, g_ref, w_ref, o_ref, xp_ref, *, ts):
    @pl.when(pl.program_id(1) == 0)
    def _():
        p = _perm_matrix(ts, ts // SUBLANES, False)
        for s in range(x_ref.shape[0] // ts):
            rows = slice(s * ts, (s + 1) * ts)
            xn = _rms(x_ref[rows, :], g_ref[...]).astype(BF16)
            xp_ref[rows, :] = jnp.dot(p, xn, preferred_element_type=F32).astype(BF16)

    o_ref[...] = jnp.dot(xp_ref[...], w_ref[...], preferred_element_type=F32)


def _lru_rec_kernel(gate_ref, rec_ref, x_ref, cw_ref, cb_ref, wg_ref, ba_ref, bx_ref, lam_ref,
                    wout_ref, o_ref, ext_ref, a_ref, b_ref, tail_ref, hcar_ref, *, chunk):
    ts, c = rec_ref.shape
    seg_len = ts // SUBLANES
    halo = (CONV_WIDTH - 1) * SUBLANES
    nblk, bw, _ = wg_ref.shape

    @pl.when(pl.program_id(0) == 0)
    def _():
        tail_ref[...] = jnp.zeros_like(tail_ref)
        hcar_ref[...] = jnp.zeros_like(hcar_ref)

    tail = rec_ref[ts - halo:ts, :]
    sub = _sublane_id((halo, c))
    ext_ref[0:halo, :] = jnp.where(sub == 0, pltpu.roll(tail_ref[...], halo - (SUBLANES - 1), 0),
                                   pltpu.roll(tail, 1, 0))
    ext_ref[halo:halo + ts, :] = rec_ref[...]
    tail_ref[...] = tail

    z = -lam_ref[...]
    softplus = jnp.maximum(z, 0.0) + jnp.log1p(jnp.exp(-jnp.abs(z)))

    for blk in range(nblk):
        cs = slice(blk * bw, (blk + 1) * bw)
        conv = cb_ref[:, cs] + cw_ref[0:1, cs] * ext_ref[0:ts, cs]
        for k in range(1, CONV_WIDTH):
            conv = conv + cw_ref[k:k + 1, cs] * ext_ref[k * SUBLANES:k * SUBLANES + ts, cs]
        g = jnp.dot(conv.astype(BF16), wg_ref[blk], preferred_element_type=F32)
        r = jax.nn.sigmoid(g[:, :bw] + ba_ref[:, cs])
        ig = jax.nn.sigmoid(g[:, bw:] + bx_ref[:, cs])
        log_a = (-LRU_C * r) * softplus[:, cs]
        a_ref[:, cs] = jnp.exp(log_a)
        th = jnp.tanh(log_a)
        b_ref[:, cs] = jnp.sqrt(-2.0 * th / (1.0 - th)) * (ig * conv)

    sub8 = _sublane_id((SUBLANES, chunk))
    for ch in range(c // chunk):
        cs = slice(ch * chunk, (ch + 1) * chunk)

        def step(j, carry, cs=cs):
            h, pp = carry
            rows = pl.ds(pl.multiple_of(j * SUBLANES, SUBLANES), SUBLANES)
            a = a_ref[rows, cs]
            h = a * h + b_ref[rows, cs]
            pp = a * pp
            b_ref[rows, cs] = h
            a_ref[rows, cs] = pp
            return h, pp

        hc, pc = lax.fori_loop(0, seg_len, step,
                               (jnp.zeros((SUBLANES, chunk), F32), jnp.ones((SUBLANES, chunk), F32)),
                               unroll=SCAN_UNROLL)
        for d in (1, 2, 4):
            keep = sub8 >= d
            hs = jnp.where(keep, pltpu.roll(hc, d, 0), 0.0)
            ps = jnp.where(keep, pltpu.roll(pc, d, 0), 1.0)
            hc = pc * hs + hc
            pc = pc * ps
        cin = hcar_ref[:, cs]
        end = hc + pc * cin
        seg_in = jnp.where(sub8 >= 1, pltpu.roll(end, 1, 0), cin)
        hcar_ref[:, cs] = end[SUBLANES - 1:SUBLANES, :]

        def fix(j, carry, cs=cs, seg_in=seg_in):
            rows = pl.ds(pl.multiple_of(j * SUBLANES, SUBLANES), SUBLANES)
            b_ref[rows, cs] = b_ref[rows, cs] + a_ref[rows, cs] * seg_in
            return carry

        lax.fori_loop(0, seg_len, fix, 0, unroll=SCAN_UNROLL)

    y = (b_ref[...] * jax.nn.gelu(gate_ref[...])).astype(BF16)
    y = jnp.dot(_perm_matrix(ts, seg_len, True), y, preferred_element_type=F32).astype(BF16)
    o_ref[...] = x_ref[...] + jnp.dot(y, wout_ref[...], preferred_element_type=F32)


def _block_diag(w, per_block):
    h, d, _ = w.shape
    nb = h // per_block
    eye = jnp.eye(per_block, dtype=w.dtype)
    m = jnp.einsum('bhij,hk->bhikj', w.reshape(nb, per_block, d, d), eye)
    return m.reshape(nb, per_block * d, per_block * d)


def _rglru_layer(x, norm_g, w_in, conv_w, conv_b, wa, ba, wx, bx, lru_l, w_out):
    t, d = x.shape
    c = w_out.shape[0]
    heads, hd, _ = wa.shape
    per_block = LANES // math.gcd(hd, LANES)
    ts = SCAN_TILE
    tm = min(IN_TILE_M, t)
    tn = IN_TILE_N

    proj = pl.pallas_call(
        functools.partial(_lru_in_kernel, ts=ts),
        grid=(t // tm, (2 * c) // tn),
        in_specs=[pl.BlockSpec((tm, d), lambda i, j: (i, 0)),
                  pl.BlockSpec((1, d), lambda i, j: (0, 0)),
                  pl.BlockSpec((d, tn), lambda i, j: (0, j))],
        out_specs=pl.BlockSpec((tm, tn), lambda i, j: (i, j)),
        out_shape=jax.ShapeDtypeStruct((t, 2 * c), F32),
        scratch_shapes=[pltpu.VMEM((tm, d), BF16)],
        compiler_params=_cparams(("parallel", "arbitrary")),
        name="lru_in",
    )(x, norm_g.reshape(1, d), w_in.astype(BF16))

    wg = jnp.concatenate([_block_diag(wa, per_block), _block_diag(wx, per_block)], axis=-1).astype(BF16)
    nblk, bw, _ = wg.shape
    halo = (CONV_WIDTH - 1) * SUBLANES
    row = lambda v: v.reshape(1, c)
    return pl.pallas_call(
        functools.partial(_lru_rec_kernel, chunk=bw * 2),
        grid=(t // ts,),
        in_specs=[pl.BlockSpec((ts, c), lambda i: (i, 0)),
                  pl.BlockSpec((ts, c), lambda i: (i, 1)),
                  pl.BlockSpec((ts, d), lambda i: (i, 0)),
                  _const_spec((CONV_WIDTH, c)), _const_spec((1, c)),
                  _const_spec((nblk, bw, 2 * bw)),
                  _const_spec((1, c)), _const_spec((1, c)), _const_spec((1, c)),
                  _const_spec((c, d))],
        out_specs=pl.BlockSpec((ts, d), lambda i: (i, 0)),
        out_shape=jax.ShapeDtypeStruct((t, d), F32),
        scratch_shapes=[pltpu.VMEM((ts + halo, c), F32), pltpu.VMEM((ts, c), F32),
                        pltpu.VMEM((ts, c), F32), pltpu.VMEM((halo, c), F32),
                        pltpu.VMEM((1, c), F32)],
        compiler_params=_cparams(("arbitrary",)),
        name="lru_rec",
    )(proj, proj, x, conv_w, row(conv_b), wg, row(ba), row(bx), row(lru_l), w_out.astype(BF16))


def _s5_disc_kernel(lre_ref, lim_ref, dt_ref, bre_ref, bim_ref, abr_ref, abi_ref, bbr_ref, bbi_ref):
    lre = jnp.minimum(lre_ref[...], -1e-4)
    lim = lim_ref[...]
    dt = jnp.exp(dt_ref[...])
    mag = jnp.exp(lre * dt)
    abr = mag * jnp.cos(lim * dt)
    abi = mag * jnp.sin(lim * dt)
    den = lre * lre + lim * lim
    kr = ((abr - 1.0) * lre + abi * lim) / den
    ki = (abi * lre - (abr - 1.0) * lim) / den
    abr_ref[...] = abr
    abi_ref[...] = abi
    bbr_ref[...] = kr * bre_ref[...] - ki * bim_ref[...]
    bbi_ref[...] = kr * bim_ref[...] + ki * bre_ref[...]


def _s5_kernel(dest_ref, x_ref, slab_ref, eo_ref, g_ref, win_ref, br_ref, bi_ref, cr_ref, ci_ref,
               d_ref, abr_ref, abi_ref, wglu_ref, o_ref, u_ref, z_ref, xr_ref, xi_ref, a8r_ref,
               a8i_ref, asr_ref, asi_ref, carr_ref, cari_ref, buf_ref, sem, *, chunk):
    ts, d_model = x_ref.shape
    tile = pl.program_id(0)
    last = pl.num_programs(0) - 1
    slot = tile & 1
    seg_len = ts // SUBLANES
    ns = xr_ref.shape[1]
    nb_in, kin, nin = br_ref.shape
    nb_out, kout, nout = cr_ref.shape

    @pl.when(tile == 0)
    def _():
        _combine_start(dest_ref, eo_ref, buf_ref, sem, 0, 0)
        carr_ref[...] = jnp.zeros_like(carr_ref)
        cari_ref[...] = jnp.zeros_like(cari_ref)
        ar, ai = abr_ref[...], abi_ref[...]
        a8r_ref[...] = jnp.broadcast_to(ar, a8r_ref.shape)
        a8i_ref[...] = jnp.broadcast_to(ai, a8i_ref.shape)

        def power(j, cur):
            pr, pi = cur
            return pr * ar - pi * ai, pr * ai + pi * ar

        lr, li = lax.fori_loop(0, seg_len - 1, power, (ar, ai))

        def seg_powers(s, cur):
            pr, pi = cur
            asr_ref[pl.ds(s, 1), :] = pr
            asi_ref[pl.ds(s, 1), :] = pi
            return pr * lr - pi * li, pr * li + pi * lr

        lax.fori_loop(0, SUBLANES, seg_powers, (lr, li))

    _combine_wait(eo_ref, buf_ref, sem, slot)
    x = _combined_tile(x_ref[...], slab_ref[...], buf_ref, slot)
    o_ref[...] = x
    _combine_start(dest_ref, eo_ref, buf_ref, sem, jnp.minimum(tile + 1, last), 1 - slot)
    xn = _rms(x, g_ref[...]).astype(BF16)
    xp = jnp.dot(_perm_matrix(ts, seg_len, False), xn, preferred_element_type=F32).astype(BF16)
    u = jnp.dot(xp, win_ref[...], preferred_element_type=F32)
    u_ref[...] = u
    ub = u.astype(BF16)
    for blk in range(nb_in):
        uc = ub[:, blk * kin:(blk + 1) * kin]
        xr_ref[:, blk * nin:(blk + 1) * nin] = jnp.dot(uc, br_ref[blk], preferred_element_type=F32)
        xi_ref[:, blk * nin:(blk + 1) * nin] = jnp.dot(uc, bi_ref[blk], preferred_element_type=F32)

    sub8 = _sublane_id((SUBLANES, chunk))
    for ch in range(ns // chunk):
        cs = slice(ch * chunk, (ch + 1) * chunk)
        ar, ai = a8r_ref[:, cs], a8i_ref[:, cs]

        def step(j, carry, cs=cs, ar=ar, ai=ai):
            sr, si = carry
            rows = pl.ds(pl.multiple_of(j * SUBLANES, SUBLANES), SUBLANES)
            nr = ar * sr - ai * si + xr_ref[rows, cs]
            ni = ar * si + ai * sr + xi_ref[rows, cs]
            xr_ref[rows, cs] = nr
            xi_ref[rows, cs] = ni
            return nr, ni

        zero = jnp.zeros((SUBLANES, chunk), F32)
        er, ei = lax.fori_loop(0, seg_len, step, (zero, zero), unroll=SCAN_UNROLL)
        for d in (1, 2, 4):
            keep = sub8 >= d
            pr, pi = asr_ref[d - 1:d, cs], asi_ref[d - 1:d, cs]
            sr = jnp.where(keep, pltpu.roll(er, d, 0), 0.0)
            si = jnp.where(keep, pltpu.roll(ei, d, 0), 0.0)
            er, ei = er + pr * sr - pi * si, ei + pr * si + pi * sr
        cr, ci = carr_ref[:, cs], cari_ref[:, cs]
        pr, pi = asr_ref[:, cs], asi_ref[:, cs]
        er, ei = er + pr * cr - pi * ci, ei + pr * ci + pi * cr
        inr = jnp.where(sub8 >= 1, pltpu.roll(er, 1, 0), cr)
        ini = jnp.where(sub8 >= 1, pltpu.roll(ei, 1, 0), ci)
        carr_ref[:, cs] = er[SUBLANES - 1:SUBLANES, :]
        cari_ref[:, cs] = ei[SUBLANES - 1:SUBLANES, :]

        def fix(j, carry, cs=cs, ar=ar, ai=ai):
            fr, fi = carry
            fr, fi = ar * fr - ai * fi, ar * fi + ai * fr
            rows = pl.ds(pl.multiple_of(j * SUBLANES, SUBLANES), SUBLANES)
            xr_ref[rows, cs] = xr_ref[rows, cs] + fr
            xi_ref[rows, cs] = xi_ref[rows, cs] + fi
            return fr, fi

        lax.fori_loop(0, seg_len, fix, (inr, ini), unroll=SCAN_UNROLL)

    for blk in range(nb_out):
        ks = slice(blk * kout, (blk + 1) * kout)
        ys = slice(blk * nout, (blk + 1) * nout)
        y = jnp.dot(xr_ref[:, ks].astype(BF16), cr_ref[blk], preferred_element_type=F32)
        y = y + jnp.dot(xi_ref[:, ks].astype(BF16), ci_ref[blk], preferred_element_type=F32)
        y = y + d_ref[:, ys] * u_ref[:, ys]
        z_ref[:, ys] = jax.nn.gelu(y).astype(BF16)

    zn = jnp.dot(_perm_matrix(ts, seg_len, True), z_ref[...], preferred_element_type=F32).astype(BF16)
    vg = jnp.dot(zn, wglu_ref[...], preferred_element_type=F32)
    o_ref[...] = o_ref[...] + vg[:, :d_model] * jax.nn.sigmoid(vg[:, d_model:])

    @pl.when(tile == last)
    def _():
        _combine_wait(eo_ref, buf_ref, sem, 1 - slot)


def _s5_layer(pending, norm_g, w_in, lam_re, lam_im, log_dt, b_re, b_im, c_re, c_im, d_skip, w_glu):
    x, slab, dest, eo = pending
    t, d = x.shape
    groups, p, k = b_re.shape
    d_ssm = groups * k
    ns = groups * p
    ts = SCAN_TILE
    seg_len = ts // SUBLANES

    col = lambda v: v.reshape(ns, 1).astype(F32)
    log_dt_col = jnp.repeat(log_dt, p)
    rb = 512
    abr, abi, bbr, bbi = pl.pallas_call(
        _s5_disc_kernel,
        grid=(ns // rb,),
        in_specs=[pl.BlockSpec((rb, 1), lambda i: (i, 0))] * 3 + [pl.BlockSpec((rb, k), lambda i: (i, 0))] * 2,
        out_specs=[pl.BlockSpec((rb, 1), lambda i: (i, 0))] * 2 + [pl.BlockSpec((rb, k), lambda i: (i, 0))] * 2,
        out_shape=[jax.ShapeDtypeStruct((ns, 1), F32)] * 2 + [jax.ShapeDtypeStruct((ns, k), F32)] * 2,
        compiler_params=_cparams(("parallel",)),
        name="s5_disc",
    )(col(lam_re), col(lam_im), col(log_dt_col), b_re.reshape(ns, k).astype(F32), b_im.reshape(ns, k).astype(F32))

    gb = 256 // k
    nb = groups // gb
    eye = jnp.eye(gb, dtype=F32)

    def in_blocks(bb):
        m = jnp.einsum('cgpk,gh->cgkhp', bb.reshape(nb, gb, p, k), eye)
        return m.reshape(nb, gb * k, gb * p).astype(BF16)

    def out_blocks(cc):
        m = jnp.einsum('cgkp,gh->cgphk', cc.reshape(nb, gb, k, p).astype(F32), eye)
        return m.reshape(nb, gb * p, gb * k).astype(BF16)

    chunk = 512
    return pl.pallas_call(
        functools.partial(_s5_kernel, chunk=chunk),
        grid_spec=pltpu.PrefetchScalarGridSpec(
            num_scalar_prefetch=1, grid=(t // ts,),
            in_specs=[pl.BlockSpec((ts, d), lambda i, dst: (i, 0)),
                      pl.BlockSpec((ts, ROUTE_LANES), lambda i, dst: (i, 0)),
                      pl.BlockSpec(memory_space=pl.ANY),
                      _const_spec((1, d)), _const_spec((d, d_ssm)),
                      _const_spec((nb, gb * k, gb * p)), _const_spec((nb, gb * k, gb * p)),
                      _const_spec((nb, gb * p, gb * k)), _const_spec((nb, gb * p, gb * k)),
                      _const_spec((1, d_ssm)), _const_spec((1, ns)), _const_spec((1, ns)),
                      _const_spec((d_ssm, 2 * d))],
            out_specs=pl.BlockSpec((ts, d), lambda i, dst: (i, 0)),
            scratch_shapes=[pltpu.VMEM((ts, d_ssm), F32), pltpu.VMEM((ts, d_ssm), BF16),
                            pltpu.VMEM((ts, ns), F32), pltpu.VMEM((ts, ns), F32),
                            pltpu.VMEM((SUBLANES, ns), F32), pltpu.VMEM((SUBLANES, ns), F32),
                            pltpu.VMEM((SUBLANES, ns), F32), pltpu.VMEM((SUBLANES, ns), F32),
                            pltpu.VMEM((1, ns), F32), pltpu.VMEM((1, ns), F32),
                            pltpu.VMEM((2, TOP_K, ts, d), F32), pltpu.SemaphoreType.DMA((2, TOP_K))]),
        out_shape=jax.ShapeDtypeStruct((t, d), F32),
        compiler_params=_cparams(("arbitrary",)),
        name="s5_mix",
    )(dest, x, slab, eo, norm_g.reshape(1, d), w_in.astype(BF16), in_blocks(bbr), in_blocks(bbi),
      out_blocks(c_re), out_blocks(-c_im), d_skip.reshape(1, d_ssm).astype(F32),
      abr.reshape(1, ns), abi.reshape(1, ns), w_glu.astype(BF16))


def _route_kernel(x_ref, g_ref, wr_ref, br_ref, slab_ref, cnt_ref, base_ref, *, n_groups, epg):
    @pl.when(pl.program_id(0) == 0)
    def _():
        base_ref[...] = jnp.zeros_like(base_ref)

    tr = x_ref.shape[0]
    xn = _rms(x_ref[...], g_ref[...])
    x_hi = xn.astype(BF16)
    x_lo = (xn - x_hi.astype(F32)).astype(BF16)
    hi = jnp.dot(x_hi, wr_ref[...], preferred_element_type=F32)
    lo = jnp.dot(x_lo, wr_ref[:, :ROUTE_LANES], preferred_element_type=F32)
    logits = hi[:, :ROUTE_LANES] + (hi[:, ROUTE_LANES:] + lo) + br_ref[...]
    lane = lax.broadcasted_iota(I32, logits.shape, 1)
    neg = jnp.float32(-3.0e38)

    def first_argmax(v):
        m = jnp.max(v, axis=1, keepdims=True)
        return m, jnp.min(jnp.where(v == m, lane, ROUTE_LANES), axis=1, keepdims=True)

    gmask = lane < n_groups
    gmax, gidx = first_argmax(jnp.where(gmask, logits, neg))
    p_group = 1.0 / jnp.sum(jnp.where(gmask, jnp.exp(logits - gmax), 0.0), axis=1, keepdims=True)
    lo = n_groups + gidx * epg
    el = jnp.where((lane >= lo) & (lane < lo + epg), logits, neg)
    m1, i1 = first_argmax(el)
    m2, i2 = first_argmax(jnp.where(lane == i1, neg, el))
    t2 = jnp.exp(m2 - m1)
    w1 = p_group / (1.0 + t2)
    w2 = p_group * t2 / (1.0 + t2)

    hit1, hit2 = lane == i1, lane == i2
    onehot = (hit1 | hit2).astype(BF16)
    r = lax.broadcasted_iota(I32, (tr, tr), 0)
    c = lax.broadcasted_iota(I32, (tr, tr), 1)
    before = jnp.dot((c < r).astype(BF16), onehot, preferred_element_type=F32) + base_ref[0:1, :]
    rank1 = jnp.sum(jnp.where(hit1, before, 0.0), axis=1, keepdims=True)
    rank2 = jnp.sum(jnp.where(hit2, before, 0.0), axis=1, keepdims=True)
    total = base_ref[0:1, :] + jnp.sum(onehot.astype(F32), axis=0, keepdims=True)
    base_ref[...] = jnp.broadcast_to(total, base_ref.shape)
    cnt_ref[...] = jnp.broadcast_to(total, cnt_ref.shape)

    vals = (w1, w2, (i1 - n_groups).astype(F32), (i2 - n_groups).astype(F32), rank1, rank2)
    slab = jnp.zeros(logits.shape, F32)
    for k, v in enumerate(vals):
        slab = jnp.where(lane == k, v, slab)
    slab_ref[...] = slab


def _row_gather_start(src_ref, dst_ref, sem, index_of, rows):
    for r in range(rows):
        pltpu.make_async_copy(src_ref.at[pl.ds(index_of(r), 1)], dst_ref.at[pl.ds(r, 1)], sem).start()


def _row_gather_wait(src_ref, dst_ref, sem):
    pltpu.make_async_copy(src_ref.at[pl.ds(0, dst_ref.shape[0])], dst_ref, sem).wait()


def _gmm_kernel(blk_ref, exp_ref, lo_ref, hi_ref, dest_ref, x_ref, g_ref, w1_ref, w3_ref, w2_ref,
                o_ref, xbuf_ref, w1b_ref, w3b_ref, w2b_ref, tok_ref, sem):
    n = pl.program_id(0)
    last = pl.num_programs(0) - 1
    prev = jnp.maximum(n - 1, 0)
    nslots = xbuf_ref.shape[0]
    slot = lax.rem(n, nslots)
    rows = xbuf_ref.shape[1]
    new_expert = (n == 0) | (exp_ref[n] != exp_ref[prev])
    new_block = (n == 0) | (blk_ref[n] != blk_ref[prev])

    def gather(item, into):
        base = blk_ref[item] * rows
        _row_gather_start(x_ref, xbuf_ref.at[into], sem.at[into], lambda r: tok_ref[base + r], rows)

    @pl.when(n == 0)
    def _():
        def invert(t, carry):
            for k in range(TOP_K):
                tok_ref[dest_ref[t * TOP_K + k]] = t
            return carry

        lax.fori_loop(0, tok_ref.shape[0] // TOP_K, invert, 0, unroll=8)
        for ahead in range(GMM_AHEAD):
            gather(jnp.minimum(ahead, last), ahead)

    @pl.when(new_expert)
    def _():
        w1b_ref[...] = w1_ref[...].astype(BF16)
        w3b_ref[...] = w3_ref[...].astype(BF16)
        w2b_ref[...] = w2_ref[...].astype(BF16)

    _row_gather_wait(x_ref, xbuf_ref.at[slot], sem.at[slot])
    xb = _rms(xbuf_ref[slot], g_ref[...]).astype(BF16)
    gather(jnp.minimum(n + GMM_AHEAD, last), lax.rem(n + GMM_AHEAD, nslots))
    h = jax.nn.silu(jnp.dot(xb, w1b_ref[...], preferred_element_type=F32))
    h = (h * jnp.dot(xb, w3b_ref[...], preferred_element_type=F32)).astype(BF16)
    out = jnp.dot(h, w2b_ref[...], preferred_element_type=F32)
    row = lax.broadcasted_iota(I32, (out.shape[0], 1), 0)
    out = jnp.where((row >= lo_ref[n]) & (row < hi_ref[n]), out, 0.0)

    @pl.when(new_block)
    def _():
        o_ref[...] = out

    @pl.when(jnp.logical_not(new_block))
    def _():
        o_ref[...] = o_ref[...] + out

    @pl.when(n == last)
    def _():
        for ahead in range(1, GMM_AHEAD + 1):
            other = lax.rem(n + ahead, nslots)
            _row_gather_wait(x_ref, xbuf_ref.at[other], sem.at[other])


def _combine_start(dest_ref, eo_ref, buf_ref, sem, tile, into):
    rows = buf_ref.shape[2]
    base = tile * (rows * TOP_K)
    for k in range(TOP_K):
        _row_gather_start(eo_ref, buf_ref.at[into, k], sem.at[into, k],
                          lambda r, k=k: dest_ref[base + r * TOP_K + k], rows)


def _combine_wait(eo_ref, buf_ref, sem, slot):
    for k in range(TOP_K):
        _row_gather_wait(eo_ref, buf_ref.at[slot, k], sem.at[slot, k])


def _combined_tile(x, slab, buf_ref, slot):
    return x + slab[:, 0:1] * buf_ref[slot, 0] + slab[:, 1:2] * buf_ref[slot, 1]


def _combine_kernel(dest_ref, x_ref, slab_ref, g_ref, eo_ref, o_ref, buf_ref, sem):
    i = pl.program_id(0)
    last = pl.num_programs(0) - 1
    slot = i & 1

    @pl.when(i == 0)
    def _():
        _combine_start(dest_ref, eo_ref, buf_ref, sem, 0, 0)

    _combine_wait(eo_ref, buf_ref, sem, slot)
    y = _combined_tile(x_ref[...], slab_ref[...], buf_ref, slot)
    _combine_start(dest_ref, eo_ref, buf_ref, sem, jnp.minimum(i + 1, last), 1 - slot)
    o_ref[...] = _rms(y, g_ref[...])

    @pl.when(i == last)
    def _():
        _combine_wait(eo_ref, buf_ref, sem, 1 - slot)


def _gmm_schedule(counts, n_rows, block):
    n_exp = counts.shape[0]
    n_blocks = n_rows // block
    n_items = n_blocks + n_exp - 1
    ends = jnp.cumsum(counts)
    starts = ends - counts
    first_blk = starts // block
    last_blk = jnp.maximum(ends - 1, 0) // block
    per_exp = jnp.where(counts > 0, last_blk - first_blk + 1, 0)
    item_end = jnp.cumsum(per_exp)
    item_start = item_end - per_exp
    n = jnp.arange(n_items, dtype=I32)
    valid = n < item_end[-1]
    e = jnp.minimum(jnp.sum(item_end[None, :] <= n[:, None], axis=1), n_exp - 1).astype(I32)
    blk = first_blk[e] + (n - item_start[e])
    lo = jnp.maximum(starts[e], blk * block) - blk * block
    hi = jnp.minimum(ends[e], (blk + 1) * block) - blk * block
    last = jnp.maximum(item_end[-1] - 1, 0)
    blk = jnp.where(valid, blk, blk[last]).astype(I32)
    e = jnp.where(valid, e, e[last]).astype(I32)
    lo = jnp.where(valid, lo, 0).astype(I32)
    hi = jnp.where(valid, hi, 0).astype(I32)
    return starts, blk, e, lo, hi


def _moe_layer(x, norm_g, wg, bg, we, be, w1, w3, w2, layer):
    t, d = x.shape
    n_groups, _, epg = we.shape
    n_exp = n_groups * epg
    f = w1.shape[-1]
    tr = ROUTE_TILE
    n_rows = t * TOP_K
    g_row = norm_g.reshape(1, d)

    wr = jnp.concatenate([wg, jnp.transpose(we, (1, 0, 2)).reshape(d, n_exp)], axis=1)
    wr = jnp.pad(wr, ((0, 0), (0, ROUTE_LANES - wr.shape[1]))).astype(F32)
    wr_hi = wr.astype(BF16)
    wr = jnp.concatenate([wr_hi, (wr - wr_hi.astype(F32)).astype(BF16)], axis=1)
    br = jnp.pad(jnp.concatenate([bg, be.reshape(-1)]), (0, ROUTE_LANES - n_groups - n_exp))

    slab, cnt = pl.pallas_call(
        functools.partial(_route_kernel, n_groups=n_groups, epg=epg),
        grid=(t // tr,),
        in_specs=[pl.BlockSpec((tr, d), lambda i: (i, 0)), _const_spec((1, d)),
                  _const_spec((d, 2 * ROUTE_LANES)), _const_spec((1, ROUTE_LANES))],
        out_specs=[pl.BlockSpec((tr, ROUTE_LANES), lambda i: (i, 0)),
                   pl.BlockSpec((SUBLANES, ROUTE_LANES), lambda i: (0, 0))],
        out_shape=[jax.ShapeDtypeStruct((t, ROUTE_LANES), F32),
                   jax.ShapeDtypeStruct((SUBLANES, ROUTE_LANES), F32)],
        scratch_shapes=[pltpu.VMEM((SUBLANES, ROUTE_LANES), F32)],
        compiler_params=_cparams(("arbitrary",)),
        name="moe_route",
    )(x, g_row, wr, br.reshape(1, ROUTE_LANES).astype(F32))

    counts = cnt[0, n_groups:n_groups + n_exp].astype(I32)
    starts, item_blk, item_exp, item_lo, item_hi = _gmm_schedule(counts, n_rows, GMM_BLOCK)
    e_idx = slab[:, 2:2 + TOP_K].astype(I32)
    start_of = jnp.sum(jnp.where(e_idx[..., None] == jnp.arange(n_exp, dtype=I32), starts, 0), axis=-1)
    dest = (start_of + slab[:, 4:4 + TOP_K].astype(I32)).reshape(-1)

    n_items = item_blk.shape[0]
    w_spec = lambda shape: pl.BlockSpec((None, None) + shape,
                                        lambda n, blk, ex, lo, hi, tk: (layer, ex[n], 0, 0))
    eo = pl.pallas_call(
        _gmm_kernel,
        grid_spec=pltpu.PrefetchScalarGridSpec(
            num_scalar_prefetch=5, grid=(n_items,),
            in_specs=[pl.BlockSpec(memory_space=pl.ANY),
                      pl.BlockSpec((1, d), lambda n, blk, ex, lo, hi, tk: (0, 0)),
                      w_spec((d, f)), w_spec((d, f)), w_spec((f, d))],
            out_specs=pl.BlockSpec((GMM_BLOCK, d), lambda n, blk, ex, lo, hi, tk: (blk[n], 0)),
            scratch_shapes=[pltpu.VMEM((GMM_AHEAD + 1, GMM_BLOCK, d), F32),
                            pltpu.VMEM((d, f), BF16), pltpu.VMEM((d, f), BF16), pltpu.VMEM((f, d), BF16),
                            pltpu.SMEM((n_rows,), I32),
                            pltpu.SemaphoreType.DMA((GMM_AHEAD + 1,))]),
        out_shape=jax.ShapeDtypeStruct((n_rows, d), F32),
        compiler_params=_cparams(("arbitrary",)),
        name="moe_gmm",
    )(item_blk, item_exp, item_lo, item_hi, dest, x, g_row, w1, w3, w2)
    return x, slab, dest, eo


def _moe_combine_norm(pending, final_g):
    x, slab, dest, eo = pending
    t, d = x.shape
    tr = ROUTE_TILE
    return pl.pallas_call(
        _combine_kernel,
        grid_spec=pltpu.PrefetchScalarGridSpec(
            num_scalar_prefetch=1, grid=(t // tr,),
            in_specs=[pl.BlockSpec((tr, d), lambda i, dst: (i, 0)),
                      pl.BlockSpec((tr, ROUTE_LANES), lambda i, dst: (i, 0)),
                      pl.BlockSpec((1, d), lambda i, dst: (0, 0)),
                      pl.BlockSpec(memory_space=pl.ANY)],
            out_specs=pl.BlockSpec((tr, d), lambda i, dst: (i, 0)),
            scratch_shapes=[pltpu.VMEM((2, TOP_K, tr, d), F32), pltpu.SemaphoreType.DMA((2, TOP_K))]),
        out_shape=jax.ShapeDtypeStruct((t, d), F32),
        compiler_params=_cparams(("arbitrary",)),
        name="moe_combine",
    )(dest, x, slab, final_g.reshape(1, d), eo)


def kernel(x, mix_norm, ffn_norm, final_norm, a_w_in, a_conv_w, a_conv_b, a_gate_a_w, a_gate_a_b, a_gate_x_w, a_gate_x_b, a_lru_L, a_w_out, b_w_in, b_lam_re, b_lam_im, b_log_dt, b_b_re, b_b_im, b_c_re, b_c_im, b_d, b_w_glu, moe_wg, moe_bg, moe_we, moe_be, moe_w1, moe_w3, moe_w2):
    batch, seq, d = x.shape
    depth = mix_norm.shape[0]
    assert depth == 2, "the combine of each MoE layer is fused into the S5 mixer / final norm"
    outs = []
    for b in range(batch):
        h = _rglru_layer(x[b], mix_norm[0], a_w_in[0], a_conv_w[0], a_conv_b[0], a_gate_a_w[0],
                         a_gate_a_b[0], a_gate_x_w[0], a_gate_x_b[0], a_lru_L[0], a_w_out[0])
        pending = _moe_layer(h, ffn_norm[0], moe_wg[0], moe_bg[0], moe_we[0], moe_be[0],
                             moe_w1, moe_w3, moe_w2, 0)
        h = _s5_layer(pending, mix_norm[1], b_w_in[0], b_lam_re[0], b_lam_im[0], b_log_dt[0],
                      b_b_re[0], b_b_im[0], b_c_re[0], b_c_im[0], b_d[0], b_w_glu[0])
        pending = _moe_layer(h, ffn_norm[1], moe_wg[1], moe_bg[1], moe_we[1], moe_be[1],
                             moe_w1, moe_w3, moe_w2, 1)
        outs.append(_moe_combine_norm(pending, final_norm))
    return jnp.stack(outs)
```

```python
import functools
import math

import jax
import jax.numpy as jnp
from jax import lax
from jax.experimental import pallas as pl
from jax.experimental.pallas import tpu as pltpu

F32 = jnp.float32
BF16 = jnp.bfloat16
I32 = jnp.int32

EPS = 1e-6
LRU_C = 8.0
CONV_WIDTH = 4
TOP_K = 2

LANES = 128
SUBLANES = 8
VMEM_LIMIT = 56 * 1024 * 1024

SCAN_TILE = 256
IN_TILE_M = 1024
IN_TILE_N = 512
ROUTE_TILE = 256
GMM_BLOCK = 256
GMM_AHEAD = 2
SCAN_UNROLL = True
ROUTE_LANES = 128


def _cparams(sem):
    return pltpu.CompilerParams(dimension_semantics=sem, vmem_limit_bytes=VMEM_LIMIT)


def _const_spec(shape):
    nd = len(shape)
    return pl.BlockSpec(shape, lambda *_: (0,) * nd, pipeline_mode=pl.Buffered(1))


def _rms(x, g):
    ms = jnp.mean(x * x, axis=-1, keepdims=True)
    return x * lax.rsqrt(ms + EPS) * g


def _perm_matrix(ts, seg_len, inverse):
    r = lax.broadcasted_iota(I32, (ts, ts), 1 if inverse else 0)
    n = lax.broadcasted_iota(I32, (ts, ts), 0 if inverse else 1)
    return (n == (r & (SUBLANES - 1)) * seg_len + (r >> 3)).astype(BF16)


def _sublane_id(shape):
    return lax.broadcasted_iota(I32, shape, 0) & (SUBLANES - 1)


def _lru_in_kernel(x_ref, g_ref, w_ref, o_ref, xp_ref, *, ts):
    @pl.when(pl.program_id(1) == 0)
    def _():
        p = _perm_matrix(ts, ts // SUBLANES, False)
        for s in range(x_ref.shape[0] // ts):
            rows = slice(s * ts, (s + 1) * ts)
            xn = _rms(x_ref[rows, :], g_ref[...]).astype(BF16)
            xp_ref[rows, :] = jnp.dot(p, xn, preferred_element_type=F32).astype(BF16)

    o_ref[...] = jnp.dot(xp_ref[...], w_ref[...], preferred_element_type=F32)


def _lru_rec_kernel(gate_ref, rec_ref, x_ref, cw_ref, cb_ref, wg_ref, ba_ref, bx_ref, lam_ref,
                    wout_ref, o_ref, ext_ref, a_ref, b_ref, tail_ref, hcar_ref, *, chunk):
    ts, c = rec_ref.shape
    seg_len = ts // SUBLANES
    halo = (CONV_WIDTH - 1) * SUBLANES
    nblk, bw, _ = wg_ref.shape

    @pl.when(pl.program_id(0) == 0)
    def _():
        tail_ref[...] = jnp.zeros_like(tail_ref)
        hcar_ref[...] = jnp.zeros_like(hcar_ref)

    tail = rec_ref[ts - halo:ts, :]
    sub = _sublane_id((halo, c))
    ext_ref[0:halo, :] = jnp.where(sub == 0, pltpu.roll(tail_ref[...], halo - (SUBLANES - 1), 0),
                                   pltpu.roll(tail, 1, 0))
    ext_ref[halo:halo + ts, :] = rec_ref[...]
    tail_ref[...] = tail

    z = -lam_ref[...]
    softplus = jnp.maximum(z, 0.0) + jnp.log1p(jnp.exp(-jnp.abs(z)))

    for blk in range(nblk):
        cs = slice(blk * bw, (blk + 1) * bw)
        conv = cb_ref[:, cs] + cw_ref[0:1, cs] * ext_ref[0:ts, cs]
        for k in range(1, CONV_WIDTH):
            conv = conv + cw_ref[k:k + 1, cs] * ext_ref[k * SUBLANES:k * SUBLANES + ts, cs]
        g = jnp.dot(conv.astype(BF16), wg_ref[blk], preferred_element_type=F32)
        r = jax.nn.sigmoid(g[:, :bw] + ba_ref[:, cs])
        ig = jax.nn.sigmoid(g[:, bw:] + bx_ref[:, cs])
        log_a = (-LRU_C * r) * softplus[:, cs]
        a_ref[:, cs] = jnp.exp(log_a)
        th = jnp.tanh(log_a)
        b_ref[:, cs] = jnp.sqrt(-2.0 * th / (1.0 - th)) * (ig * conv)

    sub8 = _sublane_id((SUBLANES, chunk))
    for ch in range(c // chunk):
        cs = slice(ch * chunk, (ch + 1) * chunk)

        def step(j, carry, cs=cs):
            h, pp = carry
            rows = pl.ds(pl.multiple_of(j * SUBLANES, SUBLANES), SUBLANES)
            a = a_ref[rows, cs]
            h = a * h + b_ref[rows, cs]
            pp = a * pp
            b_ref[rows, cs] = h
            a_ref[rows, cs] = pp
            return h, pp

        hc, pc = lax.fori_loop(0, seg_len, step,
                               (jnp.zeros((SUBLANES, chunk), F32), jnp.ones((SUBLANES, chunk), F32)),
                               unroll=SCAN_UNROLL)
        for d in (1, 2, 4):
            keep = sub8 >= d
            hs = jnp.where(keep, pltpu.roll(hc, d, 0), 0.0)
            ps = jnp.where(keep, pltpu.roll(pc, d, 0), 1.0)
            hc = pc * hs + hc
            pc = pc * ps
        cin = hcar_ref[:, cs]
        end = hc + pc * cin
        seg_in = jnp.where(sub8 >= 1, pltpu.roll(end, 1, 0), cin)
        hcar_ref[:, cs] = end[SUBLANES - 1:SUBLANES, :]

        def fix(j, carry, cs=cs, seg_in=seg_in):
            rows = pl.ds(pl.multiple_of(j * SUBLANES, SUBLANES), SUBLANES)
            b_ref[rows, cs] = b_ref[rows, cs] + a_ref[rows, cs] * seg_in
            return carry

        lax.fori_loop(0, seg_len, fix, 0, unroll=SCAN_UNROLL)

    y = (b_ref[...] * jax.nn.gelu(gate_ref[...])).astype(BF16)
    y = jnp.dot(_perm_matrix(ts, seg_len, True), y, preferred_element_type=F32).astype(BF16)
    o_ref[...] = x_ref[...] + jnp.dot(y, wout_ref[...], preferred_element_type=F32)


def _block_diag(w, per_block):
    h, d, _ = w.shape
    nb = h // per_block
    eye = jnp.eye(per_block, dtype=w.dtype)
    m = jnp.einsum('bhij,hk->bhikj', w.reshape(nb, per_block, d, d), eye)
    return m.reshape(nb, per_block * d, per_block * d)


def _rglru_layer(x, norm_g, w_in, conv_w, conv_b, wa, ba, wx, bx, lru_l, w_out):
    t, d = x.shape
    c = w_out.shape[0]
    heads, hd, _ = wa.shape
    per_block = LANES // math.gcd(hd, LANES)
    ts = SCAN_TILE
    tm = min(IN_TILE_M, t)
    tn = IN_TILE_N

    proj = pl.pallas_call(
        functools.partial(_lru_in_kernel, ts=ts),
        grid=(t // tm, (2 * c) // tn),
        in_specs=[pl.BlockSpec((tm, d), lambda i, j: (i, 0)),
                  pl.BlockSpec((1, d), lambda i, j: (0, 0)),
                  pl.BlockSpec((d, tn), lambda i, j: (0, j))],
        out_specs=pl.BlockSpec((tm, tn), lambda i, j: (i, j)),
        out_shape=jax.ShapeDtypeStruct((t, 2 * c), F32),
        scratch_shapes=[pltpu.VMEM((tm, d), BF16)],
        compiler_params=_cparams(("parallel", "arbitrary")),
        name="lru_in",
    )(x, norm_g.reshape(1, d), w_in.astype(BF16))

    wg = jnp.concatenate([_block_diag(wa, per_block), _block_diag(wx, per_block)], axis=-1).astype(BF16)
    nblk, bw, _ = wg.shape
    halo = (CONV_WIDTH - 1) * SUBLANES
    row = lambda v: v.reshape(1, c)
    return pl.pallas_call(
        functools.partial(_lru_rec_kernel, chunk=bw * 2),
        grid=(t // ts,),
        in_specs=[pl.BlockSpec((ts, c), lambda i: (i, 0)),
                  pl.BlockSpec((ts, c), lambda i: (i, 1)),
                  pl.BlockSpec((ts, d), lambda i: (i, 0)),
                  _const_spec((CONV_WIDTH, c)), _const_spec((1, c)),
                  _const_spec((nblk, bw, 2 * bw)),
                  _const_spec((1, c)), _const_spec((1, c)), _const_spec((1, c)),
                  _const_spec((c, d))],
        out_specs=pl.BlockSpec((ts, d), lambda i: (i, 0)),
        out_shape=jax.ShapeDtypeStruct((t, d), F32),
        scratch_shapes=[pltpu.VMEM((ts + halo, c), F32), pltpu.VMEM((ts, c), F32),
                        pltpu.VMEM((ts, c), F32), pltpu.VMEM((halo, c), F32),
                        pltpu.VMEM((1, c), F32)],
        compiler_params=_cparams(("arbitrary",)),
        name="lru_rec",
    )(proj, proj, x, conv_w, row(conv_b), wg, row(ba), row(bx), row(lru_l), w_out.astype(BF16))


def _s5_disc_kernel(lre_ref, lim_ref, dt_ref, bre_ref, bim_ref, abr_ref, abi_ref, bbr_ref, bbi_ref):
    lre = jnp.minimum(lre_ref[...], -1e-4)
    lim = lim_ref[...]
    dt = jnp.exp(dt_ref[...])
    mag = jnp.exp(lre * dt)
    abr = mag * jnp.cos(lim * dt)
    abi = mag * jnp.sin(lim * dt)
    den = lre * lre + lim * lim
    kr = ((abr - 1.0) * lre + abi * lim) / den
    ki = (abi * lre - (abr - 1.0) * lim) / den
    abr_ref[...] = abr
    abi_ref[...] = abi
    bbr_ref[...] = kr * bre_ref[...] - ki * bim_ref[...]
    bbi_ref[...] = kr * bim_ref[...] + ki * bre_ref[...]


def _s5_kernel(dest_ref, x_ref, slab_ref, eo_ref, g_ref, win_ref, br_ref, bi_ref, cr_ref, ci_ref,
               d_ref, abr_ref, abi_ref, wglu_ref, o_ref, u_ref, z_ref, xr_ref, xi_ref, a8r_ref,
               a8i_ref, asr_ref, asi_ref, carr_ref, cari_ref, buf_ref, sem, *, chunk):
    ts, d_model = x_ref.shape
    tile = pl.program_id(0)
    last = pl.num_programs(0) - 1
    slot = tile & 1
    seg_len = ts // SUBLANES
    ns = xr_ref.shape[1]
    nb_in, kin, nin = br_ref.shape
    nb_out, kout, nout = cr_ref.shape

    @pl.when(tile == 0)
    def _():
        _combine_start(dest_ref, eo_ref, buf_ref, sem, 0, 0)
        carr_ref[...] = jnp.zeros_like(carr_ref)
        cari_ref[...] = jnp.zeros_like(cari_ref)
        ar, ai = abr_ref[...], abi_ref[...]
        a8r_ref[...] = jnp.broadcast_to(ar, a8r_ref.shape)
        a8i_ref[...] = jnp.broadcast_to(ai, a8i_ref.shape)

        def power(j, cur):
            pr, pi = cur
            return pr * ar - pi * ai, pr * ai + pi * ar

        lr, li = lax.fori_loop(0, seg_len - 1, power, (ar, ai))

        def seg_powers(s, cur):
            pr, pi = cur
            asr_ref[pl.ds(s, 1), :] = pr
            asi_ref[pl.ds(s, 1), :] = pi
            return pr * lr - pi * li, pr * li + pi * lr

        lax.fori_loop(0, SUBLANES, seg_powers, (lr, li))

    _combine_wait(eo_ref, buf_ref, sem, slot)
    x = _combined_tile(x_ref[...], slab_ref[...], buf_ref, slot)
    o_ref[...] = x
    _combine_start(dest_ref, eo_ref, buf_ref, sem, jnp.minimum(tile + 1, last), 1 - slot)
    xn = _rms(x, g_ref[...]).astype(BF16)
    xp = jnp.dot(_perm_matrix(ts, seg_len, False), xn, preferred_element_type=F32).astype(BF16)
    u = jnp.dot(xp, win_ref[...], preferred_element_type=F32)
    u_ref[...] = u
    ub = u.astype(BF16)
    for blk in range(nb_in):
        uc = ub[:, blk * kin:(blk + 1) * kin]
        xr_ref[:, blk * nin:(blk + 1) * nin] = jnp.dot(uc, br_ref[blk], preferred_element_type=F32)
        xi_ref[:, blk * nin:(blk + 1) * nin] = jnp.dot(uc, bi_ref[blk], preferred_element_type=F32)

    sub8 = _sublane_id((SUBLANES, chunk))
    for ch in range(ns // chunk):
        cs = slice(ch * chunk, (ch + 1) * chunk)
        ar, ai = a8r_ref[:, cs], a8i_ref[:, cs]

        def step(j, carry, cs=cs, ar=ar, ai=ai):
            sr, si = carry
            rows = pl.ds(pl.multiple_of(j * SUBLANES, SUBLANES), SUBLANES)
            nr = ar * sr - ai * si + xr_ref[rows, cs]
            ni = ar * si + ai * sr + xi_ref[rows, cs]
            xr_ref[rows, cs] = nr
            xi_ref[rows, cs] = ni
            return nr, ni

        zero = jnp.zeros((SUBLANES, chunk), F32)
        er, ei = lax.fori_loop(0, seg_len, step, (zero, zero), unroll=SCAN_UNROLL)
        for d in (1, 2, 4):
            keep = sub8 >= d
            pr, pi = asr_ref[d - 1:d, cs], asi_ref[d - 1:d, cs]
            sr = jnp.where(keep, pltpu.roll(er, d, 0), 0.0)
            si = jnp.where(keep, pltpu.roll(ei, d, 0), 0.0)
            er, ei = er + pr * sr - pi * si, ei + pr * si + pi * sr
        cr, ci = carr_ref[:, cs], cari_ref[:, cs]
        pr, pi = asr_ref[:, cs], asi_ref[:, cs]
        er, ei = er + pr * cr - pi * ci, ei + pr * ci + pi * cr
        inr = jnp.where(sub8 >= 1, pltpu.roll(er, 1, 0), cr)
        ini = jnp.where(sub8 >= 1, pltpu.roll(ei, 1, 0), ci)
        carr_ref[:, cs] = er[SUBLANES - 1:SUBLANES, :]
        cari_ref[:, cs] = ei[SUBLANES - 1:SUBLANES, :]

        def fix(j, carry, cs=cs, ar=ar, ai=ai):
            fr, fi = carry
            fr, fi = ar * fr - ai * fi, ar * fi + ai * fr
            rows = pl.ds(pl.multiple_of(j * SUBLANES, SUBLANES), SUBLANES)
            xr_ref[rows, cs] = xr_ref[rows, cs] + fr
            xi_ref[rows, cs] = xi_ref[rows, cs] + fi
            return fr, fi

        lax.fori_loop(0, seg_len, fix, (inr, ini), unroll=SCAN_UNROLL)

    for blk in range(nb_out):
        ks = slice(blk * kout, (blk + 1) * kout)
        ys = slice(blk * nout, (blk + 1) * nout)
        y = jnp.dot(xr_ref[:, ks].astype(BF16), cr_ref[blk], preferred_element_type=F32)
        y = y + jnp.dot(xi_ref[:, ks].astype(BF16), ci_ref[blk], preferred_element_type=F32)
        y = y + d_ref[:, ys] * u_ref[:, ys]
        z_ref[:, ys] = jax.nn.gelu(y).astype(BF16)

    zn = jnp.dot(_perm_matrix(ts, seg_len, True), z_ref[...], preferred_element_type=F32).astype(BF16)
    vg = jnp.dot(zn, wglu_ref[...], preferred_element_type=F32)
    o_ref[...] = o_ref[...] + vg[:, :d_model] * jax.nn.sigmoid(vg[:, d_model:])

    @pl.when(tile == last)
    def _():
        _combine_wait(eo_ref, buf_ref, sem, 1 - slot)


def _s5_layer(pending, norm_g, w_in, lam_re, lam_im, log_dt, b_re, b_im, c_re, c_im, d_skip, w_glu):
    x, slab, dest, eo = pending
    t, d = x.shape
    groups, p, k = b_re.shape
    d_ssm = groups * k
    ns = groups * p
    ts = SCAN_TILE
    seg_len = ts // SUBLANES

    col = lambda v: v.reshape(ns, 1).astype(F32)
    log_dt_col = jnp.repeat(log_dt, p)
    rb = 512
    abr, abi, bbr, bbi = pl.pallas_call(
        _s5_disc_kernel,
        grid=(ns // rb,),
        in_specs=[pl.BlockSpec((rb, 1), lambda i: (i, 0))] * 3 + [pl.BlockSpec((rb, k), lambda i: (i, 0))] * 2,
        out_specs=[pl.BlockSpec((rb, 1), lambda i: (i, 0))] * 2 + [pl.BlockSpec((rb, k), lambda i: (i, 0))] * 2,
        out_shape=[jax.ShapeDtypeStruct((ns, 1), F32)] * 2 + [jax.ShapeDtypeStruct((ns, k), F32)] * 2,
        compiler_params=_cparams(("parallel",)),
        name="s5_disc",
    )(col(lam_re), col(lam_im), col(log_dt_col), b_re.reshape(ns, k).astype(F32), b_im.reshape(ns, k).astype(F32))

    gb = 256 // k
    nb = groups // gb
    eye = jnp.eye(gb, dtype=F32)

    def in_blocks(bb):
        m = jnp.einsum('cgpk,gh->cgkhp', bb.reshape(nb, gb, p, k), eye)
        return m.reshape(nb, gb * k, gb * p).astype(BF16)

    def out_blocks(cc):
        m = jnp.einsum('cgkp,gh->cgphk', cc.reshape(nb, gb, k, p).astype(F32), eye)
        return m.reshape(nb, gb * p, gb * k).astype(BF16)

    chunk = 512
    return pl.pallas_call(
        functools.partial(_s5_kernel, chunk=chunk),
        grid_spec=pltpu.PrefetchScalarGridSpec(
            num_scalar_prefetch=1, grid=(t // ts,),
            in_specs=[pl.BlockSpec((ts, d), lambda i, dst: (i, 0)),
                      pl.BlockSpec((ts, ROUTE_LANES), lambda i, dst: (i, 0)),
                      pl.BlockSpec(memory_space=pl.ANY),
                      _const_spec((1, d)), _const_spec((d, d_ssm)),
                      _const_spec((nb, gb * k, gb * p)), _const_spec((nb, gb * k, gb * p)),
                      _const_spec((nb, gb * p, gb * k)), _const_spec((nb, gb * p, gb * k)),
                      _const_spec((1, d_ssm)), _const_spec((1, ns)), _const_spec((1, ns)),
                      _const_spec((d_ssm, 2 * d))],
            out_specs=pl.BlockSpec((ts, d), lambda i, dst: (i, 0)),
            scratch_shapes=[pltpu.VMEM((ts, d_ssm), F32), pltpu.VMEM((ts, d_ssm), BF16),
                            pltpu.VMEM((ts, ns), F32), pltpu.VMEM((ts, ns), F32),
                            pltpu.VMEM((SUBLANES, ns), F32), pltpu.VMEM((SUBLANES, ns), F32),
                            pltpu.VMEM((SUBLANES, ns), F32), pltpu.VMEM((SUBLANES, ns), F32),
                            pltpu.VMEM((1, ns), F32), pltpu.VMEM((1, ns), F32),
                            pltpu.VMEM((2, TOP_K, ts, d), F32), pltpu.SemaphoreType.DMA((2, TOP_K))]),
        out_shape=jax.ShapeDtypeStruct((t, d), F32),
        compiler_params=_cparams(("arbitrary",)),
        name="s5_mix",
    )(dest, x, slab, eo, norm_g.reshape(1, d), w_in.astype(BF16), in_blocks(bbr), in_blocks(bbi),
      out_blocks(c_re), out_blocks(-c_im), d_skip.reshape(1, d_ssm).astype(F32),
      abr.reshape(1, ns), abi.reshape(1, ns), w_glu.astype(BF16))


def _route_kernel(x_ref, g_ref, wr_ref, br_ref, slab_ref, cnt_ref, base_ref, *, n_groups, epg):
    @pl.when(pl.program_id(0) == 0)
    def _():
        base_ref[...] = jnp.zeros_like(base_ref)

    tr = x_ref.shape[0]
    xn = _rms(x_ref[...], g_ref[...])
    x_hi = xn.astype(BF16)
    x_lo = (xn - x_hi.astype(F32)).astype(BF16)
    hi = jnp.dot(x_hi, wr_ref[...], preferred_element_type=F32)
    lo = jnp.dot(x_lo, wr_ref[:, :ROUTE_LANES], preferred_element_type=F32)
    logits = hi[:, :ROUTE_LANES] + (hi[:, ROUTE_LANES:] + lo) + br_ref[...]
    lane = lax.broadcasted_iota(I32, logits.shape, 1)
    neg = jnp.float32(-3.0e38)

    def first_argmax(v):
        m = jnp.max(v, axis=1, keepdims=True)
        return m, jnp.min(jnp.where(v == m, lane, ROUTE_LANES), axis=1, keepdims=True)

    gmask = lane < n_groups
    gmax, gidx = first_argmax(jnp.where(gmask, logits, neg))
    p_group = 1.0 / jnp.sum(jnp.where(gmask, jnp.exp(logits - gmax), 0.0), axis=1, keepdims=True)
    lo = n_groups + gidx * epg
    el = jnp.where((lane >= lo) & (lane < lo + epg), logits, neg)
    m1, i1 = first_argmax(el)
    m2, i2 = first_argmax(jnp.where(lane == i1, neg, el))
    t2 = jnp.exp(m2 - m1)
    w1 = p_group / (1.0 + t2)
    w2 = p_group * t2 / (1.0 + t2)

    hit1, hit2 = lane == i1, lane == i2
    onehot = (hit1 | hit2).astype(BF16)
    r = lax.broadcasted_iota(I32, (tr, tr), 0)
    c = lax.broadcasted_iota(I32, (tr, tr), 1)
    before = jnp.dot((c < r).astype(BF16), onehot, preferred_element_type=F32) + base_ref[0:1, :]
    rank1 = jnp.sum(jnp.where(hit1, before, 0.0), axis=1, keepdims=True)
    rank2 = jnp.sum(jnp.where(hit2, before, 0.0), axis=1, keepdims=True)
    total = base_ref[0:1, :] + jnp.sum(onehot.astype(F32), axis=0, keepdims=True)
    base_ref[...] = jnp.broadcast_to(total, base_ref.shape)
    cnt_ref[...] = jnp.broadcast_to(total, cnt_ref.shape)

    vals = (w1, w2, (i1 - n_groups).astype(F32), (i2 - n_groups).astype(F32), rank1, rank2)
    slab = jnp.zeros(logits.shape, F32)
    for k, v in enumerate(vals):
        slab = jnp.where(lane == k, v, slab)
    slab_ref[...] = slab


def _row_gather_start(src_ref, dst_ref, sem, index_of, rows):
    for r in range(rows):
        pltpu.make_async_copy(src_ref.at[pl.ds(index_of(r), 1)], dst_ref.at[pl.ds(r, 1)], sem).start()


def _row_gather_wait(src_ref, dst_ref, sem):
    pltpu.make_async_copy(src_ref.at[pl.ds(0, dst_ref.shape[0])], dst_ref, sem).wait()


def _gmm_kernel(blk_ref, exp_ref, lo_ref, hi_ref, nexp_ref, wslot_ref, dest_ref, x_ref, g_ref,
                w1_ref, w3_ref, w2_ref, o_ref, xbuf_ref, w1f_ref, w3f_ref, w2f_ref,
                w1b_ref, w3b_ref, w2b_ref, tok_ref, sem, wsem, *, layer):
    n = pl.program_id(0)
    last = pl.num_programs(0) - 1
    prev = jnp.maximum(n - 1, 0)
    nslots = xbuf_ref.shape[0]
    slot = lax.rem(n, nslots)
    rows = xbuf_ref.shape[1]
    new_expert = (n == 0) | (exp_ref[n] != exp_ref[prev])
    new_block = (n == 0) | (blk_ref[n] != blk_ref[prev])

    def gather(item, into):
        base = blk_ref[item] * rows
        _row_gather_start(x_ref, xbuf_ref.at[into], sem.at[into], lambda r: tok_ref[base + r], rows)

    def weight_copies(expert, into):
        return [pltpu.make_async_copy(src.at[layer, expert], dst.at[into], wsem.at[into, j])
                for j, (src, dst) in enumerate(((w1_ref, w1f_ref), (w3_ref, w3f_ref), (w2_ref, w2f_ref)))]

    @pl.when(n == 0)
    def _():
        def invert(t, carry):
            for k in range(TOP_K):
                tok_ref[dest_ref[t * TOP_K + k]] = t
            return carry

        for cp in weight_copies(exp_ref[0], wslot_ref[0]):
            cp.start()
        lax.fori_loop(0, tok_ref.shape[0] // TOP_K, invert, 0, unroll=8)
        for ahead in range(GMM_AHEAD):
            gather(jnp.minimum(ahead, last), ahead)

    @pl.when(new_expert)
    def _():
        ws = wslot_ref[n]
        for cp in weight_copies(exp_ref[n], ws):
            cp.wait()
        w1b_ref[...] = w1f_ref[ws].astype(BF16)
        w3b_ref[...] = w3f_ref[ws].astype(BF16)
        w2b_ref[...] = w2f_ref[ws].astype(BF16)

        @pl.when(nexp_ref[n] != exp_ref[n])
        def _():
            for cp in weight_copies(nexp_ref[n], 1 - ws):
                cp.start()

    _row_gather_wait(x_ref, xbuf_ref.at[slot], sem.at[slot])
    xb = _rms(xbuf_ref[slot], g_ref[...]).astype(BF16)
    gather(jnp.minimum(n + GMM_AHEAD, last), lax.rem(n + GMM_AHEAD, nslots))
    h = jax.nn.silu(jnp.dot(xb, w1b_ref[...], preferred_element_type=F32))
    h = (h * jnp.dot(xb, w3b_ref[...], preferred_element_type=F32)).astype(BF16)
    out = jnp.dot(h, w2b_ref[...], preferred_element_type=F32)
    row = lax.broadcasted_iota(I32, (out.shape[0], 1), 0)
    out = jnp.where((row >= lo_ref[n]) & (row < hi_ref[n]), out, 0.0)

    @pl.when(new_block)
    def _():
        o_ref[...] = out

    @pl.when(jnp.logical_not(new_block))
    def _():
        o_ref[...] = o_ref[...] + out

    @pl.when(n == last)
    def _():
        for ahead in range(1, GMM_AHEAD + 1):
            other = lax.rem(n + ahead, nslots)
            _row_gather_wait(x_ref, xbuf_ref.at[other], sem.at[other])


def _combine_start(dest_ref, eo_ref, buf_ref, sem, tile, into):
    rows = buf_ref.shape[2]
    base = tile * (rows * TOP_K)
    for k in range(TOP_K):
        _row_gather_start(eo_ref, buf_ref.at[into, k], sem.at[into, k],
                          lambda r, k=k: dest_ref[base + r * TOP_K + k], rows)


def _combine_wait(eo_ref, buf_ref, sem, slot):
    for k in range(TOP_K):
        _row_gather_wait(eo_ref, buf_ref.at[slot, k], sem.at[slot, k])


def _combined_tile(x, slab, buf_ref, slot):
    return x + slab[:, 0:1] * buf_ref[slot, 0] + slab[:, 1:2] * buf_ref[slot, 1]


def _combine_kernel(dest_ref, x_ref, slab_ref, g_ref, eo_ref, o_ref, buf_ref, sem):
    i = pl.program_id(0)
    last = pl.num_programs(0) - 1
    slot = i & 1

    @pl.when(i == 0)
    def _():
        _combine_start(dest_ref, eo_ref, buf_ref, sem, 0, 0)

    _combine_wait(eo_ref, buf_ref, sem, slot)
    y = _combined_tile(x_ref[...], slab_ref[...], buf_ref, slot)
    _combine_start(dest_ref, eo_ref, buf_ref, sem, jnp.minimum(i + 1, last), 1 - slot)
    o_ref[...] = _rms(y, g_ref[...])

    @pl.when(i == last)
    def _():
        _combine_wait(eo_ref, buf_ref, sem, 1 - slot)


def _gmm_schedule(counts, n_rows, block):
    n_exp = counts.shape[0]
    n_blocks = n_rows // block
    n_items = n_blocks + n_exp - 1
    ends = jnp.cumsum(counts)
    starts = ends - counts
    first_blk = starts // block
    last_blk = jnp.maximum(ends - 1, 0) // block
    per_exp = jnp.where(counts > 0, last_blk - first_blk + 1, 0)
    item_end = jnp.cumsum(per_exp)
    item_start = item_end - per_exp
    n = jnp.arange(n_items, dtype=I32)
    valid = n < item_end[-1]
    e = jnp.minimum(jnp.sum(item_end[None, :] <= n[:, None], axis=1), n_exp - 1).astype(I32)
    blk = first_blk[e] + (n - item_start[e])
    lo = jnp.maximum(starts[e], blk * block) - blk * block
    hi = jnp.minimum(ends[e], (blk + 1) * block) - blk * block
    last = jnp.maximum(item_end[-1] - 1, 0)
    blk = jnp.where(valid, blk, blk[last]).astype(I32)
    e = jnp.where(valid, e, e[last]).astype(I32)
    lo = jnp.where(valid, lo, 0).astype(I32)
    hi = jnp.where(valid, hi, 0).astype(I32)
    ids = jnp.arange(n_exp, dtype=I32)
    later = jnp.where((per_exp > 0)[None, :] & (ids[None, :] > e[:, None]), ids[None, :], n_exp)
    nxt = jnp.min(later, axis=1)
    nxt = jnp.where(nxt == n_exp, e, nxt).astype(I32)
    wslot = (jnp.sum(((per_exp > 0)[None, :] & (ids[None, :] < e[:, None])).astype(I32), axis=1) & 1)
    return starts, (blk, e, lo, hi, nxt, wslot.astype(I32))


def _moe_layer(x, norm_g, wg, bg, we, be, w1, w3, w2, layer):
    t, d = x.shape
    n_groups, _, epg = we.shape
    n_exp = n_groups * epg
    f = w1.shape[-1]
    tr = ROUTE_TILE
    n_rows = t * TOP_K
    g_row = norm_g.reshape(1, d)

    wr = jnp.concatenate([wg, jnp.transpose(we, (1, 0, 2)).reshape(d, n_exp)], axis=1)
    wr = jnp.pad(wr, ((0, 0), (0, ROUTE_LANES - wr.shape[1]))).astype(F32)
    wr_hi = wr.astype(BF16)
    wr = jnp.concatenate([wr_hi, (wr - wr_hi.astype(F32)).astype(BF16)], axis=1)
    br = jnp.pad(jnp.concatenate([bg, be.reshape(-1)]), (0, ROUTE_LANES - n_groups - n_exp))

    slab, cnt = pl.pallas_call(
        functools.partial(_route_kernel, n_groups=n_groups, epg=epg),
        grid=(t // tr,),
        in_specs=[pl.BlockSpec((tr, d), lambda i: (i, 0)), _const_spec((1, d)),
                  _const_spec((d, 2 * ROUTE_LANES)), _const_spec((1, ROUTE_LANES))],
        out_specs=[pl.BlockSpec((tr, ROUTE_LANES), lambda i: (i, 0)),
                   pl.BlockSpec((SUBLANES, ROUTE_LANES), lambda i: (0, 0))],
        out_shape=[jax.ShapeDtypeStruct((t, ROUTE_LANES), F32),
                   jax.ShapeDtypeStruct((SUBLANES, ROUTE_LANES), F32)],
        scratch_shapes=[pltpu.VMEM((SUBLANES, ROUTE_LANES), F32)],
        compiler_params=_cparams(("arbitrary",)),
        name="moe_route",
    )(x, g_row, wr, br.reshape(1, ROUTE_LANES).astype(F32))

    counts = cnt[0, n_groups:n_groups + n_exp].astype(I32)
    starts, items = _gmm_schedule(counts, n_rows, GMM_BLOCK)
    e_idx = slab[:, 2:2 + TOP_K].astype(I32)
    start_of = jnp.sum(jnp.where(e_idx[..., None] == jnp.arange(n_exp, dtype=I32), starts, 0), axis=-1)
    dest = (start_of + slab[:, 4:4 + TOP_K].astype(I32)).reshape(-1)

    n_items = items[0].shape[0]
    any_spec = pl.BlockSpec(memory_space=pl.ANY)
    eo = pl.pallas_call(
        functools.partial(_gmm_kernel, layer=layer),
        grid_spec=pltpu.PrefetchScalarGridSpec(
            num_scalar_prefetch=len(items) + 1, grid=(n_items,),
            in_specs=[any_spec, pl.BlockSpec((1, d), lambda n, *_: (0, 0)),
                      any_spec, any_spec, any_spec],
            out_specs=pl.BlockSpec((GMM_BLOCK, d), lambda n, blk, *_: (blk[n], 0)),
            scratch_shapes=[pltpu.VMEM((GMM_AHEAD + 1, GMM_BLOCK, d), F32),
                            pltpu.VMEM((2, d, f), F32), pltpu.VMEM((2, d, f), F32),
                            pltpu.VMEM((2, f, d), F32),
                            pltpu.VMEM((d, f), BF16), pltpu.VMEM((d, f), BF16), pltpu.VMEM((f, d), BF16),
                            pltpu.SMEM((n_rows,), I32),
                            pltpu.SemaphoreType.DMA((GMM_AHEAD + 1,)),
                            pltpu.SemaphoreType.DMA((2, 3))]),
        out_shape=jax.ShapeDtypeStruct((n_rows, d), F32),
        compiler_params=_cparams(("arbitrary",)),
        name="moe_gmm",
    )(*items, dest, x, g_row, w1, w3, w2)
    return x, slab, dest, eo


def _moe_combine_norm(pending, final_g):
    x, slab, dest, eo = pending
    t, d = x.shape
    tr = ROUTE_TILE
    return pl.pallas_call(
        _combine_kernel,
        grid_spec=pltpu.PrefetchScalarGridSpec(
            num_scalar_prefetch=1, grid=(t // tr,),
            in_specs=[pl.BlockSpec((tr, d), lambda i, dst: (i, 0)),
                      pl.BlockSpec((tr, ROUTE_LANES), lambda i, dst: (i, 0)),
                      pl.BlockSpec((1, d), lambda i, dst: (0, 0)),
                      pl.BlockSpec(memory_space=pl.ANY)],
            out_specs=pl.BlockSpec((tr, d), lambda i, dst: (i, 0)),
            scratch_shapes=[pltpu.VMEM((2, TOP_K, tr, d), F32), pltpu.SemaphoreType.DMA((2, TOP_K))]),
        out_shape=jax.ShapeDtypeStruct((t, d), F32),
        compiler_params=_cparams(("arbitrary",)),
        name="moe_combine",
    )(dest, x, slab, final_g.reshape(1, d), eo)


def kernel(x, mix_norm, ffn_norm, final_norm, a_w_in, a_conv_w, a_conv_b, a_gate_a_w, a_gate_a_b, a_gate_x_w, a_gate_x_b, a_lru_L, a_w_out, b_w_in, b_lam_re, b_lam_im, b_log_dt, b_b_re, b_b_im, b_c_re, b_c_im, b_d, b_w_glu, moe_wg, moe_bg, moe_we, moe_be, moe_w1, moe_w3, moe_w2):
    batch, seq, d = x.shape
    depth = mix_norm.shape[0]
    assert depth == 2, "the combine of each MoE layer is fused into the S5 mixer / final norm"
    outs = []
    for b in range(batch):
        h = _rglru_layer(x[b], mix_norm[0], a_w_in[0], a_conv_w[0], a_conv_b[0], a_gate_a_w[0],
                         a_gate_a_b[0], a_gate_x_w[0], a_gate_x_b[0], a_lru_L[0], a_w_out[0])
        pending = _moe_layer(h, ffn_norm[0], moe_wg[0], moe_bg[0], moe_we[0], moe_be[0],
                             moe_w1, moe_w3, moe_w2, 0)
        h = _s5_layer(pending, mix_norm[1], b_w_in[0], b_lam_re[0], b_lam_im[0], b_log_dt[0],
                      b_b_re[0], b_b_im[0], b_c_re[0], b_c_im[0], b_d[0], b_w_glu[0])
        pending = _moe_layer(h, ffn_norm[1], moe_wg[1], moe_bg[1], moe_we[1], moe_be[1],
                             moe_w1, moe_w3, moe_w2, 1)
        outs.append(_moe_combine_norm(pending, final_norm))
    return jnp.stack(outs)
```

```python
import functools
import math

import jax
import jax.numpy as jnp
from jax import lax
from jax.experimental import pallas as pl
from jax.experimental.pallas import tpu as pltpu

F32 = jnp.float32
BF16 = jnp.bfloat16
I32 = jnp.int32

EPS = 1e-6
LRU_C = 8.0
CONV_WIDTH = 4
TOP_K = 2

LANES = 128
SUBLANES = 8
VMEM_LIMIT = 56 * 1024 * 1024

SCAN_TILE = 256
IN_TILE_M = 1024
IN_TILE_N = 512
ROUTER_TILE = 512
ROUTE_TILE = 256
GMM_BLOCK = 256
GMM_AHEAD = 2
COMBINE_AHEAD = 2
SCAN_UNROLL = True
ROUTE_LANES = 128


def _cparams(sem):
    return pltpu.CompilerParams(dimension_semantics=sem, vmem_limit_bytes=VMEM_LIMIT)


def _const_spec(shape):
    nd = len(shape)
    return pl.BlockSpec(shape, lambda *_: (0,) * nd, pipeline_mode=pl.Buffered(1))


def _rms(x, g):
    ms = jnp.mean(x * x, axis=-1, keepdims=True)
    return x * lax.rsqrt(ms + EPS) * g


def _perm_matrix(ts, seg_len, inverse):
    r = lax.broadcasted_iota(I32, (ts, ts), 1 if inverse else 0)
    n = lax.broadcasted_iota(I32, (ts, ts), 0 if inverse else 1)
    return (n == (r & (SUBLANES - 1)) * seg_len + (r >> 3)).astype(BF16)


def _sublane_id(shape):
    return lax.broadcasted_iota(I32, shape, 0) & (SUBLANES - 1)


def _lru_in_kernel(x_ref, g_ref, w_ref, o_ref, xp_ref, *, ts):
    @pl.when(pl.program_id(1) == 0)
    def _():
        p = _perm_matrix(ts, ts // SUBLANES, False)
        for s in range(x_ref.shape[0] // ts):
            rows = slice(s * ts, (s + 1) * ts)
            xn = _rms(x_ref[rows, :], g_ref[...]).astype(BF16)
            xp_ref[rows, :] = jnp.dot(p, xn, preferred_element_type=F32).astype(BF16)

    o_ref[...] = jnp.dot(xp_ref[...], w_ref[...], preferred_element_type=F32)


def _lru_rec_kernel(gate_ref, rec_ref, x_ref, cw_ref, cb_ref, wg_ref, ba_ref, bx_ref, lam_ref,
                    wout_ref, o_ref, ext_ref, a_ref, b_ref, tail_ref, hcar_ref, *, chunk):
    ts, c = rec_ref.shape
    seg_len = ts // SUBLANES
    halo = (CONV_WIDTH - 1) * SUBLANES
    nblk, bw, _ = wg_ref.shape

    @pl.when(pl.program_id(0) == 0)
    def _():
        tail_ref[...] = jnp.zeros_like(tail_ref)
        hcar_ref[...] = jnp.zeros_like(hcar_ref)

    tail = rec_ref[ts - halo:ts, :]
    sub = _sublane_id((halo, c))
    ext_ref[0:halo, :] = jnp.where(sub == 0, pltpu.roll(tail_ref[...], halo - (SUBLANES - 1), 0),
                                   pltpu.roll(tail, 1, 0))
    ext_ref[halo:halo + ts, :] = rec_ref[...]
    tail_ref[...] = tail

    z = -lam_ref[...]
    softplus = jnp.maximum(z, 0.0) + jnp.log1p(jnp.exp(-jnp.abs(z)))

    for blk in range(nblk):
        cs = slice(blk * bw, (blk + 1) * bw)
        conv = cb_ref[:, cs] + cw_ref[0:1, cs] * ext_ref[0:ts, cs]
        for k in range(1, CONV_WIDTH):
            conv = conv + cw_ref[k:k + 1, cs] * ext_ref[k * SUBLANES:k * SUBLANES + ts, cs]
        g = jnp.dot(conv.astype(BF16), wg_ref[blk], preferred_element_type=F32)
        r = jax.nn.sigmoid(g[:, :bw] + ba_ref[:, cs])
        ig = jax.nn.sigmoid(g[:, bw:] + bx_ref[:, cs])
        log_a = (-LRU_C * r) * softplus[:, cs]
        a_ref[:, cs] = jnp.exp(log_a)
        th = jnp.tanh(log_a)
        b_ref[:, cs] = jnp.sqrt(-2.0 * th / (1.0 - th)) * (ig * conv)

    sub8 = _sublane_id((SUBLANES, chunk))
    for ch in range(c // chunk):
        cs = slice(ch * chunk, (ch + 1) * chunk)

        def step(j, carry, cs=cs):
            h, pp = carry
            rows = pl.ds(pl.multiple_of(j * SUBLANES, SUBLANES), SUBLANES)
            a = a_ref[rows, cs]
            h = a * h + b_ref[rows, cs]
            pp = a * pp
            b_ref[rows, cs] = h
            a_ref[rows, cs] = pp
            return h, pp

        hc, pc = lax.fori_loop(0, seg_len, step,
                               (jnp.zeros((SUBLANES, chunk), F32), jnp.ones((SUBLANES, chunk), F32)),
                               unroll=SCAN_UNROLL)
        for d in (1, 2, 4):
            keep = sub8 >= d
            hs = jnp.where(keep, pltpu.roll(hc, d, 0), 0.0)
            ps = jnp.where(keep, pltpu.roll(pc, d, 0), 1.0)
            hc = pc * hs + hc
            pc = pc * ps
        cin = hcar_ref[:, cs]
        end = hc + pc * cin
        seg_in = jnp.where(sub8 >= 1, pltpu.roll(end, 1, 0), cin)
        hcar_ref[:, cs] = end[SUBLANES - 1:SUBLANES, :]

        def fix(j, carry, cs=cs, seg_in=seg_in):
            rows = pl.ds(pl.multiple_of(j * SUBLANES, SUBLANES), SUBLANES)
            b_ref[rows, cs] = b_ref[rows, cs] + a_ref[rows, cs] * seg_in
            return carry

        lax.fori_loop(0, seg_len, fix, 0, unroll=SCAN_UNROLL)

    y = (b_ref[...] * jax.nn.gelu(gate_ref[...])).astype(BF16)
    y = jnp.dot(_perm_matrix(ts, seg_len, True), y, preferred_element_type=F32).astype(BF16)
    o_ref[...] = x_ref[...] + jnp.dot(y, wout_ref[...], preferred_element_type=F32)


def _block_diag(w, per_block):
    h, d, _ = w.shape
    nb = h // per_block
    eye = jnp.eye(per_block, dtype=w.dtype)
    m = jnp.einsum('bhij,hk->bhikj', w.reshape(nb, per_block, d, d), eye)
    return m.reshape(nb, per_block * d, per_block * d)


def _rglru_layer(x, norm_g, w_in, conv_w, conv_b, wa, ba, wx, bx, lru_l, w_out):
    t, d = x.shape
    c = w_out.shape[0]
    heads, hd, _ = wa.shape
    per_block = LANES // math.gcd(hd, LANES)
    ts = SCAN_TILE
    tm = min(IN_TILE_M, t)
    tn = IN_TILE_N

    proj = pl.pallas_call(
        functools.partial(_lru_in_kernel, ts=ts),
        grid=(t // tm, (2 * c) // tn),
        in_specs=[pl.BlockSpec((tm, d), lambda i, j: (i, 0)),
                  pl.BlockSpec((1, d), lambda i, j: (0, 0)),
                  pl.BlockSpec((d, tn), lambda i, j: (0, j))],
        out_specs=pl.BlockSpec((tm, tn), lambda i, j: (i, j)),
        out_shape=jax.ShapeDtypeStruct((t, 2 * c), F32),
        scratch_shapes=[pltpu.VMEM((tm, d), BF16)],
        compiler_params=_cparams(("parallel", "arbitrary")),
        name="lru_in",
    )(x, norm_g.reshape(1, d), w_in.astype(BF16))

    wg = jnp.concatenate([_block_diag(wa, per_block), _block_diag(wx, per_block)], axis=-1).astype(BF16)
    nblk, bw, _ = wg.shape
    halo = (CONV_WIDTH - 1) * SUBLANES
    row = lambda v: v.reshape(1, c)
    return pl.pallas_call(
        functools.partial(_lru_rec_kernel, chunk=bw * 2),
        grid=(t // ts,),
        in_specs=[pl.BlockSpec((ts, c), lambda i: (i, 0)),
                  pl.BlockSpec((ts, c), lambda i: (i, 1)),
                  pl.BlockSpec((ts, d), lambda i: (i, 0)),
                  _const_spec((CONV_WIDTH, c)), _const_spec((1, c)),
                  _const_spec((nblk, bw, 2 * bw)),
                  _const_spec((1, c)), _const_spec((1, c)), _const_spec((1, c)),
                  _const_spec((c, d))],
        out_specs=pl.BlockSpec((ts, d), lambda i: (i, 0)),
        out_shape=jax.ShapeDtypeStruct((t, d), F32),
        scratch_shapes=[pltpu.VMEM((ts + halo, c), F32), pltpu.VMEM((ts, c), F32),
                        pltpu.VMEM((ts, c), F32), pltpu.VMEM((halo, c), F32),
                        pltpu.VMEM((1, c), F32)],
        compiler_params=_cparams(("arbitrary",)),
        name="lru_rec",
    )(proj, proj, x, conv_w, row(conv_b), wg, row(ba), row(bx), row(lru_l), w_out.astype(BF16))


def _s5_disc_kernel(lre_ref, lim_ref, dt_ref, bre_ref, bim_ref, abr_ref, abi_ref, bbr_ref, bbi_ref):
    lre = jnp.minimum(lre_ref[...], -1e-4)
    lim = lim_ref[...]
    dt = jnp.exp(dt_ref[...])
    mag = jnp.exp(lre * dt)
    abr = mag * jnp.cos(lim * dt)
    abi = mag * jnp.sin(lim * dt)
    den = lre * lre + lim * lim
    kr = ((abr - 1.0) * lre + abi * lim) / den
    ki = (abi * lre - (abr - 1.0) * lim) / den
    abr_ref[...] = abr
    abi_ref[...] = abi
    bbr_ref[...] = kr * bre_ref[...] - ki * bim_ref[...]
    bbi_ref[...] = kr * bim_ref[...] + ki * bre_ref[...]


def _s5_kernel(dest_ref, x_ref, slab_ref, eo_ref, g_ref, win_ref, br_ref, bi_ref, cr_ref, ci_ref,
               d_ref, abr_ref, abi_ref, wglu_ref, o_ref, u_ref, z_ref, xr_ref, xi_ref, a8r_ref,
               a8i_ref, asr_ref, asi_ref, carr_ref, cari_ref, buf_ref, sem, *, chunk):
    ts, d_model = x_ref.shape
    tile = pl.program_id(0)
    last = pl.num_programs(0) - 1
    slot = tile & 1
    seg_len = ts // SUBLANES
    ns = xr_ref.shape[1]
    nb_in, kin, nin = br_ref.shape
    nb_out, kout, nout = cr_ref.shape

    @pl.when(tile == 0)
    def _():
        _combine_start(dest_ref, eo_ref, buf_ref, sem, 0, 0)
        carr_ref[...] = jnp.zeros_like(carr_ref)
        cari_ref[...] = jnp.zeros_like(cari_ref)
        ar, ai = abr_ref[...], abi_ref[...]
        a8r_ref[...] = jnp.broadcast_to(ar, a8r_ref.shape)
        a8i_ref[...] = jnp.broadcast_to(ai, a8i_ref.shape)

        def power(j, cur):
            pr, pi = cur
            return pr * ar - pi * ai, pr * ai + pi * ar

        lr, li = lax.fori_loop(0, seg_len - 1, power, (ar, ai))

        def seg_powers(s, cur):
            pr, pi = cur
            asr_ref[pl.ds(s, 1), :] = pr
            asi_ref[pl.ds(s, 1), :] = pi
            return pr * lr - pi * li, pr * li + pi * lr

        lax.fori_loop(0, SUBLANES, seg_powers, (lr, li))

    _combine_wait(eo_ref, buf_ref, sem, slot)
    x = _combined_tile(x_ref[...], slab_ref[...], buf_ref, slot)
    o_ref[...] = x
    _combine_start(dest_ref, eo_ref, buf_ref, sem, jnp.minimum(tile + 1, last), 1 - slot)
    xn = _rms(x, g_ref[...]).astype(BF16)
    xp = jnp.dot(_perm_matrix(ts, seg_len, False), xn, preferred_element_type=F32).astype(BF16)
    u = jnp.dot(xp, win_ref[...], preferred_element_type=F32)
    u_ref[...] = u
    ub = u.astype(BF16)
    for blk in range(nb_in):
        uc = ub[:, blk * kin:(blk + 1) * kin]
        xr_ref[:, blk * nin:(blk + 1) * nin] = jnp.dot(uc, br_ref[blk], preferred_element_type=F32)
        xi_ref[:, blk * nin:(blk + 1) * nin] = jnp.dot(uc, bi_ref[blk], preferred_element_type=F32)

    sub8 = _sublane_id((SUBLANES, chunk))
    for ch in range(ns // chunk):
        cs = slice(ch * chunk, (ch + 1) * chunk)
        ar, ai = a8r_ref[:, cs], a8i_ref[:, cs]

        def step(j, carry, cs=cs, ar=ar, ai=ai):
            sr, si = carry
            rows = pl.ds(pl.multiple_of(j * SUBLANES, SUBLANES), SUBLANES)
            nr = ar * sr - ai * si + xr_ref[rows, cs]
            ni = ar * si + ai * sr + xi_ref[rows, cs]
            xr_ref[rows, cs] = nr
            xi_ref[rows, cs] = ni
            return nr, ni

        zero = jnp.zeros((SUBLANES, chunk), F32)
        er, ei = lax.fori_loop(0, seg_len, step, (zero, zero), unroll=SCAN_UNROLL)
        for d in (1, 2, 4):
            keep = sub8 >= d
            pr, pi = asr_ref[d - 1:d, cs], asi_ref[d - 1:d, cs]
            sr = jnp.where(keep, pltpu.roll(er, d, 0), 0.0)
            si = jnp.where(keep, pltpu.roll(ei, d, 0), 0.0)
            er, ei = er + pr * sr - pi * si, ei + pr * si + pi * sr
        cr, ci = carr_ref[:, cs], cari_ref[:, cs]
        pr, pi = asr_ref[:, cs], asi_ref[:, cs]
        er, ei = er + pr * cr - pi * ci, ei + pr * ci + pi * cr
        inr = jnp.where(sub8 >= 1, pltpu.roll(er, 1, 0), cr)
        ini = jnp.where(sub8 >= 1, pltpu.roll(ei, 1, 0), ci)
        carr_ref[:, cs] = er[SUBLANES - 1:SUBLANES, :]
        cari_ref[:, cs] = ei[SUBLANES - 1:SUBLANES, :]

        def fix(j, carry, cs=cs, ar=ar, ai=ai):
            fr, fi = carry
            fr, fi = ar * fr - ai * fi, ar * fi + ai * fr
            rows = pl.ds(pl.multiple_of(j * SUBLANES, SUBLANES), SUBLANES)
            xr_ref[rows, cs] = xr_ref[rows, cs] + fr
            xi_ref[rows, cs] = xi_ref[rows, cs] + fi
            return fr, fi

        lax.fori_loop(0, seg_len, fix, (inr, ini), unroll=SCAN_UNROLL)

    for blk in range(nb_out):
        ks = slice(blk * kout, (blk + 1) * kout)
        ys = slice(blk * nout, (blk + 1) * nout)
        y = jnp.dot(xr_ref[:, ks].astype(BF16), cr_ref[blk], preferred_element_type=F32)
        y = y + jnp.dot(xi_ref[:, ks].astype(BF16), ci_ref[blk], preferred_element_type=F32)
        y = y + d_ref[:, ys] * u_ref[:, ys]
        z_ref[:, ys] = jax.nn.gelu(y).astype(BF16)

    zn = jnp.dot(_perm_matrix(ts, seg_len, True), z_ref[...], preferred_element_type=F32).astype(BF16)
    vg = jnp.dot(zn, wglu_ref[...], preferred_element_type=F32)
    o_ref[...] = o_ref[...] + vg[:, :d_model] * jax.nn.sigmoid(vg[:, d_model:])

    @pl.when(tile == last)
    def _():
        _combine_wait(eo_ref, buf_ref, sem, 1 - slot)


def _s5_layer(pending, norm_g, w_in, lam_re, lam_im, log_dt, b_re, b_im, c_re, c_im, d_skip, w_glu):
    x, slab, dest, eo = pending
    t, d = x.shape
    groups, p, k = b_re.shape
    d_ssm = groups * k
    ns = groups * p
    ts = SCAN_TILE
    seg_len = ts // SUBLANES

    lane = lambda v: v.reshape(1, ns).astype(F32)
    rows_k = lambda b: jnp.transpose(b, (2, 0, 1)).reshape(k, ns).astype(F32)
    abr, abi, bbr, bbi = pl.pallas_call(
        _s5_disc_kernel,
        in_specs=[_const_spec((1, ns))] * 3 + [_const_spec((k, ns))] * 2,
        out_specs=[pl.BlockSpec((1, ns), lambda i: (0, 0))] * 2 + [pl.BlockSpec((k, ns), lambda i: (0, 0))] * 2,
        out_shape=[jax.ShapeDtypeStruct((1, ns), F32)] * 2 + [jax.ShapeDtypeStruct((k, ns), F32)] * 2,
        grid=(1,),
        compiler_params=_cparams(("arbitrary",)),
        name="s5_disc",
    )(lane(lam_re), lane(lam_im), lane(jnp.repeat(log_dt, p)), rows_k(b_re), rows_k(b_im))

    gb = 256 // k
    nb = groups // gb
    eye = jnp.eye(gb, dtype=F32)

    def in_blocks(bb):
        m = jnp.einsum('kcgp,gh->cgkhp', bb.reshape(k, nb, gb, p), eye)
        return m.reshape(nb, gb * k, gb * p).astype(BF16)

    def out_blocks(cc):
        m = jnp.einsum('cgkp,gh->cgphk', cc.reshape(nb, gb, k, p).astype(F32), eye)
        return m.reshape(nb, gb * p, gb * k).astype(BF16)

    chunk = 512
    return pl.pallas_call(
        functools.partial(_s5_kernel, chunk=chunk),
        grid_spec=pltpu.PrefetchScalarGridSpec(
            num_scalar_prefetch=1, grid=(t // ts,),
            in_specs=[pl.BlockSpec((ts, d), lambda i, dst: (i, 0)),
                      pl.BlockSpec((ts, ROUTE_LANES), lambda i, dst: (i, 0)),
                      pl.BlockSpec(memory_space=pl.ANY),
                      _const_spec((1, d)), _const_spec((d, d_ssm)),
                      _const_spec((nb, gb * k, gb * p)), _const_spec((nb, gb * k, gb * p)),
                      _const_spec((nb, gb * p, gb * k)), _const_spec((nb, gb * p, gb * k)),
                      _const_spec((1, d_ssm)), _const_spec((1, ns)), _const_spec((1, ns)),
                      _const_spec((d_ssm, 2 * d))],
            out_specs=pl.BlockSpec((ts, d), lambda i, dst: (i, 0)),
            scratch_shapes=[pltpu.VMEM((ts, d_ssm), F32), pltpu.VMEM((ts, d_ssm), BF16),
                            pltpu.VMEM((ts, ns), F32), pltpu.VMEM((ts, ns), F32),
                            pltpu.VMEM((SUBLANES, ns), F32), pltpu.VMEM((SUBLANES, ns), F32),
                            pltpu.VMEM((SUBLANES, ns), F32), pltpu.VMEM((SUBLANES, ns), F32),
                            pltpu.VMEM((1, ns), F32), pltpu.VMEM((1, ns), F32),
                            pltpu.VMEM((2, TOP_K, ts, d), F32), pltpu.SemaphoreType.DMA((2, TOP_K))]),
        out_shape=jax.ShapeDtypeStruct((t, d), F32),
        compiler_params=_cparams(("arbitrary",)),
        name="s5_mix",
    )(dest, x, slab, eo, norm_g.reshape(1, d), w_in.astype(BF16), in_blocks(bbr), in_blocks(bbi),
      out_blocks(c_re), out_blocks(-c_im), d_skip.reshape(1, d_ssm).astype(F32),
      abr, abi, w_glu.astype(BF16))


def _route_kernel(x_ref, g_ref, wr_ref, br_ref, slab_ref, cnt_ref, base_ref, *, n_groups, epg):
    @pl.when(pl.program_id(0) == 0)
    def _():
        base_ref[...] = jnp.zeros_like(base_ref)

    tr = x_ref.shape[0]
    xn = _rms(x_ref[...], g_ref[...])
    x_hi = xn.astype(BF16)
    x_lo = (xn - x_hi.astype(F32)).astype(BF16)
    hi = jnp.dot(x_hi, wr_ref[...], preferred_element_type=F32)
    lo = jnp.dot(x_lo, wr_ref[:, :ROUTE_LANES], preferred_element_type=F32)
    logits = hi[:, :ROUTE_LANES] + (hi[:, ROUTE_LANES:] + lo) + br_ref[...]
    lane = lax.broadcasted_iota(I32, logits.shape, 1)
    neg = jnp.float32(-3.0e38)

    def first_argmax(v):
        m = jnp.max(v, axis=1, keepdims=True)
        return m, jnp.min(jnp.where(v == m, lane, ROUTE_LANES), axis=1, keepdims=True)

    gmask = lane < n_groups
    gmax, gidx = first_argmax(jnp.where(gmask, logits, neg))
    p_group = 1.0 / jnp.sum(jnp.where(gmask, jnp.exp(logits - gmax), 0.0), axis=1, keepdims=True)
    lo = n_groups + gidx * epg
    el = jnp.where((lane >= lo) & (lane < lo + epg), logits, neg)
    m1, i1 = first_argmax(el)
    m2, i2 = first_argmax(jnp.where(lane == i1, neg, el))
    t2 = jnp.exp(m2 - m1)
    w1 = p_group / (1.0 + t2)
    w2 = p_group * t2 / (1.0 + t2)

    hit1, hit2 = lane == i1, lane == i2
    onehot = (hit1 | hit2).astype(BF16)
    r = lax.broadcasted_iota(I32, (tr, tr), 0)
    c = lax.broadcasted_iota(I32, (tr, tr), 1)
    before = jnp.dot((c < r).astype(BF16), onehot, preferred_element_type=F32) + base_ref[0:1, :]
    rank1 = jnp.sum(jnp.where(hit1, before, 0.0), axis=1, keepdims=True)
    rank2 = jnp.sum(jnp.where(hit2, before, 0.0), axis=1, keepdims=True)
    total = base_ref[0:1, :] + jnp.sum(onehot.astype(F32), axis=0, keepdims=True)
    base_ref[...] = jnp.broadcast_to(total, base_ref.shape)
    cnt_ref[...] = jnp.broadcast_to(total, cnt_ref.shape)

    vals = (w1, w2, (i1 - n_groups).astype(F32), (i2 - n_groups).astype(F32), rank1, rank2)
    slab = jnp.zeros(logits.shape, F32)
    for k, v in enumerate(vals):
        slab = jnp.where(lane == k, v, slab)
    slab_ref[...] = slab


def _row_gather_start(src_ref, dst_ref, sem, index_of, rows):
    for r in range(rows):
        pltpu.make_async_copy(src_ref.at[pl.ds(index_of(r), 1)], dst_ref.at[pl.ds(r, 1)], sem).start()


def _row_gather_wait(src_ref, dst_ref, sem):
    pltpu.make_async_copy(src_ref.at[pl.ds(0, dst_ref.shape[0])], dst_ref, sem).wait()


def _gmm_kernel(blk_ref, exp_ref, lo_ref, hi_ref, nexp_ref, wslot_ref, dest_ref, x_ref, g_ref,
                w1_ref, w3_ref, w2_ref, o_ref, xbuf_ref, w1f_ref, w3f_ref, w2f_ref,
                w1b_ref, w3b_ref, w2b_ref, tok_ref, sem, wsem, *, layer):
    n = pl.program_id(0)
    last = pl.num_programs(0) - 1
    prev = jnp.maximum(n - 1, 0)
    nslots = xbuf_ref.shape[0]
    slot = lax.rem(n, nslots)
    rows = xbuf_ref.shape[1]
    new_expert = (n == 0) | (exp_ref[n] != exp_ref[prev])
    new_block = (n == 0) | (blk_ref[n] != blk_ref[prev])

    def gather(item, into):
        base = blk_ref[item] * rows
        _row_gather_start(x_ref, xbuf_ref.at[into], sem.at[into], lambda r: tok_ref[base + r], rows)

    def weight_copies(expert, into):
        return [pltpu.make_async_copy(src.at[layer, expert], dst.at[into], wsem.at[into, j])
                for j, (src, dst) in enumerate(((w1_ref, w1f_ref), (w3_ref, w3f_ref), (w2_ref, w2f_ref)))]

    @pl.when(n == 0)
    def _():
        def invert(t, carry):
            for k in range(TOP_K):
                tok_ref[dest_ref[t * TOP_K + k]] = t
            return carry

        for cp in weight_copies(exp_ref[0], wslot_ref[0]):
            cp.start()
        lax.fori_loop(0, tok_ref.shape[0] // TOP_K, invert, 0, unroll=8)
        for ahead in range(GMM_AHEAD):
            gather(jnp.minimum(ahead, last), ahead)

    @pl.when(new_expert)
    def _():
        ws = wslot_ref[n]
        for cp in weight_copies(exp_ref[n], ws):
            cp.wait()
        w1b_ref[...] = w1f_ref[ws].astype(BF16)
        w3b_ref[...] = w3f_ref[ws].astype(BF16)
        w2b_ref[...] = w2f_ref[ws].astype(BF16)

        @pl.when(nexp_ref[n] != exp_ref[n])
        def _():
            for cp in weight_copies(nexp_ref[n], 1 - ws):
                cp.start()

    _row_gather_wait(x_ref, xbuf_ref.at[slot], sem.at[slot])
    xb = _rms(xbuf_ref[slot], g_ref[...]).astype(BF16)
    gather(jnp.minimum(n + GMM_AHEAD, last), lax.rem(n + GMM_AHEAD, nslots))
    h = jax.nn.silu(jnp.dot(xb, w1b_ref[...], preferred_element_type=F32))
    h = (h * jnp.dot(xb, w3b_ref[...], preferred_element_type=F32)).astype(BF16)
    out = jnp.dot(h, w2b_ref[...], preferred_element_type=F32)
    row = lax.broadcasted_iota(I32, (out.shape[0], 1), 0)
    out = jnp.where((row >= lo_ref[n]) & (row < hi_ref[n]), out, 0.0)

    @pl.when(new_block)
    def _():
        o_ref[...] = out

    @pl.when(jnp.logical_not(new_block))
    def _():
        o_ref[...] = o_ref[...] + out

    @pl.when(n == last)
    def _():
        for ahead in range(1, GMM_AHEAD + 1):
            other = lax.rem(n + ahead, nslots)
            _row_gather_wait(x_ref, xbuf_ref.at[other], sem.at[other])


def _combine_start(dest_ref, eo_ref, buf_ref, sem, tile, into):
    rows = buf_ref.shape[2]
    base = tile * (rows * TOP_K)
    for k in range(TOP_K):
        _row_gather_start(eo_ref, buf_ref.at[into, k], sem.at[into, k],
                          lambda r, k=k: dest_ref[base + r * TOP_K + k], rows)


def _combine_wait(eo_ref, buf_ref, sem, slot):
    for k in range(TOP_K):
        _row_gather_wait(eo_ref, buf_ref.at[slot, k], sem.at[slot, k])


def _combined_tile(x, slab, buf_ref, slot):
    return x + slab[:, 0:1] * buf_ref[slot, 0] + slab[:, 1:2] * buf_ref[slot, 1]


def _combine_kernel(dest_ref, x_ref, slab_ref, g_ref, eo_ref, o_ref, buf_ref, sem):
    i = pl.program_id(0)
    last = pl.num_programs(0) - 1
    nslots = buf_ref.shape[0]
    slot = lax.rem(i, nslots)

    @pl.when(i == 0)
    def _():
        for ahead in range(COMBINE_AHEAD):
            _combine_start(dest_ref, eo_ref, buf_ref, sem, jnp.minimum(ahead, last), ahead)

    _combine_wait(eo_ref, buf_ref, sem, slot)
    y = _combined_tile(x_ref[...], slab_ref[...], buf_ref, slot)
    _combine_start(dest_ref, eo_ref, buf_ref, sem, jnp.minimum(i + COMBINE_AHEAD, last),
                   lax.rem(i + COMBINE_AHEAD, nslots))
    o_ref[...] = _rms(y, g_ref[...])

    @pl.when(i == last)
    def _():
        for ahead in range(1, COMBINE_AHEAD + 1):
            _combine_wait(eo_ref, buf_ref, sem, lax.rem(i + ahead, nslots))


def _gmm_schedule(counts, n_rows, block):
    n_exp = counts.shape[0]
    n_blocks = n_rows // block
    n_items = n_blocks + n_exp - 1
    ends = jnp.cumsum(counts)
    starts = ends - counts
    first_blk = starts // block
    last_blk = jnp.maximum(ends - 1, 0) // block
    per_exp = jnp.where(counts > 0, last_blk - first_blk + 1, 0)
    item_end = jnp.cumsum(per_exp)
    item_start = item_end - per_exp
    n = jnp.arange(n_items, dtype=I32)
    valid = n < item_end[-1]
    e = jnp.minimum(jnp.sum(item_end[None, :] <= n[:, None], axis=1), n_exp - 1).astype(I32)
    blk = first_blk[e] + (n - item_start[e])
    lo = jnp.maximum(starts[e], blk * block) - blk * block
    hi = jnp.minimum(ends[e], (blk + 1) * block) - blk * block
    last = jnp.maximum(item_end[-1] - 1, 0)
    blk = jnp.where(valid, blk, blk[last]).astype(I32)
    e = jnp.where(valid, e, e[last]).astype(I32)
    lo = jnp.where(valid, lo, 0).astype(I32)
    hi = jnp.where(valid, hi, 0).astype(I32)
    ids = jnp.arange(n_exp, dtype=I32)
    later = jnp.where((per_exp > 0)[None, :] & (ids[None, :] > e[:, None]), ids[None, :], n_exp)
    nxt = jnp.min(later, axis=1)
    nxt = jnp.where(nxt == n_exp, e, nxt).astype(I32)
    wslot = (jnp.sum(((per_exp > 0)[None, :] & (ids[None, :] < e[:, None])).astype(I32), axis=1) & 1)
    return starts, (blk, e, lo, hi, nxt, wslot.astype(I32))


def _moe_layer(x, norm_g, wg, bg, we, be, w1, w3, w2, layer):
    t, d = x.shape
    n_groups, _, epg = we.shape
    n_exp = n_groups * epg
    f = w1.shape[-1]
    tr = min(ROUTER_TILE, t)
    n_rows = t * TOP_K
    g_row = norm_g.reshape(1, d)

    wr = jnp.concatenate([wg, jnp.transpose(we, (1, 0, 2)).reshape(d, n_exp)], axis=1)
    wr = jnp.pad(wr, ((0, 0), (0, ROUTE_LANES - wr.shape[1]))).astype(F32)
    wr_hi = wr.astype(BF16)
    wr = jnp.concatenate([wr_hi, (wr - wr_hi.astype(F32)).astype(BF16)], axis=1)
    br = jnp.pad(jnp.concatenate([bg, be.reshape(-1)]), (0, ROUTE_LANES - n_groups - n_exp))

    slab, cnt = pl.pallas_call(
        functools.partial(_route_kernel, n_groups=n_groups, epg=epg),
        grid=(t // tr,),
        in_specs=[pl.BlockSpec((tr, d), lambda i: (i, 0)), _const_spec((1, d)),
                  _const_spec((d, 2 * ROUTE_LANES)), _const_spec((1, ROUTE_LANES))],
        out_specs=[pl.BlockSpec((tr, ROUTE_LANES), lambda i: (i, 0)),
                   pl.BlockSpec((SUBLANES, ROUTE_LANES), lambda i: (0, 0))],
        out_shape=[jax.ShapeDtypeStruct((t, ROUTE_LANES), F32),
                   jax.ShapeDtypeStruct((SUBLANES, ROUTE_LANES), F32)],
        scratch_shapes=[pltpu.VMEM((SUBLANES, ROUTE_LANES), F32)],
        compiler_params=_cparams(("arbitrary",)),
        name="moe_route",
    )(x, g_row, wr, br.reshape(1, ROUTE_LANES).astype(F32))

    counts = cnt[0, n_groups:n_groups + n_exp].astype(I32)
    starts, items = _gmm_schedule(counts, n_rows, GMM_BLOCK)
    e_idx = slab[:, 2:2 + TOP_K].astype(I32)
    start_of = jnp.sum(jnp.where(e_idx[..., None] == jnp.arange(n_exp, dtype=I32), starts, 0), axis=-1)
    dest = (start_of + slab[:, 4:4 + TOP_K].astype(I32)).reshape(-1)

    n_items = items[0].shape[0]
    any_spec = pl.BlockSpec(memory_space=pl.ANY)
    eo = pl.pallas_call(
        functools.partial(_gmm_kernel, layer=layer),
        grid_spec=pltpu.PrefetchScalarGridSpec(
            num_scalar_prefetch=len(items) + 1, grid=(n_items,),
            in_specs=[any_spec, pl.BlockSpec((1, d), lambda n, *_: (0, 0)),
                      any_spec, any_spec, any_spec],
            out_specs=pl.BlockSpec((GMM_BLOCK, d), lambda n, blk, *_: (blk[n], 0)),
            scratch_shapes=[pltpu.VMEM((GMM_AHEAD + 1, GMM_BLOCK, d), F32),
                            pltpu.VMEM((2, d, f), F32), pltpu.VMEM((2, d, f), F32),
                            pltpu.VMEM((2, f, d), F32),
                            pltpu.VMEM((d, f), BF16), pltpu.VMEM((d, f), BF16), pltpu.VMEM((f, d), BF16),
                            pltpu.SMEM((n_rows,), I32),
                            pltpu.SemaphoreType.DMA((GMM_AHEAD + 1,)),
                            pltpu.SemaphoreType.DMA((2, 3))]),
        out_shape=jax.ShapeDtypeStruct((n_rows, d), F32),
        compiler_params=_cparams(("arbitrary",)),
        name="moe_gmm",
    )(*items, dest, x, g_row, w1, w3, w2)
    return x, slab, dest, eo


def _moe_combine_norm(pending, final_g):
    x, slab, dest, eo = pending
    t, d = x.shape
    tr = ROUTE_TILE
    return pl.pallas_call(
        _combine_kernel,
        grid_spec=pltpu.PrefetchScalarGridSpec(
            num_scalar_prefetch=1, grid=(t // tr,),
            in_specs=[pl.BlockSpec((tr, d), lambda i, dst: (i, 0)),
                      pl.BlockSpec((tr, ROUTE_LANES), lambda i, dst: (i, 0)),
                      pl.BlockSpec((1, d), lambda i, dst: (0, 0)),
                      pl.BlockSpec(memory_space=pl.ANY)],
            out_specs=pl.BlockSpec((tr, d), lambda i, dst: (i, 0)),
            scratch_shapes=[pltpu.VMEM((COMBINE_AHEAD + 1, TOP_K, tr, d), F32),
                            pltpu.SemaphoreType.DMA((COMBINE_AHEAD + 1, TOP_K))]),
        out_shape=jax.ShapeDtypeStruct((t, d), F32),
        compiler_params=_cparams(("arbitrary",)),
        name="moe_combine",
    )(dest, x, slab, final_g.reshape(1, d), eo)


def kernel(x, mix_norm, ffn_norm, final_norm, a_w_in, a_conv_w, a_conv_b, a_gate_a_w, a_gate_a_b, a_gate_x_w, a_gate_x_b, a_lru_L, a_w_out, b_w_in, b_lam_re, b_lam_im, b_log_dt, b_b_re, b_b_im, b_c_re, b_c_im, b_d, b_w_glu, moe_wg, moe_bg, moe_we, moe_be, moe_w1, moe_w3, moe_w2):
    batch, seq, d = x.shape
    depth = mix_norm.shape[0]
    assert depth == 2, "the combine of each MoE layer is fused into the S5 mixer / final norm"
    outs = []
    for b in range(batch):
        h = _rglru_layer(x[b], mix_norm[0], a_w_in[0], a_conv_w[0], a_conv_b[0], a_gate_a_w[0],
                         a_gate_a_b[0], a_gate_x_w[0], a_gate_x_b[0], a_lru_L[0], a_w_out[0])
        pending = _moe_layer(h, ffn_norm[0], moe_wg[0], moe_bg[0], moe_we[0], moe_be[0],
                             moe_w1, moe_w3, moe_w2, 0)
        h = _s5_layer(pending, mix_norm[1], b_w_in[0], b_lam_re[0], b_lam_im[0], b_log_dt[0],
                      b_b_re[0], b_b_im[0], b_c_re[0], b_c_im[0], b_d[0], b_w_glu[0])
        pending = _moe_layer(h, ffn_norm[1], moe_wg[1], moe_bg[1], moe_we[1], moe_be[1],
                             moe_w1, moe_w3, moe_w2, 1)
        outs.append(_moe_combine_norm(pending, final_norm))
    return jnp.stack(outs)
```

```python
import functools
import math

import jax
import jax.numpy as jnp
from jax import lax
from jax.experimental import pallas as pl
from jax.experimental.pallas import tpu as pltpu

F32 = jnp.float32
BF16 = jnp.bfloat16
I32 = jnp.int32

EPS = 1e-6
LRU_C = 8.0
CONV_WIDTH = 4
TOP_K = 2

LANES = 128
SUBLANES = 8
VMEM_LIMIT = 56 * 1024 * 1024

SCAN_TILE = 256
IN_TILE_M = 1024
IN_TILE_N = 1024
ROUTER_TILE = 512
ROUTE_TILE = 256
GMM_BLOCK = 256
GMM_AHEAD = 2
COMBINE_AHEAD = 2
SCAN_UNROLL = True
ROUTE_LANES = 128


def _cparams(sem):
    return pltpu.CompilerParams(dimension_semantics=sem, vmem_limit_bytes=VMEM_LIMIT)


def _const_spec(shape):
    nd = len(shape)
    return pl.BlockSpec(shape, lambda *_: (0,) * nd, pipeline_mode=pl.Buffered(1))


def _rms(x, g):
    ms = jnp.mean(x * x, axis=-1, keepdims=True)
    return x * lax.rsqrt(ms + EPS) * g


def _perm_matrix(ts, seg_len, inverse):
    r = lax.broadcasted_iota(I32, (ts, ts), 1 if inverse else 0)
    n = lax.broadcasted_iota(I32, (ts, ts), 0 if inverse else 1)
    return (n == (r & (SUBLANES - 1)) * seg_len + (r >> 3)).astype(BF16)


def _sublane_id(shape):
    return lax.broadcasted_iota(I32, shape, 0) & (SUBLANES - 1)


def _lru_in_kernel(x_ref, g_ref, w_ref, o_ref, xp_ref, *, ts):
    @pl.when(pl.program_id(1) == 0)
    def _():
        p = _perm_matrix(ts, ts // SUBLANES, False)
        for s in range(x_ref.shape[0] // ts):
            rows = slice(s * ts, (s + 1) * ts)
            xn = _rms(x_ref[rows, :], g_ref[...]).astype(BF16)
            xp_ref[rows, :] = jnp.dot(p, xn, preferred_element_type=F32).astype(BF16)

    o_ref[...] = jnp.dot(xp_ref[...], w_ref[...], preferred_element_type=F32)


def _lru_rec_kernel(gate_ref, rec_ref, x_ref, cw_ref, cb_ref, wg_ref, ba_ref, bx_ref, lam_ref,
                    wout_ref, o_ref, ext_ref, a_ref, b_ref, tail_ref, hcar_ref, *, chunk):
    ts, c = rec_ref.shape
    seg_len = ts // SUBLANES
    halo = (CONV_WIDTH - 1) * SUBLANES
    nblk, bw, _ = wg_ref.shape

    @pl.when(pl.program_id(0) == 0)
    def _():
        tail_ref[...] = jnp.zeros_like(tail_ref)
        hcar_ref[...] = jnp.zeros_like(hcar_ref)

    tail = rec_ref[ts - halo:ts, :]
    sub = _sublane_id((halo, c))
    ext_ref[0:halo, :] = jnp.where(sub == 0, pltpu.roll(tail_ref[...], halo - (SUBLANES - 1), 0),
                                   pltpu.roll(tail, 1, 0))
    ext_ref[halo:halo + ts, :] = rec_ref[...]
    tail_ref[...] = tail

    z = -lam_ref[...]
    softplus = jnp.maximum(z, 0.0) + jnp.log1p(jnp.exp(-jnp.abs(z)))

    for blk in range(nblk):
        cs = slice(blk * bw, (blk + 1) * bw)
        conv = cb_ref[:, cs] + cw_ref[0:1, cs] * ext_ref[0:ts, cs]
        for k in range(1, CONV_WIDTH):
            conv = conv + cw_ref[k:k + 1, cs] * ext_ref[k * SUBLANES:k * SUBLANES + ts, cs]
        g = jnp.dot(conv.astype(BF16), wg_ref[blk], preferred_element_type=F32)
        r = jax.nn.sigmoid(g[:, :bw] + ba_ref[:, cs])
        ig = jax.nn.sigmoid(g[:, bw:] + bx_ref[:, cs])
        log_a = (-LRU_C * r) * softplus[:, cs]
        a_ref[:, cs] = jnp.exp(log_a)
        th = jnp.tanh(log_a)
        b_ref[:, cs] = jnp.sqrt(-2.0 * th / (1.0 - th)) * (ig * conv)

    sub8 = _sublane_id((SUBLANES, chunk))
    for ch in range(c // chunk):
        cs = slice(ch * chunk, (ch + 1) * chunk)

        def step(j, carry, cs=cs):
            h, pp = carry
            rows = pl.ds(pl.multiple_of(j * SUBLANES, SUBLANES), SUBLANES)
            a = a_ref[rows, cs]
            h = a * h + b_ref[rows, cs]
            pp = a * pp
            b_ref[rows, cs] = h
            a_ref[rows, cs] = pp
            return h, pp

        hc, pc = lax.fori_loop(0, seg_len, step,
                               (jnp.zeros((SUBLANES, chunk), F32), jnp.ones((SUBLANES, chunk), F32)),
                               unroll=SCAN_UNROLL)
        for d in (1, 2, 4):
            keep = sub8 >= d
            hs = jnp.where(keep, pltpu.roll(hc, d, 0), 0.0)
            ps = jnp.where(keep, pltpu.roll(pc, d, 0), 1.0)
            hc = pc * hs + hc
            pc = pc * ps
        cin = hcar_ref[:, cs]
        end = hc + pc * cin
        seg_in = jnp.where(sub8 >= 1, pltpu.roll(end, 1, 0), cin)
        hcar_ref[:, cs] = end[SUBLANES - 1:SUBLANES, :]

        def fix(j, carry, cs=cs, seg_in=seg_in):
            rows = pl.ds(pl.multiple_of(j * SUBLANES, SUBLANES), SUBLANES)
            b_ref[rows, cs] = b_ref[rows, cs] + a_ref[rows, cs] * seg_in
            return carry

        lax.fori_loop(0, seg_len, fix, 0, unroll=SCAN_UNROLL)

    y = (b_ref[...] * jax.nn.gelu(gate_ref[...])).astype(BF16)
    y = jnp.dot(_perm_matrix(ts, seg_len, True), y, preferred_element_type=F32).astype(BF16)
    o_ref[...] = x_ref[...] + jnp.dot(y, wout_ref[...], preferred_element_type=F32)


def _block_diag(w, per_block):
    h, d, _ = w.shape
    nb = h // per_block
    eye = jnp.eye(per_block, dtype=w.dtype)
    m = jnp.einsum('bhij,hk->bhikj', w.reshape(nb, per_block, d, d), eye)
    return m.reshape(nb, per_block * d, per_block * d)


def _rglru_layer(x, norm_g, w_in, conv_w, conv_b, wa, ba, wx, bx, lru_l, w_out):
    t, d = x.shape
    c = w_out.shape[0]
    heads, hd, _ = wa.shape
    per_block = LANES // math.gcd(hd, LANES)
    ts = SCAN_TILE
    tm = min(IN_TILE_M, t)
    tn = IN_TILE_N

    proj = pl.pallas_call(
        functools.partial(_lru_in_kernel, ts=ts),
        grid=(t // tm, (2 * c) // tn),
        in_specs=[pl.BlockSpec((tm, d), lambda i, j: (i, 0)),
                  pl.BlockSpec((1, d), lambda i, j: (0, 0)),
                  pl.BlockSpec((d, tn), lambda i, j: (0, j))],
        out_specs=pl.BlockSpec((tm, tn), lambda i, j: (i, j)),
        out_shape=jax.ShapeDtypeStruct((t, 2 * c), F32),
        scratch_shapes=[pltpu.VMEM((tm, d), BF16)],
        compiler_params=_cparams(("parallel", "arbitrary")),
        name="lru_in",
    )(x, norm_g.reshape(1, d), w_in.astype(BF16))

    wg = jnp.concatenate([_block_diag(wa, per_block), _block_diag(wx, per_block)], axis=-1).astype(BF16)
    nblk, bw, _ = wg.shape
    halo = (CONV_WIDTH - 1) * SUBLANES
    row = lambda v: v.reshape(1, c)
    return pl.pallas_call(
        functools.partial(_lru_rec_kernel, chunk=bw * 2),
        grid=(t // ts,),
        in_specs=[pl.BlockSpec((ts, c), lambda i: (i, 0)),
                  pl.BlockSpec((ts, c), lambda i: (i, 1)),
                  pl.BlockSpec((ts, d), lambda i: (i, 0)),
                  _const_spec((CONV_WIDTH, c)), _const_spec((1, c)),
                  _const_spec((nblk, bw, 2 * bw)),
                  _const_spec((1, c)), _const_spec((1, c)), _const_spec((1, c)),
                  _const_spec((c, d))],
        out_specs=pl.BlockSpec((ts, d), lambda i: (i, 0)),
        out_shape=jax.ShapeDtypeStruct((t, d), F32),
        scratch_shapes=[pltpu.VMEM((ts + halo, c), F32), pltpu.VMEM((ts, c), F32),
                        pltpu.VMEM((ts, c), F32), pltpu.VMEM((halo, c), F32),
                        pltpu.VMEM((1, c), F32)],
        compiler_params=_cparams(("arbitrary",)),
        name="lru_rec",
    )(proj, proj, x, conv_w, row(conv_b), wg, row(ba), row(bx), row(lru_l), w_out.astype(BF16))


def _s5_disc_kernel(lre_ref, lim_ref, dt_ref, bre_ref, bim_ref, abr_ref, abi_ref, bb_ref):
    lre = jnp.minimum(lre_ref[...], -1e-4)
    lim = lim_ref[...]
    dt = jnp.exp(dt_ref[...])
    mag = jnp.exp(lre * dt)
    abr = mag * jnp.cos(lim * dt)
    abi = mag * jnp.sin(lim * dt)
    den = lre * lre + lim * lim
    kr = ((abr - 1.0) * lre + abi * lim) / den
    ki = (abi * lre - (abr - 1.0) * lim) / den
    abr_ref[...] = abr
    abi_ref[...] = abi
    bb_ref[0] = kr * bre_ref[...] - ki * bim_ref[...]
    bb_ref[1] = kr * bim_ref[...] + ki * bre_ref[...]


def _s5_kernel(dest_ref, x_ref, slab_ref, eo_ref, g_ref, win_ref, bin_ref, cout_ref,
               d_ref, abr_ref, abi_ref, wglu_ref, o_ref, u_ref, z_ref, xr_ref, xi_ref, a8r_ref,
               a8i_ref, asr_ref, asi_ref, carr_ref, cari_ref, buf_ref, sem, *, chunk):
    ts, d_model = x_ref.shape
    tile = pl.program_id(0)
    last = pl.num_programs(0) - 1
    slot = tile & 1
    seg_len = ts // SUBLANES
    ns = xr_ref.shape[1]
    _, nb_in, kin, nin = bin_ref.shape
    _, nb_out, kout, nout = cout_ref.shape

    @pl.when(tile == 0)
    def _():
        _combine_start(dest_ref, eo_ref, buf_ref, sem, 0, 0)
        carr_ref[...] = jnp.zeros_like(carr_ref)
        cari_ref[...] = jnp.zeros_like(cari_ref)
        ar, ai = abr_ref[...], abi_ref[...]
        a8r_ref[...] = jnp.broadcast_to(ar, a8r_ref.shape)
        a8i_ref[...] = jnp.broadcast_to(ai, a8i_ref.shape)

        def power(j, cur):
            pr, pi = cur
            return pr * ar - pi * ai, pr * ai + pi * ar

        lr, li = lax.fori_loop(0, seg_len - 1, power, (ar, ai))

        def seg_powers(s, cur):
            pr, pi = cur
            asr_ref[pl.ds(s, 1), :] = pr
            asi_ref[pl.ds(s, 1), :] = pi
            return pr * lr - pi * li, pr * li + pi * lr

        lax.fori_loop(0, SUBLANES, seg_powers, (lr, li))

    _combine_wait(eo_ref, buf_ref, sem, slot)
    x = _combined_tile(x_ref[...], slab_ref[...], buf_ref, slot)
    o_ref[...] = x
    _combine_start(dest_ref, eo_ref, buf_ref, sem, jnp.minimum(tile + 1, last), 1 - slot)
    xn = _rms(x, g_ref[...]).astype(BF16)
    xp = jnp.dot(_perm_matrix(ts, seg_len, False), xn, preferred_element_type=F32).astype(BF16)
    u = jnp.dot(xp, win_ref[...], preferred_element_type=F32)
    u_ref[...] = u
    ub = u.astype(BF16)
    for blk in range(nb_in):
        uc = ub[:, blk * kin:(blk + 1) * kin]
        xr_ref[:, blk * nin:(blk + 1) * nin] = jnp.dot(uc, bin_ref[0, blk], preferred_element_type=F32)
        xi_ref[:, blk * nin:(blk + 1) * nin] = jnp.dot(uc, bin_ref[1, blk], preferred_element_type=F32)

    sub8 = _sublane_id((SUBLANES, chunk))
    for ch in range(ns // chunk):
        cs = slice(ch * chunk, (ch + 1) * chunk)
        ar, ai = a8r_ref[:, cs], a8i_ref[:, cs]

        def step(j, carry, cs=cs, ar=ar, ai=ai):
            sr, si = carry
            rows = pl.ds(pl.multiple_of(j * SUBLANES, SUBLANES), SUBLANES)
            nr = ar * sr - ai * si + xr_ref[rows, cs]
            ni = ar * si + ai * sr + xi_ref[rows, cs]
            xr_ref[rows, cs] = nr
            xi_ref[rows, cs] = ni
            return nr, ni

        zero = jnp.zeros((SUBLANES, chunk), F32)
        er, ei = lax.fori_loop(0, seg_len, step, (zero, zero), unroll=SCAN_UNROLL)
        for d in (1, 2, 4):
            keep = sub8 >= d
            pr, pi = asr_ref[d - 1:d, cs], asi_ref[d - 1:d, cs]
            sr = jnp.where(keep, pltpu.roll(er, d, 0), 0.0)
            si = jnp.where(keep, pltpu.roll(ei, d, 0), 0.0)
            er, ei = er + pr * sr - pi * si, ei + pr * si + pi * sr
        cr, ci = carr_ref[:, cs], cari_ref[:, cs]
        pr, pi = asr_ref[:, cs], asi_ref[:, cs]
        er, ei = er + pr * cr - pi * ci, ei + pr * ci + pi * cr
        inr = jnp.where(sub8 >= 1, pltpu.roll(er, 1, 0), cr)
        ini = jnp.where(sub8 >= 1, pltpu.roll(ei, 1, 0), ci)
        carr_ref[:, cs] = er[SUBLANES - 1:SUBLANES, :]
        cari_ref[:, cs] = ei[SUBLANES - 1:SUBLANES, :]

        def fix(j, carry, cs=cs, ar=ar, ai=ai):
            fr, fi = carry
            fr, fi = ar * fr - ai * fi, ar * fi + ai * fr
            rows = pl.ds(pl.multiple_of(j * SUBLANES, SUBLANES), SUBLANES)
            xr_ref[rows, cs] = xr_ref[rows, cs] + fr
            xi_ref[rows, cs] = xi_ref[rows, cs] + fi
            return fr, fi

        lax.fori_loop(0, seg_len, fix, (inr, ini), unroll=SCAN_UNROLL)

    for blk in range(nb_out):
        ks = slice(blk * kout, (blk + 1) * kout)
        ys = slice(blk * nout, (blk + 1) * nout)
        y = jnp.dot(xr_ref[:, ks].astype(BF16), cout_ref[0, blk], preferred_element_type=F32)
        y = y + jnp.dot(xi_ref[:, ks].astype(BF16), cout_ref[1, blk], preferred_element_type=F32)
        y = y + d_ref[:, ys] * u_ref[:, ys]
        z_ref[:, ys] = jax.nn.gelu(y).astype(BF16)

    zn = jnp.dot(_perm_matrix(ts, seg_len, True), z_ref[...], preferred_element_type=F32).astype(BF16)
    vg = jnp.dot(zn, wglu_ref[...], preferred_element_type=F32)
    o_ref[...] = o_ref[...] + vg[:, :d_model] * jax.nn.sigmoid(vg[:, d_model:])

    @pl.when(tile == last)
    def _():
        _combine_wait(eo_ref, buf_ref, sem, 1 - slot)


def _s5_layer(pending, norm_g, w_in, lam_re, lam_im, log_dt, b_re, b_im, c_re, c_im, d_skip, w_glu):
    x, slab, dest, eo = pending
    t, d = x.shape
    groups, p, k = b_re.shape
    d_ssm = groups * k
    ns = groups * p
    ts = SCAN_TILE
    seg_len = ts // SUBLANES

    lane = lambda v: v.reshape(1, ns).astype(F32)
    rows_k = lambda b: jnp.transpose(b, (2, 0, 1)).reshape(k, ns).astype(F32)
    abr, abi, bb = pl.pallas_call(
        _s5_disc_kernel,
        in_specs=[_const_spec((1, ns))] * 3 + [_const_spec((k, ns))] * 2,
        out_specs=[pl.BlockSpec((1, ns), lambda i: (0, 0))] * 2 + [pl.BlockSpec((2, k, ns), lambda i: (0, 0, 0))],
        out_shape=[jax.ShapeDtypeStruct((1, ns), F32)] * 2 + [jax.ShapeDtypeStruct((2, k, ns), F32)],
        grid=(1,),
        compiler_params=_cparams(("arbitrary",)),
        name="s5_disc",
    )(lane(lam_re), lane(lam_im), lane(jnp.repeat(log_dt, p)), rows_k(b_re), rows_k(b_im))

    gb = 256 // k
    nb = groups // gb
    eye = jnp.eye(gb, dtype=F32)

    def in_blocks(b2):
        m = jnp.einsum('rkcgp,gh->rcgkhp', b2.reshape(2, k, nb, gb, p), eye)
        return m.reshape(2, nb, gb * k, gb * p).astype(BF16)

    def out_blocks(c2):
        m = jnp.einsum('rcgkp,gh->rcgphk', c2.reshape(2, nb, gb, k, p).astype(F32), eye)
        return m.reshape(2, nb, gb * p, gb * k).astype(BF16)

    chunk = 512
    return pl.pallas_call(
        functools.partial(_s5_kernel, chunk=chunk),
        grid_spec=pltpu.PrefetchScalarGridSpec(
            num_scalar_prefetch=1, grid=(t // ts,),
            in_specs=[pl.BlockSpec((ts, d), lambda i, dst: (i, 0)),
                      pl.BlockSpec((ts, ROUTE_LANES), lambda i, dst: (i, 0)),
                      pl.BlockSpec(memory_space=pl.ANY),
                      _const_spec((1, d)), _const_spec((d, d_ssm)),
                      _const_spec((2, nb, gb * k, gb * p)), _const_spec((2, nb, gb * p, gb * k)),
                      _const_spec((1, d_ssm)), _const_spec((1, ns)), _const_spec((1, ns)),
                      _const_spec((d_ssm, 2 * d))],
            out_specs=pl.BlockSpec((ts, d), lambda i, dst: (i, 0)),
            scratch_shapes=[pltpu.VMEM((ts, d_ssm), F32), pltpu.VMEM((ts, d_ssm), BF16),
                            pltpu.VMEM((ts, ns), F32), pltpu.VMEM((ts, ns), F32),
                            pltpu.VMEM((SUBLANES, ns), F32), pltpu.VMEM((SUBLANES, ns), F32),
                            pltpu.VMEM((SUBLANES, ns), F32), pltpu.VMEM((SUBLANES, ns), F32),
                            pltpu.VMEM((1, ns), F32), pltpu.VMEM((1, ns), F32),
                            pltpu.VMEM((2, TOP_K, ts, d), F32), pltpu.SemaphoreType.DMA((2, TOP_K))]),
        out_shape=jax.ShapeDtypeStruct((t, d), F32),
        compiler_params=_cparams(("arbitrary",)),
        name="s5_mix",
    )(dest, x, slab, eo, norm_g.reshape(1, d), w_in.astype(BF16), in_blocks(bb),
      out_blocks(jnp.stack([c_re, -c_im])), d_skip.reshape(1, d_ssm).astype(F32),
      abr, abi, w_glu.astype(BF16))


def _route_kernel(x_ref, g_ref, wr_ref, br_ref, slab_ref, er_ref, cnt_ref, base_ref, *, n_groups, epg):
    @pl.when(pl.program_id(0) == 0)
    def _():
        base_ref[...] = jnp.zeros_like(base_ref)

    tr = x_ref.shape[0]
    xn = _rms(x_ref[...], g_ref[...])
    x_hi = xn.astype(BF16)
    x_lo = (xn - x_hi.astype(F32)).astype(BF16)
    hi = jnp.dot(x_hi, wr_ref[...], preferred_element_type=F32)
    lo = jnp.dot(x_lo, wr_ref[:, :ROUTE_LANES], preferred_element_type=F32)
    logits = hi[:, :ROUTE_LANES] + (hi[:, ROUTE_LANES:] + lo) + br_ref[...]
    lane = lax.broadcasted_iota(I32, logits.shape, 1)
    neg = jnp.float32(-3.0e38)

    def first_argmax(v):
        m = jnp.max(v, axis=1, keepdims=True)
        return m, jnp.min(jnp.where(v == m, lane, ROUTE_LANES), axis=1, keepdims=True)

    gmask = lane < n_groups
    gmax, gidx = first_argmax(jnp.where(gmask, logits, neg))
    p_group = 1.0 / jnp.sum(jnp.where(gmask, jnp.exp(logits - gmax), 0.0), axis=1, keepdims=True)
    lo = n_groups + gidx * epg
    el = jnp.where((lane >= lo) & (lane < lo + epg), logits, neg)
    m1, i1 = first_argmax(el)
    m2, i2 = first_argmax(jnp.where(lane == i1, neg, el))
    t2 = jnp.exp(m2 - m1)
    w1 = p_group / (1.0 + t2)
    w2 = p_group * t2 / (1.0 + t2)

    hit1, hit2 = lane == i1, lane == i2
    onehot = (hit1 | hit2).astype(BF16)
    r = lax.broadcasted_iota(I32, (tr, tr), 0)
    c = lax.broadcasted_iota(I32, (tr, tr), 1)
    before = jnp.dot((c < r).astype(BF16), onehot, preferred_element_type=F32) + base_ref[0:1, :]
    rank1 = jnp.sum(jnp.where(hit1, before, 0.0), axis=1, keepdims=True)
    rank2 = jnp.sum(jnp.where(hit2, before, 0.0), axis=1, keepdims=True)
    total = base_ref[0:1, :] + jnp.sum(onehot.astype(F32), axis=0, keepdims=True)
    base_ref[...] = jnp.broadcast_to(total, base_ref.shape)
    cnt_ref[...] = jnp.broadcast_to(total, cnt_ref.shape)

    vals = (w1, w2, (i1 - n_groups).astype(F32), (i2 - n_groups).astype(F32), rank1, rank2)
    slab = jnp.zeros(logits.shape, F32)
    for k, v in enumerate(vals):
        slab = jnp.where(lane == k, v, slab)
    slab_ref[...] = slab
    er_ref[...] = jnp.transpose(slab)[0:SUBLANES, :]


def _row_gather_start(src_ref, dst_ref, sem, index_of, rows):
    for r in range(rows):
        pltpu.make_async_copy(src_ref.at[pl.ds(index_of(r), 1)], dst_ref.at[pl.ds(r, 1)], sem).start()


def _row_gather_wait(src_ref, dst_ref, sem):
    pltpu.make_async_copy(src_ref.at[pl.ds(0, dst_ref.shape[0])], dst_ref, sem).wait()


def _gmm_kernel(blk_ref, exp_ref, lo_ref, hi_ref, nexp_ref, wslot_ref, dest_ref, x_ref, g_ref,
                w1_ref, w3_ref, w2_ref, o_ref, xbuf_ref, w1f_ref, w3f_ref, w2f_ref,
                w1b_ref, w3b_ref, w2b_ref, tok_ref, sem, wsem, *, layer):
    n = pl.program_id(0)
    last = pl.num_programs(0) - 1
    prev = jnp.maximum(n - 1, 0)
    nslots = xbuf_ref.shape[0]
    slot = lax.rem(n, nslots)
    rows = xbuf_ref.shape[1]
    new_expert = (n == 0) | (exp_ref[n] != exp_ref[prev])
    new_block = (n == 0) | (blk_ref[n] != blk_ref[prev])

    def gather(item, into):
        base = blk_ref[item] * rows
        _row_gather_start(x_ref, xbuf_ref.at[into], sem.at[into], lambda r: tok_ref[base + r], rows)

    def weight_copies(expert, into):
        return [pltpu.make_async_copy(src.at[layer, expert], dst.at[into], wsem.at[into, j])
                for j, (src, dst) in enumerate(((w1_ref, w1f_ref), (w3_ref, w3f_ref), (w2_ref, w2f_ref)))]

    @pl.when(n == 0)
    def _():
        def invert(t, carry):
            for k in range(TOP_K):
                tok_ref[dest_ref[k * (tok_ref.shape[0] // TOP_K) + t]] = t
            return carry

        for cp in weight_copies(exp_ref[0], wslot_ref[0]):
            cp.start()
        lax.fori_loop(0, tok_ref.shape[0] // TOP_K, invert, 0, unroll=8)
        for ahead in range(GMM_AHEAD):
            gather(jnp.minimum(ahead, last), ahead)

    @pl.when(new_expert)
    def _():
        ws = wslot_ref[n]
        for cp in weight_copies(exp_ref[n], ws):
            cp.wait()
        w1b_ref[...] = w1f_ref[ws].astype(BF16)
        w3b_ref[...] = w3f_ref[ws].astype(BF16)
        w2b_ref[...] = w2f_ref[ws].astype(BF16)

        @pl.when(nexp_ref[n] != exp_ref[n])
        def _():
            for cp in weight_copies(nexp_ref[n], 1 - ws):
                cp.start()

    _row_gather_wait(x_ref, xbuf_ref.at[slot], sem.at[slot])
    xb = _rms(xbuf_ref[slot], g_ref[...]).astype(BF16)
    gather(jnp.minimum(n + GMM_AHEAD, last), lax.rem(n + GMM_AHEAD, nslots))
    h = jax.nn.silu(jnp.dot(xb, w1b_ref[...], preferred_element_type=F32))
    h = (h * jnp.dot(xb, w3b_ref[...], preferred_element_type=F32)).astype(BF16)
    out = jnp.dot(h, w2b_ref[...], preferred_element_type=F32)
    row = lax.broadcasted_iota(I32, (out.shape[0], 1), 0)
    out = jnp.where((row >= lo_ref[n]) & (row < hi_ref[n]), out, 0.0)

    @pl.when(new_block)
    def _():
        o_ref[...] = out

    @pl.when(jnp.logical_not(new_block))
    def _():
        o_ref[...] = o_ref[...] + out

    @pl.when(n == last)
    def _():
        for ahead in range(1, GMM_AHEAD + 1):
            other = lax.rem(n + ahead, nslots)
            _row_gather_wait(x_ref, xbuf_ref.at[other], sem.at[other])


def _combine_start(dest_ref, eo_ref, buf_ref, sem, tile, into):
    rows = buf_ref.shape[2]
    n_tok = dest_ref.shape[0] // TOP_K
    for k in range(TOP_K):
        base = k * n_tok + tile * rows
        _row_gather_start(eo_ref, buf_ref.at[into, k], sem.at[into, k],
                          lambda r, base=base: dest_ref[base + r], rows)


def _combine_wait(eo_ref, buf_ref, sem, slot):
    for k in range(TOP_K):
        _row_gather_wait(eo_ref, buf_ref.at[slot, k], sem.at[slot, k])


def _combined_tile(x, slab, buf_ref, slot):
    return x + slab[:, 0:1] * buf_ref[slot, 0] + slab[:, 1:2] * buf_ref[slot, 1]


def _combine_kernel(dest_ref, x_ref, slab_ref, g_ref, eo_ref, o_ref, buf_ref, sem):
    i = pl.program_id(0)
    last = pl.num_programs(0) - 1
    nslots = buf_ref.shape[0]
    slot = lax.rem(i, nslots)

    @pl.when(i == 0)
    def _():
        for ahead in range(COMBINE_AHEAD):
            _combine_start(dest_ref, eo_ref, buf_ref, sem, jnp.minimum(ahead, last), ahead)

    _combine_wait(eo_ref, buf_ref, sem, slot)
    y = _combined_tile(x_ref[...], slab_ref[...], buf_ref, slot)
    _combine_start(dest_ref, eo_ref, buf_ref, sem, jnp.minimum(i + COMBINE_AHEAD, last),
                   lax.rem(i + COMBINE_AHEAD, nslots))
    o_ref[...] = _rms(y, g_ref[...])

    @pl.when(i == last)
    def _():
        for ahead in range(1, COMBINE_AHEAD + 1):
            _combine_wait(eo_ref, buf_ref, sem, lax.rem(i + ahead, nslots))


def _gmm_schedule(counts, n_rows, block):
    n_exp = counts.shape[0]
    n_blocks = n_rows // block
    n_items = n_blocks + n_exp - 1
    ends = jnp.cumsum(counts)
    starts = ends - counts
    first_blk = starts // block
    last_blk = jnp.maximum(ends - 1, 0) // block
    per_exp = jnp.where(counts > 0, last_blk - first_blk + 1, 0)
    item_end = jnp.cumsum(per_exp)
    item_start = item_end - per_exp
    n = jnp.arange(n_items, dtype=I32)
    valid = n < item_end[-1]
    e = jnp.minimum(jnp.sum(item_end[None, :] <= n[:, None], axis=1), n_exp - 1).astype(I32)
    blk = first_blk[e] + (n - item_start[e])
    lo = jnp.maximum(starts[e], blk * block) - blk * block
    hi = jnp.minimum(ends[e], (blk + 1) * block) - blk * block
    last = jnp.maximum(item_end[-1] - 1, 0)
    blk = jnp.where(valid, blk, blk[last]).astype(I32)
    e = jnp.where(valid, e, e[last]).astype(I32)
    lo = jnp.where(valid, lo, 0).astype(I32)
    hi = jnp.where(valid, hi, 0).astype(I32)
    ids = jnp.arange(n_exp, dtype=I32)
    later = jnp.where((per_exp > 0)[None, :] & (ids[None, :] > e[:, None]), ids[None, :], n_exp)
    nxt = jnp.min(later, axis=1)
    nxt = jnp.where(nxt == n_exp, e, nxt).astype(I32)
    wslot = (jnp.sum(((per_exp > 0)[None, :] & (ids[None, :] < e[:, None])).astype(I32), axis=1) & 1)
    return starts, (blk, e, lo, hi, nxt, wslot.astype(I32))


def _moe_layer(x, norm_g, wg, bg, we, be, w1, w3, w2, layer):
    t, d = x.shape
    n_groups, _, epg = we.shape
    n_exp = n_groups * epg
    f = w1.shape[-1]
    tr = min(ROUTER_TILE, t)
    n_rows = t * TOP_K
    g_row = norm_g.reshape(1, d)

    wr = jnp.concatenate([wg, jnp.transpose(we, (1, 0, 2)).reshape(d, n_exp)], axis=1)
    wr = jnp.pad(wr, ((0, 0), (0, ROUTE_LANES - wr.shape[1]))).astype(F32)
    wr_hi = wr.astype(BF16)
    wr = jnp.concatenate([wr_hi, (wr - wr_hi.astype(F32)).astype(BF16)], axis=1)
    br = jnp.pad(jnp.concatenate([bg, be.reshape(-1)]), (0, ROUTE_LANES - n_groups - n_exp))

    slab, er, cnt = pl.pallas_call(
        functools.partial(_route_kernel, n_groups=n_groups, epg=epg),
        grid=(t // tr,),
        in_specs=[pl.BlockSpec((tr, d), lambda i: (i, 0)), _const_spec((1, d)),
                  _const_spec((d, 2 * ROUTE_LANES)), _const_spec((1, ROUTE_LANES))],
        out_specs=[pl.BlockSpec((tr, ROUTE_LANES), lambda i: (i, 0)),
                   pl.BlockSpec((SUBLANES, tr), lambda i: (0, i)),
                   pl.BlockSpec((SUBLANES, ROUTE_LANES), lambda i: (0, 0))],
        out_shape=[jax.ShapeDtypeStruct((t, ROUTE_LANES), F32),
                   jax.ShapeDtypeStruct((SUBLANES, t), F32),
                   jax.ShapeDtypeStruct((SUBLANES, ROUTE_LANES), F32)],
        scratch_shapes=[pltpu.VMEM((SUBLANES, ROUTE_LANES), F32)],
        compiler_params=_cparams(("arbitrary",)),
        name="moe_route",
    )(x, g_row, wr, br.reshape(1, ROUTE_LANES).astype(F32))

    counts = cnt[0, n_groups:n_groups + n_exp].astype(I32)
    starts, items = _gmm_schedule(counts, n_rows, GMM_BLOCK)
    e_idx = er[2:2 + TOP_K].astype(I32)
    start_of = jnp.sum(jnp.where(e_idx[..., None] == jnp.arange(n_exp, dtype=I32), starts, 0), axis=-1)
    dest = (start_of + er[2 + TOP_K:2 + 2 * TOP_K].astype(I32)).reshape(-1)

    n_items = items[0].shape[0]
    any_spec = pl.BlockSpec(memory_space=pl.ANY)
    eo = pl.pallas_call(
        functools.partial(_gmm_kernel, layer=layer),
        grid_spec=pltpu.PrefetchScalarGridSpec(
            num_scalar_prefetch=len(items) + 1, grid=(n_items,),
            in_specs=[any_spec, pl.BlockSpec((1, d), lambda n, *_: (0, 0)),
                      any_spec, any_spec, any_spec],
            out_specs=pl.BlockSpec((GMM_BLOCK, d), lambda n, blk, *_: (blk[n], 0)),
            scratch_shapes=[pltpu.VMEM((GMM_AHEAD + 1, GMM_BLOCK, d), F32),
                            pltpu.VMEM((2, d, f), F32), pltpu.VMEM((2, d, f), F32),
                            pltpu.VMEM((2, f, d), F32),
                            pltpu.VMEM((d, f), BF16), pltpu.VMEM((d, f), BF16), pltpu.VMEM((f, d), BF16),
                            pltpu.SMEM((n_rows,), I32),
                            pltpu.SemaphoreType.DMA((GMM_AHEAD + 1,)),
                            pltpu.SemaphoreType.DMA((2, 3))]),
        out_shape=jax.ShapeDtypeStruct((n_rows, d), F32),
        compiler_params=_cparams(("arbitrary",)),
        name="moe_gmm",
    )(*items, dest, x, g_row, w1, w3, w2)
    return x, slab, dest, eo


def _moe_combine_norm(pending, final_g):
    x, slab, dest, eo = pending
    t, d = x.shape
    tr = ROUTE_TILE
    return pl.pallas_call(
        _combine_kernel,
        grid_spec=pltpu.PrefetchScalarGridSpec(
            num_scalar_prefetch=1, grid=(t // tr,),
            in_specs=[pl.BlockSpec((tr, d), lambda i, dst: (i, 0)),
                      pl.BlockSpec((tr, ROUTE_LANES), lambda i, dst: (i, 0)),
                      pl.BlockSpec((1, d), lambda i, dst: (0, 0)),
                      pl.BlockSpec(memory_space=pl.ANY)],
            out_specs=pl.BlockSpec((tr, d), lambda i, dst: (i, 0)),
            scratch_shapes=[pltpu.VMEM((COMBINE_AHEAD + 1, TOP_K, tr, d), F32),
                            pltpu.SemaphoreType.DMA((COMBINE_AHEAD + 1, TOP_K))]),
        out_shape=jax.ShapeDtypeStruct((t, d), F32),
        compiler_params=_cparams(("arbitrary",)),
        name="moe_combine",
    )(dest, x, slab, final_g.reshape(1, d), eo)


def kernel(x, mix_norm, ffn_norm, final_norm, a_w_in, a_conv_w, a_conv_b, a_gate_a_w, a_gate_a_b, a_gate_x_w, a_gate_x_b, a_lru_L, a_w_out, b_w_in, b_lam_re, b_lam_im, b_log_dt, b_b_re, b_b_im, b_c_re, b_c_im, b_d, b_w_glu, moe_wg, moe_bg, moe_we, moe_be, moe_w1, moe_w3, moe_w2):
    batch, seq, d = x.shape
    depth = mix_norm.shape[0]
    assert depth == 2, "the combine of each MoE layer is fused into the S5 mixer / final norm"
    outs = []
    for b in range(batch):
        h = _rglru_layer(x[b], mix_norm[0], a_w_in[0], a_conv_w[0], a_conv_b[0], a_gate_a_w[0],
                         a_gate_a_b[0], a_gate_x_w[0], a_gate_x_b[0], a_lru_L[0], a_w_out[0])
        pending = _moe_layer(h, ffn_norm[0], moe_wg[0], moe_bg[0], moe_we[0], moe_be[0],
                             moe_w1, moe_w3, moe_w2, 0)
        h = _s5_layer(pending, mix_norm[1], b_w_in[0], b_lam_re[0], b_lam_im[0], b_log_dt[0],
                      b_b_re[0], b_b_im[0], b_c_re[0], b_c_im[0], b_d[0], b_w_glu[0])
        pending = _moe_layer(h, ffn_norm[1], moe_wg[1], moe_bg[1], moe_we[1], moe_be[1],
                             moe_w1, moe_w3, moe_w2, 1)
        outs.append(_moe_combine_norm(pending, final_norm))
    return jnp.stack(outs)
```

```python
import functools
import math

import jax
import jax.numpy as jnp
from jax import lax
from jax.experimental import pallas as pl
from jax.experimental.pallas import tpu as pltpu

F32 = jnp.float32
BF16 = jnp.bfloat16
I32 = jnp.int32

EPS = 1e-6
LRU_C = 8.0
CONV_WIDTH = 4
TOP_K = 2

LANES = 128
SUBLANES = 8
VMEM_LIMIT = 56 * 1024 * 1024

SCAN_TILE = 256
IN_TILE_M = 1024
IN_TILE_N = 1024
ROUTER_TILE = 512
ROUTE_TILE = 256
GMM_BLOCK = 256
GMM_AHEAD = 2
COMBINE_AHEAD = 2
SCAN_UNROLL = True
ROUTE_LANES = 128


def _cparams(sem):
    return pltpu.CompilerParams(dimension_semantics=sem, vmem_limit_bytes=VMEM_LIMIT)


def _const_spec(shape):
    nd = len(shape)
    return pl.BlockSpec(shape, lambda *_: (0,) * nd, pipeline_mode=pl.Buffered(1))


def _rms(x, g):
    ms = jnp.mean(x * x, axis=-1, keepdims=True)
    return x * lax.rsqrt(ms + EPS) * g


def _perm_matrix(ts, seg_len, inverse):
    r = lax.broadcasted_iota(I32, (ts, ts), 1 if inverse else 0)
    n = lax.broadcasted_iota(I32, (ts, ts), 0 if inverse else 1)
    return (n == (r & (SUBLANES - 1)) * seg_len + (r >> 3)).astype(BF16)


def _sublane_id(shape):
    return lax.broadcasted_iota(I32, shape, 0) & (SUBLANES - 1)


def _lru_in_kernel(x_ref, g_ref, w_ref, o_ref, xp_ref, *, ts):
    @pl.when(pl.program_id(1) == 0)
    def _():
        p = _perm_matrix(ts, ts // SUBLANES, False)
        for s in range(x_ref.shape[0] // ts):
            rows = slice(s * ts, (s + 1) * ts)
            xn = _rms(x_ref[rows, :], g_ref[...]).astype(BF16)
            xp_ref[rows, :] = jnp.dot(p, xn, preferred_element_type=F32).astype(BF16)

    o_ref[...] = jnp.dot(xp_ref[...], w_ref[...], preferred_element_type=F32)


def _lru_rec_kernel(gate_ref, rec_ref, x_ref, cw_ref, cb_ref, wg_ref, ba_ref, bx_ref, lam_ref,
                    wout_ref, o_ref, ext_ref, a_ref, b_ref, tail_ref, hcar_ref, *, chunk):
    ts, c = rec_ref.shape
    seg_len = ts // SUBLANES
    halo = (CONV_WIDTH - 1) * SUBLANES
    nblk, bw, _ = wg_ref.shape

    @pl.when(pl.program_id(0) == 0)
    def _():
        tail_ref[...] = jnp.zeros_like(tail_ref)
        hcar_ref[...] = jnp.zeros_like(hcar_ref)

    tail = rec_ref[ts - halo:ts, :]
    sub = _sublane_id((halo, c))
    ext_ref[0:halo, :] = jnp.where(sub == 0, pltpu.roll(tail_ref[...], halo - (SUBLANES - 1), 0),
                                   pltpu.roll(tail, 1, 0))
    ext_ref[halo:halo + ts, :] = rec_ref[...]
    tail_ref[...] = tail

    z = -lam_ref[...]
    softplus = jnp.maximum(z, 0.0) + jnp.log1p(jnp.exp(-jnp.abs(z)))

    for blk in range(nblk):
        cs = slice(blk * bw, (blk + 1) * bw)
        conv = cb_ref[:, cs] + cw_ref[0:1, cs] * ext_ref[0:ts, cs]
        for k in range(1, CONV_WIDTH):
            conv = conv + cw_ref[k:k + 1, cs] * ext_ref[k * SUBLANES:k * SUBLANES + ts, cs]
        g = jnp.dot(conv.astype(BF16), wg_ref[blk], preferred_element_type=F32)
        r = jax.nn.sigmoid(g[:, :bw] + ba_ref[:, cs])
        ig = jax.nn.sigmoid(g[:, bw:] + bx_ref[:, cs])
        log_a = (-LRU_C * r) * softplus[:, cs]
        a_ref[:, cs] = jnp.exp(log_a)
        th = jnp.tanh(log_a)
        b_ref[:, cs] = jnp.sqrt(-2.0 * th / (1.0 - th)) * (ig * conv)

    sub8 = _sublane_id((SUBLANES, chunk))
    for ch in range(c // chunk):
        cs = slice(ch * chunk, (ch + 1) * chunk)

        def step(j, carry, cs=cs):
            h, pp = carry
            rows = pl.ds(pl.multiple_of(j * SUBLANES, SUBLANES), SUBLANES)
            a = a_ref[rows, cs]
            h = a * h + b_ref[rows, cs]
            pp = a * pp
            b_ref[rows, cs] = h
            a_ref[rows, cs] = pp
            return h, pp

        hc, pc = lax.fori_loop(0, seg_len, step,
                               (jnp.zeros((SUBLANES, chunk), F32), jnp.ones((SUBLANES, chunk), F32)),
                               unroll=SCAN_UNROLL)
        for d in (1, 2, 4):
            keep = sub8 >= d
            hs = jnp.where(keep, pltpu.roll(hc, d, 0), 0.0)
            ps = jnp.where(keep, pltpu.roll(pc, d, 0), 1.0)
            hc = pc * hs + hc
            pc = pc * ps
        cin = hcar_ref[:, cs]
        end = hc + pc * cin
        seg_in = jnp.where(sub8 >= 1, pltpu.roll(end, 1, 0), cin)
        hcar_ref[:, cs] = end[SUBLANES - 1:SUBLANES, :]

        def fix(j, carry, cs=cs, seg_in=seg_in):
            rows = pl.ds(pl.multiple_of(j * SUBLANES, SUBLANES), SUBLANES)
            b_ref[rows, cs] = b_ref[rows, cs] + a_ref[rows, cs] * seg_in
            return carry

        lax.fori_loop(0, seg_len, fix, 0, unroll=SCAN_UNROLL)

    y = (b_ref[...] * jax.nn.gelu(gate_ref[...])).astype(BF16)
    y = jnp.dot(_perm_matrix(ts, seg_len, True), y, preferred_element_type=F32).astype(BF16)
    o_ref[...] = x_ref[...] + jnp.dot(y, wout_ref[...], preferred_element_type=F32)


def _block_diag(w, per_block):
    h, d, _ = w.shape
    nb = h // per_block
    eye = jnp.eye(per_block, dtype=w.dtype)
    m = jnp.einsum('bhij,hk->bhikj', w.reshape(nb, per_block, d, d), eye)
    return m.reshape(nb, per_block * d, per_block * d)


def _rglru_layer(x, norm_g, w_in, conv_w, conv_b, wa, ba, wx, bx, lru_l, w_out):
    t, d = x.shape
    c = w_out.shape[0]
    heads, hd, _ = wa.shape
    per_block = LANES // math.gcd(hd, LANES)
    ts = SCAN_TILE
    tm = min(IN_TILE_M, t)
    tn = IN_TILE_N

    proj = pl.pallas_call(
        functools.partial(_lru_in_kernel, ts=ts),
        grid=(t // tm, (2 * c) // tn),
        in_specs=[pl.BlockSpec((tm, d), lambda i, j: (i, 0)),
                  pl.BlockSpec((1, d), lambda i, j: (0, 0)),
                  pl.BlockSpec((d, tn), lambda i, j: (0, j))],
        out_specs=pl.BlockSpec((tm, tn), lambda i, j: (i, j)),
        out_shape=jax.ShapeDtypeStruct((t, 2 * c), F32),
        scratch_shapes=[pltpu.VMEM((tm, d), BF16)],
        compiler_params=_cparams(("parallel", "arbitrary")),
        name="lru_in",
    )(x, norm_g.reshape(1, d), w_in.astype(BF16))

    wg = jnp.concatenate([_block_diag(wa, per_block), _block_diag(wx, per_block)], axis=-1).astype(BF16)
    nblk, bw, _ = wg.shape
    halo = (CONV_WIDTH - 1) * SUBLANES
    row = lambda v: v.reshape(1, c)
    return pl.pallas_call(
        functools.partial(_lru_rec_kernel, chunk=bw * 2),
        grid=(t // ts,),
        in_specs=[pl.BlockSpec((ts, c), lambda i: (i, 0)),
                  pl.BlockSpec((ts, c), lambda i: (i, 1)),
                  pl.BlockSpec((ts, d), lambda i: (i, 0)),
                  _const_spec((CONV_WIDTH, c)), _const_spec((1, c)),
                  _const_spec((nblk, bw, 2 * bw)),
                  _const_spec((1, c)), _const_spec((1, c)), _const_spec((1, c)),
                  _const_spec((c, d))],
        out_specs=pl.BlockSpec((ts, d), lambda i: (i, 0)),
        out_shape=jax.ShapeDtypeStruct((t, d), F32),
        scratch_shapes=[pltpu.VMEM((ts + halo, c), F32), pltpu.VMEM((ts, c), F32),
                        pltpu.VMEM((ts, c), F32), pltpu.VMEM((halo, c), F32),
                        pltpu.VMEM((1, c), F32)],
        compiler_params=_cparams(("arbitrary",)),
        name="lru_rec",
    )(proj, proj, x, conv_w, row(conv_b), wg, row(ba), row(bx), row(lru_l), w_out.astype(BF16))


def _s5_disc_kernel(lre_ref, lim_ref, dt_ref, bre_ref, bim_ref, abr_ref, abi_ref, bb_ref):
    lre = jnp.minimum(lre_ref[...], -1e-4)
    lim = lim_ref[...]
    dt = jnp.exp(dt_ref[...])
    mag = jnp.exp(lre * dt)
    abr = mag * jnp.cos(lim * dt)
    abi = mag * jnp.sin(lim * dt)
    den = lre * lre + lim * lim
    kr = ((abr - 1.0) * lre + abi * lim) / den
    ki = (abi * lre - (abr - 1.0) * lim) / den
    abr_ref[...] = abr
    abi_ref[...] = abi
    bb_ref[0] = kr * bre_ref[...] - ki * bim_ref[...]
    bb_ref[1] = kr * bim_ref[...] + ki * bre_ref[...]


def _s5_kernel(dest_ref, x_ref, slab_ref, eo_ref, g_ref, win_ref, bin_ref, cout_ref,
               d_ref, abr_ref, abi_ref, wglu_ref, o_ref, u_ref, z_ref, xr_ref, xi_ref, a8r_ref,
               a8i_ref, asr_ref, asi_ref, carr_ref, cari_ref, buf_ref, sem, *, chunk):
    ts, d_model = x_ref.shape
    tile = pl.program_id(0)
    last = pl.num_programs(0) - 1
    slot = tile & 1
    seg_len = ts // SUBLANES
    ns = xr_ref.shape[1]
    _, nb_in, kin, nin = bin_ref.shape
    _, nb_out, kout, nout = cout_ref.shape

    @pl.when(tile == 0)
    def _():
        _combine_start(dest_ref, eo_ref, buf_ref, sem, 0, 0)
        carr_ref[...] = jnp.zeros_like(carr_ref)
        cari_ref[...] = jnp.zeros_like(cari_ref)
        ar, ai = abr_ref[...], abi_ref[...]
        a8r_ref[...] = jnp.broadcast_to(ar, a8r_ref.shape)
        a8i_ref[...] = jnp.broadcast_to(ai, a8i_ref.shape)

        def power(j, cur):
            pr, pi = cur
            return pr * ar - pi * ai, pr * ai + pi * ar

        lr, li = lax.fori_loop(0, seg_len - 1, power, (ar, ai))

        def seg_powers(s, cur):
            pr, pi = cur
            asr_ref[pl.ds(s, 1), :] = pr
            asi_ref[pl.ds(s, 1), :] = pi
            return pr * lr - pi * li, pr * li + pi * lr

        lax.fori_loop(0, SUBLANES, seg_powers, (lr, li))

    _combine_wait(eo_ref, buf_ref, sem, slot)
    x = _combined_tile(x_ref[...], slab_ref[...], buf_ref, slot)
    o_ref[...] = x
    _combine_start(dest_ref, eo_ref, buf_ref, sem, jnp.minimum(tile + 1, last), 1 - slot)
    xn = _rms(x, g_ref[...]).astype(BF16)
    xp = jnp.dot(_perm_matrix(ts, seg_len, False), xn, preferred_element_type=F32).astype(BF16)
    u = jnp.dot(xp, win_ref[...], preferred_element_type=F32)
    u_ref[...] = u
    ub = u.astype(BF16)
    for blk in range(nb_in):
        uc = ub[:, blk * kin:(blk + 1) * kin]
        xr_ref[:, blk * nin:(blk + 1) * nin] = jnp.dot(uc, bin_ref[0, blk], preferred_element_type=F32)
        xi_ref[:, blk * nin:(blk + 1) * nin] = jnp.dot(uc, bin_ref[1, blk], preferred_element_type=F32)

    sub8 = _sublane_id((SUBLANES, chunk))
    for ch in range(ns // chunk):
        cs = slice(ch * chunk, (ch + 1) * chunk)
        ar, ai = a8r_ref[:, cs], a8i_ref[:, cs]

        def step(j, carry, cs=cs, ar=ar, ai=ai):
            sr, si = carry
            rows = pl.ds(pl.multiple_of(j * SUBLANES, SUBLANES), SUBLANES)
            nr = ar * sr - ai * si + xr_ref[rows, cs]
            ni = ar * si + ai * sr + xi_ref[rows, cs]
            xr_ref[rows, cs] = nr
            xi_ref[rows, cs] = ni
            return nr, ni

        zero = jnp.zeros((SUBLANES, chunk), F32)
        er, ei = lax.fori_loop(0, seg_len, step, (zero, zero), unroll=SCAN_UNROLL)
        for d in (1, 2, 4):
            keep = sub8 >= d
            pr, pi = asr_ref[d - 1:d, cs], asi_ref[d - 1:d, cs]
            sr = jnp.where(keep, pltpu.roll(er, d, 0), 0.0)
            si = jnp.where(keep, pltpu.roll(ei, d, 0), 0.0)
            er, ei = er + pr * sr - pi * si, ei + pr * si + pi * sr
        cr, ci = carr_ref[:, cs], cari_ref[:, cs]
        pr, pi = asr_ref[:, cs], asi_ref[:, cs]
        er, ei = er + pr * cr - pi * ci, ei + pr * ci + pi * cr
        inr = jnp.where(sub8 >= 1, pltpu.roll(er, 1, 0), cr)
        ini = jnp.where(sub8 >= 1, pltpu.roll(ei, 1, 0), ci)
        carr_ref[:, cs] = er[SUBLANES - 1:SUBLANES, :]
        cari_ref[:, cs] = ei[SUBLANES - 1:SUBLANES, :]

        def fix(j, carry, cs=cs, ar=ar, ai=ai):
            fr, fi = carry
            fr, fi = ar * fr - ai * fi, ar * fi + ai * fr
            rows = pl.ds(pl.multiple_of(j * SUBLANES, SUBLANES), SUBLANES)
            xr_ref[rows, cs] = xr_ref[rows, cs] + fr
            xi_ref[rows, cs] = xi_ref[rows, cs] + fi
            return fr, fi

        lax.fori_loop(0, seg_len, fix, (inr, ini), unroll=SCAN_UNROLL)

    for blk in range(nb_out):
        ks = slice(blk * kout, (blk + 1) * kout)
        ys = slice(blk * nout, (blk + 1) * nout)
        y = jnp.dot(xr_ref[:, ks].astype(BF16), cout_ref[0, blk], preferred_element_type=F32)
        y = y + jnp.dot(xi_ref[:, ks].astype(BF16), cout_ref[1, blk], preferred_element_type=F32)
        y = y + d_ref[:, ys] * u_ref[:, ys]
        z_ref[:, ys] = jax.nn.gelu(y).astype(BF16)

    zn = jnp.dot(_perm_matrix(ts, seg_len, True), z_ref[...], preferred_element_type=F32).astype(BF16)
    vg = jnp.dot(zn, wglu_ref[...], preferred_element_type=F32)
    o_ref[...] = o_ref[...] + vg[:, :d_model] * jax.nn.sigmoid(vg[:, d_model:])

    @pl.when(tile == last)
    def _():
        _combine_wait(eo_ref, buf_ref, sem, 1 - slot)


def _s5_layer(pending, norm_g, w_in, lam_re, lam_im, log_dt, b_re, b_im, c_re, c_im, d_skip, w_glu):
    x, slab, dest, eo = pending
    t, d = x.shape
    groups, p, k = b_re.shape
    d_ssm = groups * k
    ns = groups * p
    ts = SCAN_TILE
    seg_len = ts // SUBLANES

    lane = lambda v: v.reshape(1, ns).astype(F32)
    rows_k = lambda b: jnp.transpose(b, (2, 0, 1)).reshape(k, ns).astype(F32)
    abr, abi, bb = pl.pallas_call(
        _s5_disc_kernel,
        in_specs=[_const_spec((1, ns))] * 3 + [_const_spec((k, ns))] * 2,
        out_specs=[pl.BlockSpec((1, ns), lambda i: (0, 0))] * 2 + [pl.BlockSpec((2, k, ns), lambda i: (0, 0, 0))],
        out_shape=[jax.ShapeDtypeStruct((1, ns), F32)] * 2 + [jax.ShapeDtypeStruct((2, k, ns), F32)],
        grid=(1,),
        compiler_params=_cparams(("arbitrary",)),
        name="s5_disc",
    )(lane(lam_re), lane(lam_im), lane(jnp.repeat(log_dt, p)), rows_k(b_re), rows_k(b_im))

    gb = 256 // k
    nb = groups // gb
    eye = jnp.eye(gb, dtype=F32)

    def in_blocks(b2):
        m = jnp.einsum('rkcgp,gh->rcgkhp', b2.reshape(2, k, nb, gb, p), eye)
        return m.reshape(2, nb, gb * k, gb * p).astype(BF16)

    def out_blocks(c2):
        m = jnp.einsum('rcgkp,gh->rcgphk', c2.reshape(2, nb, gb, k, p).astype(F32), eye)
        return m.reshape(2, nb, gb * p, gb * k).astype(BF16)

    chunk = 512
    return pl.pallas_call(
        functools.partial(_s5_kernel, chunk=chunk),
        grid_spec=pltpu.PrefetchScalarGridSpec(
            num_scalar_prefetch=1, grid=(t // ts,),
            in_specs=[pl.BlockSpec((ts, d), lambda i, dst: (i, 0)),
                      pl.BlockSpec((ts, ROUTE_LANES), lambda i, dst: (i, 0)),
                      pl.BlockSpec(memory_space=pl.ANY),
                      _const_spec((1, d)), _const_spec((d, d_ssm)),
                      _const_spec((2, nb, gb * k, gb * p)), _const_spec((2, nb, gb * p, gb * k)),
                      _const_spec((1, d_ssm)), _const_spec((1, ns)), _const_spec((1, ns)),
                      _const_spec((d_ssm, 2 * d))],
            out_specs=pl.BlockSpec((ts, d), lambda i, dst: (i, 0)),
            scratch_shapes=[pltpu.VMEM((ts, d_ssm), F32), pltpu.VMEM((ts, d_ssm), BF16),
                            pltpu.VMEM((ts, ns), F32), pltpu.VMEM((ts, ns), F32),
                            pltpu.VMEM((SUBLANES, ns), F32), pltpu.VMEM((SUBLANES, ns), F32),
                            pltpu.VMEM((SUBLANES, ns), F32), pltpu.VMEM((SUBLANES, ns), F32),
                            pltpu.VMEM((1, ns), F32), pltpu.VMEM((1, ns), F32),
                            pltpu.VMEM((2, TOP_K, ts, d), F32), pltpu.SemaphoreType.DMA((2, TOP_K))]),
        out_shape=jax.ShapeDtypeStruct((t, d), F32),
        compiler_params=_cparams(("arbitrary",)),
        name="s5_mix",
    )(dest, x, slab, eo, norm_g.reshape(1, d), w_in.astype(BF16), in_blocks(bb),
      out_blocks(jnp.stack([c_re, -c_im])), d_skip.reshape(1, d_ssm).astype(F32),
      abr, abi, w_glu.astype(BF16))


def _route_kernel(x_ref, g_ref, wr_ref, br_ref, slab_ref, er_ref, cnt_ref, base_ref, *, n_groups, epg):
    @pl.when(pl.program_id(0) == 0)
    def _():
        base_ref[...] = jnp.zeros_like(base_ref)

    tr = x_ref.shape[0]
    xn = _rms(x_ref[...], g_ref[...])
    x_hi = xn.astype(BF16)
    x_lo = (xn - x_hi.astype(F32)).astype(BF16)
    hi = jnp.dot(x_hi, wr_ref[...], preferred_element_type=F32)
    lo = jnp.dot(x_lo, wr_ref[:, :ROUTE_LANES], preferred_element_type=F32)
    logits = hi[:, :ROUTE_LANES] + (hi[:, ROUTE_LANES:] + lo) + br_ref[...]
    lane = lax.broadcasted_iota(I32, logits.shape, 1)
    neg = jnp.float32(-3.0e38)

    def first_argmax(v):
        m = jnp.max(v, axis=1, keepdims=True)
        return m, jnp.min(jnp.where(v == m, lane, ROUTE_LANES), axis=1, keepdims=True)

    gmask = lane < n_groups
    gmax, gidx = first_argmax(jnp.where(gmask, logits, neg))
    p_group = 1.0 / jnp.sum(jnp.where(gmask, jnp.exp(logits - gmax), 0.0), axis=1, keepdims=True)
    lo = n_groups + gidx * epg
    el = jnp.where((lane >= lo) & (lane < lo + epg), logits, neg)
    m1, i1 = first_argmax(el)
    m2, i2 = first_argmax(jnp.where(lane == i1, neg, el))
    t2 = jnp.exp(m2 - m1)
    w1 = p_group / (1.0 + t2)
    w2 = p_group * t2 / (1.0 + t2)

    hit1, hit2 = lane == i1, lane == i2
    onehot = (hit1 | hit2).astype(BF16)
    r = lax.broadcasted_iota(I32, (tr, tr), 0)
    c = lax.broadcasted_iota(I32, (tr, tr), 1)
    before = jnp.dot((c < r).astype(BF16), onehot, preferred_element_type=F32) + base_ref[0:1, :]
    rank1 = jnp.sum(jnp.where(hit1, before, 0.0), axis=1, keepdims=True)
    rank2 = jnp.sum(jnp.where(hit2, before, 0.0), axis=1, keepdims=True)
    total = base_ref[0:1, :] + jnp.sum(onehot.astype(F32), axis=0, keepdims=True)
    base_ref[...] = jnp.broadcast_to(total, base_ref.shape)
    cnt_ref[...] = jnp.broadcast_to(total, cnt_ref.shape)

    vals = (w1, w2, (i1 - n_groups).astype(F32), (i2 - n_groups).astype(F32), rank1, rank2)
    slab = jnp.zeros(logits.shape, F32)
    for k, v in enumerate(vals):
        slab = jnp.where(lane == k, v, slab)
    slab_ref[...] = slab
    er_ref[...] = jnp.transpose(slab)[0:SUBLANES, :]


def _row_gather_start(src_ref, dst_ref, sem, index_of, rows):
    for r in range(rows):
        pltpu.make_async_copy(src_ref.at[pl.ds(index_of(r), 1)], dst_ref.at[pl.ds(r, 1)], sem).start()


def _row_gather_wait(src_ref, dst_ref, sem):
    pltpu.make_async_copy(src_ref.at[pl.ds(0, dst_ref.shape[0])], dst_ref, sem).wait()


def _gmm_kernel(exp_ref, nexp_ref, wslot_ref, padlo_ref, padhi_ref, meta_ref, dest_ref, x_ref, g_ref,
                w1_ref, w3_ref, w2_ref, o_ref, xbuf_ref, w1f_ref, w3f_ref, w2f_ref,
                w1b_ref, w3b_ref, w2b_ref, tok_ref, sem, wsem, *, layer):
    n = pl.program_id(0)
    n_real = meta_ref[0]
    prev = jnp.maximum(n - 1, 0)
    nslots = xbuf_ref.shape[0]
    slot = lax.rem(n, nslots)
    rows = xbuf_ref.shape[1]
    new_expert = (n == 0) | (exp_ref[n] != exp_ref[prev])

    def gather(block, into):
        base = block * rows
        _row_gather_start(x_ref, xbuf_ref.at[into], sem.at[into], lambda r: tok_ref[base + r], rows)

    def weight_copies(expert, into):
        return [pltpu.make_async_copy(src.at[layer, expert], dst.at[into], wsem.at[into, j])
                for j, (src, dst) in enumerate(((w1_ref, w1f_ref), (w3_ref, w3f_ref), (w2_ref, w2f_ref)))]

    @pl.when(n == 0)
    def _():
        for cp in weight_copies(exp_ref[0], wslot_ref[0]):
            cp.start()

        def pad_expert(e, carry):
            def pad(j, c):
                tok_ref[j] = 0
                return c
            return lax.fori_loop(padlo_ref[e], padhi_ref[e], pad, carry)

        lax.fori_loop(0, padlo_ref.shape[0], pad_expert, 0)
        n_tok = dest_ref.shape[0] // TOP_K

        def invert(t, carry):
            for k in range(TOP_K):
                tok_ref[dest_ref[k * n_tok + t]] = t
            return carry

        lax.fori_loop(0, n_tok, invert, 0, unroll=8)
        for ahead in range(GMM_AHEAD):
            gather(ahead, ahead)

    @pl.when(new_expert)
    def _():
        ws = wslot_ref[n]
        for cp in weight_copies(exp_ref[n], ws):
            cp.wait()
        w1b_ref[...] = w1f_ref[ws].astype(BF16)
        w3b_ref[...] = w3f_ref[ws].astype(BF16)
        w2b_ref[...] = w2f_ref[ws].astype(BF16)

        @pl.when(nexp_ref[n] != exp_ref[n])
        def _():
            for cp in weight_copies(nexp_ref[n], 1 - ws):
                cp.start()

    @pl.when(n < n_real)
    def _():
        _row_gather_wait(x_ref, xbuf_ref.at[slot], sem.at[slot])
        xb = _rms(xbuf_ref[slot], g_ref[...]).astype(BF16)
        gather(jnp.minimum(n + GMM_AHEAD, n_real - 1), lax.rem(n + GMM_AHEAD, nslots))
        h = jax.nn.silu(jnp.dot(xb, w1b_ref[...], preferred_element_type=F32))
        h = (h * jnp.dot(xb, w3b_ref[...], preferred_element_type=F32)).astype(BF16)
        o_ref[...] = jnp.dot(h, w2b_ref[...], preferred_element_type=F32)

    @pl.when(n >= n_real)
    def _():
        o_ref[...] = jnp.zeros_like(o_ref)

    @pl.when(n == n_real - 1)
    def _():
        for ahead in range(1, GMM_AHEAD + 1):
            other = lax.rem(n + ahead, nslots)
            _row_gather_wait(x_ref, xbuf_ref.at[other], sem.at[other])


def _combine_start(dest_ref, eo_ref, buf_ref, sem, tile, into):
    rows = buf_ref.shape[2]
    n_tok = dest_ref.shape[0] // TOP_K
    for k in range(TOP_K):
        base = k * n_tok + tile * rows
        _row_gather_start(eo_ref, buf_ref.at[into, k], sem.at[into, k],
                          lambda r, base=base: dest_ref[base + r], rows)


def _combine_wait(eo_ref, buf_ref, sem, slot):
    for k in range(TOP_K):
        _row_gather_wait(eo_ref, buf_ref.at[slot, k], sem.at[slot, k])


def _combined_tile(x, slab, buf_ref, slot):
    return x + slab[:, 0:1] * buf_ref[slot, 0] + slab[:, 1:2] * buf_ref[slot, 1]


def _combine_kernel(dest_ref, x_ref, slab_ref, g_ref, eo_ref, o_ref, buf_ref, sem):
    i = pl.program_id(0)
    last = pl.num_programs(0) - 1
    nslots = buf_ref.shape[0]
    slot = lax.rem(i, nslots)

    @pl.when(i == 0)
    def _():
        for ahead in range(COMBINE_AHEAD):
            _combine_start(dest_ref, eo_ref, buf_ref, sem, jnp.minimum(ahead, last), ahead)

    _combine_wait(eo_ref, buf_ref, sem, slot)
    y = _combined_tile(x_ref[...], slab_ref[...], buf_ref, slot)
    _combine_start(dest_ref, eo_ref, buf_ref, sem, jnp.minimum(i + COMBINE_AHEAD, last),
                   lax.rem(i + COMBINE_AHEAD, nslots))
    o_ref[...] = _rms(y, g_ref[...])

    @pl.when(i == last)
    def _():
        for ahead in range(1, COMBINE_AHEAD + 1):
            _combine_wait(eo_ref, buf_ref, sem, lax.rem(i + ahead, nslots))


def _gmm_schedule(counts, n_rows, block):
    n_exp = counts.shape[0]
    n_items = n_rows // block + n_exp
    per_exp = (counts + block - 1) // block
    blk_end = jnp.cumsum(per_exp)
    starts = (blk_end - per_exp) * block
    n_real = blk_end[-1]
    n = jnp.arange(n_items, dtype=I32)
    ids = jnp.arange(n_exp, dtype=I32)
    last_used = jnp.max(jnp.where(per_exp > 0, ids, 0))
    e = jnp.minimum(jnp.sum(blk_end[None, :] <= n[:, None], axis=1), last_used).astype(I32)
    later = jnp.where((per_exp > 0)[None, :] & (ids[None, :] > e[:, None]), ids[None, :], n_exp)
    nxt = jnp.min(later, axis=1)
    nxt = jnp.where(nxt == n_exp, e, nxt).astype(I32)
    wslot = (jnp.sum(((per_exp > 0)[None, :] & (ids[None, :] < e[:, None])).astype(I32), axis=1) & 1)
    pad_lo = (starts + counts).astype(I32)
    pad_hi = (blk_end * block).astype(I32)
    meta = jnp.stack([n_real]).astype(I32)
    return starts, (e, nxt, wslot.astype(I32), pad_lo, pad_hi, meta)


def _moe_layer(x, norm_g, wg, bg, we, be, w1, w3, w2, layer):
    t, d = x.shape
    n_groups, _, epg = we.shape
    n_exp = n_groups * epg
    f = w1.shape[-1]
    tr = min(ROUTER_TILE, t)
    n_rows = t * TOP_K
    g_row = norm_g.reshape(1, d)

    wr = jnp.concatenate([wg, jnp.transpose(we, (1, 0, 2)).reshape(d, n_exp)], axis=1)
    wr = jnp.pad(wr, ((0, 0), (0, ROUTE_LANES - wr.shape[1]))).astype(F32)
    wr_hi = wr.astype(BF16)
    wr = jnp.concatenate([wr_hi, (wr - wr_hi.astype(F32)).astype(BF16)], axis=1)
    br = jnp.pad(jnp.concatenate([bg, be.reshape(-1)]), (0, ROUTE_LANES - n_groups - n_exp))

    slab, er, cnt = pl.pallas_call(
        functools.partial(_route_kernel, n_groups=n_groups, epg=epg),
        grid=(t // tr,),
        in_specs=[pl.BlockSpec((tr, d), lambda i: (i, 0)), _const_spec((1, d)),
                  _const_spec((d, 2 * ROUTE_LANES)), _const_spec((1, ROUTE_LANES))],
        out_specs=[pl.BlockSpec((tr, ROUTE_LANES), lambda i: (i, 0)),
                   pl.BlockSpec((SUBLANES, tr), lambda i: (0, i)),
                   pl.BlockSpec((SUBLANES, ROUTE_LANES), lambda i: (0, 0))],
        out_shape=[jax.ShapeDtypeStruct((t, ROUTE_LANES), F32),
                   jax.ShapeDtypeStruct((SUBLANES, t), F32),
                   jax.ShapeDtypeStruct((SUBLANES, ROUTE_LANES), F32)],
        scratch_shapes=[pltpu.VMEM((SUBLANES, ROUTE_LANES), F32)],
        compiler_params=_cparams(("arbitrary",)),
        name="moe_route",
    )(x, g_row, wr, br.reshape(1, ROUTE_LANES).astype(F32))

    counts = cnt[0, n_groups:n_groups + n_exp].astype(I32)
    starts, items = _gmm_schedule(counts, n_rows, GMM_BLOCK)
    e_idx = er[2:2 + TOP_K].astype(I32)
    start_of = jnp.sum(jnp.where(e_idx[..., None] == jnp.arange(n_exp, dtype=I32), starts, 0), axis=-1)
    dest = (start_of + er[2 + TOP_K:2 + 2 * TOP_K].astype(I32)).reshape(-1)

    n_items = items[0].shape[0]
    any_spec = pl.BlockSpec(memory_space=pl.ANY)
    eo = pl.pallas_call(
        functools.partial(_gmm_kernel, layer=layer),
        grid_spec=pltpu.PrefetchScalarGridSpec(
            num_scalar_prefetch=len(items) + 1, grid=(n_items,),
            in_specs=[any_spec, pl.BlockSpec((1, d), lambda n, *_: (0, 0)),
                      any_spec, any_spec, any_spec],
            out_specs=pl.BlockSpec((GMM_BLOCK, d), lambda n, *_: (n, 0)),
            scratch_shapes=[pltpu.VMEM((GMM_AHEAD + 1, GMM_BLOCK, d), F32),
                            pltpu.VMEM((2, d, f), F32), pltpu.VMEM((2, d, f), F32),
                            pltpu.VMEM((2, f, d), F32),
                            pltpu.VMEM((d, f), BF16), pltpu.VMEM((d, f), BF16), pltpu.VMEM((f, d), BF16),
                            pltpu.SMEM((n_items * GMM_BLOCK,), I32),
                            pltpu.SemaphoreType.DMA((GMM_AHEAD + 1,)),
                            pltpu.SemaphoreType.DMA((2, 3))]),
        out_shape=jax.ShapeDtypeStruct((n_items * GMM_BLOCK, d), F32),
        compiler_params=_cparams(("arbitrary",)),
        name="moe_gmm",
    )(*items, dest, x, g_row, w1, w3, w2)
    return x, slab, dest, eo


def _moe_combine_norm(pending, final_g):
    x, slab, dest, eo = pending
    t, d = x.shape
    tr = ROUTE_TILE
    return pl.pallas_call(
        _combine_kernel,
        grid_spec=pltpu.PrefetchScalarGridSpec(
            num_scalar_prefetch=1, grid=(t // tr,),
            in_specs=[pl.BlockSpec((tr, d), lambda i, dst: (i, 0)),
                      pl.BlockSpec((tr, ROUTE_LANES), lambda i, dst: (i, 0)),
                      pl.BlockSpec((1, d), lambda i, dst: (0, 0)),
                      pl.BlockSpec(memory_space=pl.ANY)],
            out_specs=pl.BlockSpec((tr, d), lambda i, dst: (i, 0)),
            scratch_shapes=[pltpu.VMEM((COMBINE_AHEAD + 1, TOP_K, tr, d), F32),
                            pltpu.SemaphoreType.DMA((COMBINE_AHEAD + 1, TOP_K))]),
        out_shape=jax.ShapeDtypeStruct((t, d), F32),
        compiler_params=_cparams(("arbitrary",)),
        name="moe_combine",
    )(dest, x, slab, final_g.reshape(1, d), eo)


def kernel(x, mix_norm, ffn_norm, final_norm, a_w_in, a_conv_w, a_conv_b, a_gate_a_w, a_gate_a_b, a_gate_x_w, a_gate_x_b, a_lru_L, a_w_out, b_w_in, b_lam_re, b_lam_im, b_log_dt, b_b_re, b_b_im, b_c_re, b_c_im, b_d, b_w_glu, moe_wg, moe_bg, moe_we, moe_be, moe_w1, moe_w3, moe_w2):
    batch, seq, d = x.shape
    depth = mix_norm.shape[0]
    assert depth == 2, "the combine of each MoE layer is fused into the S5 mixer / final norm"
    outs = []
    for b in range(batch):
        h = _rglru_layer(x[b], mix_norm[0], a_w_in[0], a_conv_w[0], a_conv_b[0], a_gate_a_w[0],
                         a_gate_a_b[0], a_gate_x_w[0], a_gate_x_b[0], a_lru_L[0], a_w_out[0])
        pending = _moe_layer(h, ffn_norm[0], moe_wg[0], moe_bg[0], moe_we[0], moe_be[0],
                             moe_w1, moe_w3, moe_w2, 0)
        h = _s5_layer(pending, mix_norm[1], b_w_in[0], b_lam_re[0], b_lam_im[0], b_log_dt[0],
                      b_b_re[0], b_b_im[0], b_c_re[0], b_c_im[0], b_d[0], b_w_glu[0])
        pending = _moe_layer(h, ffn_norm[1], moe_wg[1], moe_bg[1], moe_we[1], moe_be[1],
                             moe_w1, moe_w3, moe_w2, 1)
        outs.append(_moe_combine_norm(pending, final_norm))
    return jnp.stack(outs)
```

```python
import functools
import math

import jax
import jax.numpy as jnp
from jax import lax
from jax.experimental import pallas as pl
from jax.experimental.pallas import tpu as pltpu

F32 = jnp.float32
BF16 = jnp.bfloat16
I32 = jnp.int32

EPS = 1e-6
LRU_C = 8.0
CONV_WIDTH = 4
TOP_K = 2

LANES = 128
SUBLANES = 8
VMEM_LIMIT = 56 * 1024 * 1024

SCAN_TILE = 256
IN_TILE_M = 1024
IN_TILE_N = 1024
ROUTER_TILE = 512
ROUTE_TILE = 256
GMM_BLOCK = 256
GMM_AHEAD = 2
COMBINE_AHEAD = 2
SCAN_UNROLL = True
ROUTE_LANES = 128


def _cparams(sem):
    return pltpu.CompilerParams(dimension_semantics=sem, vmem_limit_bytes=VMEM_LIMIT)


def _const_spec(shape):
    nd = len(shape)
    return pl.BlockSpec(shape, lambda *_: (0,) * nd, pipeline_mode=pl.Buffered(1))


def _rms(x, g):
    ms = jnp.mean(x * x, axis=-1, keepdims=True)
    return x * lax.rsqrt(ms + EPS) * g


def _perm_matrix(ts, seg_len, inverse):
    r = lax.broadcasted_iota(I32, (ts, ts), 1 if inverse else 0)
    n = lax.broadcasted_iota(I32, (ts, ts), 0 if inverse else 1)
    return (n == (r & (SUBLANES - 1)) * seg_len + (r >> 3)).astype(BF16)


def _sublane_id(shape):
    return lax.broadcasted_iota(I32, shape, 0) & (SUBLANES - 1)


def _lru_in_kernel(x_ref, g_ref, w_ref, o_ref, xp_ref, *, ts):
    @pl.when(pl.program_id(1) == 0)
    def _():
        p = _perm_matrix(ts, ts // SUBLANES, False)
        for s in range(x_ref.shape[0] // ts):
            rows = slice(s * ts, (s + 1) * ts)
            xn = _rms(x_ref[rows, :], g_ref[...]).astype(BF16)
            xp_ref[rows, :] = jnp.dot(p, xn, preferred_element_type=F32).astype(BF16)

    o_ref[...] = jnp.dot(xp_ref[...], w_ref[...], preferred_element_type=F32)


def _lru_rec_kernel(gate_ref, rec_ref, x_ref, cw_ref, cb_ref, wg_ref, ba_ref, bx_ref, lam_ref,
                    wout_ref, o_ref, ext_ref, a_ref, b_ref, tail_ref, hcar_ref, *, chunk):
    ts, c = rec_ref.shape
    seg_len = ts // SUBLANES
    halo = (CONV_WIDTH - 1) * SUBLANES
    nblk, bw, _ = wg_ref.shape

    @pl.when(pl.program_id(0) == 0)
    def _():
        tail_ref[...] = jnp.zeros_like(tail_ref)
        hcar_ref[...] = jnp.zeros_like(hcar_ref)

    tail = rec_ref[ts - halo:ts, :]
    sub = _sublane_id((halo, c))
    ext_ref[0:halo, :] = jnp.where(sub == 0, pltpu.roll(tail_ref[...], halo - (SUBLANES - 1), 0),
                                   pltpu.roll(tail, 1, 0))
    ext_ref[halo:halo + ts, :] = rec_ref[...]
    tail_ref[...] = tail

    z = -lam_ref[...]
    softplus = jnp.maximum(z, 0.0) + jnp.log1p(jnp.exp(-jnp.abs(z)))

    for blk in range(nblk):
        cs = slice(blk * bw, (blk + 1) * bw)
        conv = cb_ref[:, cs] + cw_ref[0:1, cs] * ext_ref[0:ts, cs]
        for k in range(1, CONV_WIDTH):
            conv = conv + cw_ref[k:k + 1, cs] * ext_ref[k * SUBLANES:k * SUBLANES + ts, cs]
        g = jnp.dot(conv.astype(BF16), wg_ref[blk], preferred_element_type=F32)
        r = jax.nn.sigmoid(g[:, :bw] + ba_ref[:, cs])
        ig = jax.nn.sigmoid(g[:, bw:] + bx_ref[:, cs])
        log_a = (-LRU_C * r) * softplus[:, cs]
        a_ref[:, cs] = jnp.exp(log_a)
        th = jnp.tanh(log_a)
        b_ref[:, cs] = jnp.sqrt(-2.0 * th / (1.0 - th)) * (ig * conv)

    sub8 = _sublane_id((SUBLANES, chunk))
    for ch in range(c // chunk):
        cs = slice(ch * chunk, (ch + 1) * chunk)

        def step(j, carry, cs=cs):
            h, pp = carry
            rows = pl.ds(pl.multiple_of(j * SUBLANES, SUBLANES), SUBLANES)
            a = a_ref[rows, cs]
            h = a * h + b_ref[rows, cs]
            pp = a * pp
            b_ref[rows, cs] = h
            a_ref[rows, cs] = pp
            return h, pp

        hc, pc = lax.fori_loop(0, seg_len, step,
                               (jnp.zeros((SUBLANES, chunk), F32), jnp.ones((SUBLANES, chunk), F32)),
                               unroll=SCAN_UNROLL)
        for d in (1, 2, 4):
            keep = sub8 >= d
            hs = jnp.where(keep, pltpu.roll(hc, d, 0), 0.0)
            ps = jnp.where(keep, pltpu.roll(pc, d, 0), 1.0)
            hc = pc * hs + hc
            pc = pc * ps
        cin = hcar_ref[:, cs]
        end = hc + pc * cin
        seg_in = jnp.where(sub8 >= 1, pltpu.roll(end, 1, 0), cin)
        hcar_ref[:, cs] = end[SUBLANES - 1:SUBLANES, :]

        def fix(j, carry, cs=cs, seg_in=seg_in):
            rows = pl.ds(pl.multiple_of(j * SUBLANES, SUBLANES), SUBLANES)
            b_ref[rows, cs] = b_ref[rows, cs] + a_ref[rows, cs] * seg_in
            return carry

        lax.fori_loop(0, seg_len, fix, 0, unroll=SCAN_UNROLL)

    y = (b_ref[...] * jax.nn.gelu(gate_ref[...])).astype(BF16)
    y = jnp.dot(_perm_matrix(ts, seg_len, True), y, preferred_element_type=F32).astype(BF16)
    o_ref[...] = x_ref[...] + jnp.dot(y, wout_ref[...], preferred_element_type=F32)


def _block_diag(w, per_block):
    h, d, _ = w.shape
    nb = h // per_block
    eye = jnp.eye(per_block, dtype=w.dtype)
    m = jnp.einsum('bhij,hk->bhikj', w.reshape(nb, per_block, d, d), eye)
    return m.reshape(nb, per_block * d, per_block * d)


def _rglru_layer(x, norm_g, w_in, conv_w, conv_b, wa, ba, wx, bx, lru_l, w_out):
    t, d = x.shape
    c = w_out.shape[0]
    heads, hd, _ = wa.shape
    per_block = LANES // math.gcd(hd, LANES)
    ts = SCAN_TILE
    tm = min(IN_TILE_M, t)
    tn = IN_TILE_N

    proj = pl.pallas_call(
        functools.partial(_lru_in_kernel, ts=ts),
        grid=(t // tm, (2 * c) // tn),
        in_specs=[pl.BlockSpec((tm, d), lambda i, j: (i, 0)),
                  pl.BlockSpec((1, d), lambda i, j: (0, 0)),
                  pl.BlockSpec((d, tn), lambda i, j: (0, j))],
        out_specs=pl.BlockSpec((tm, tn), lambda i, j: (i, j)),
        out_shape=jax.ShapeDtypeStruct((t, 2 * c), F32),
        scratch_shapes=[pltpu.VMEM((tm, d), BF16)],
        compiler_params=_cparams(("parallel", "arbitrary")),
        name="lru_in",
    )(x, norm_g.reshape(1, d), w_in.astype(BF16))

    wg = jnp.concatenate([_block_diag(wa, per_block), _block_diag(wx, per_block)], axis=-1).astype(BF16)
    nblk, bw, _ = wg.shape
    halo = (CONV_WIDTH - 1) * SUBLANES
    row = lambda v: v.reshape(1, c)
    return pl.pallas_call(
        functools.partial(_lru_rec_kernel, chunk=bw * 2),
        grid=(t // ts,),
        in_specs=[pl.BlockSpec((ts, c), lambda i: (i, 0)),
                  pl.BlockSpec((ts, c), lambda i: (i, 1)),
                  pl.BlockSpec((ts, d), lambda i: (i, 0)),
                  _const_spec((CONV_WIDTH, c)), _const_spec((1, c)),
                  _const_spec((nblk, bw, 2 * bw)),
                  _const_spec((1, c)), _const_spec((1, c)), _const_spec((1, c)),
                  _const_spec((c, d))],
        out_specs=pl.BlockSpec((ts, d), lambda i: (i, 0)),
        out_shape=jax.ShapeDtypeStruct((t, d), F32),
        scratch_shapes=[pltpu.VMEM((ts + halo, c), F32), pltpu.VMEM((ts, c), F32),
                        pltpu.VMEM((ts, c), F32), pltpu.VMEM((halo, c), F32),
                        pltpu.VMEM((1, c), F32)],
        compiler_params=_cparams(("arbitrary",)),
        name="lru_rec",
    )(proj, proj, x, conv_w, row(conv_b), wg, row(ba), row(bx), row(lru_l), w_out.astype(BF16))


def _s5_disc_kernel(lre_ref, lim_ref, dt_ref, bre_ref, bim_ref, abr_ref, abi_ref, bb_ref):
    lre = jnp.minimum(lre_ref[...], -1e-4)
    lim = lim_ref[...]
    dt = jnp.exp(dt_ref[...])
    mag = jnp.exp(lre * dt)
    abr = mag * jnp.cos(lim * dt)
    abi = mag * jnp.sin(lim * dt)
    den = lre * lre + lim * lim
    kr = ((abr - 1.0) * lre + abi * lim) / den
    ki = (abi * lre - (abr - 1.0) * lim) / den
    abr_ref[...] = abr
    abi_ref[...] = abi
    bb_ref[0] = kr * bre_ref[...] - ki * bim_ref[...]
    bb_ref[1] = kr * bim_ref[...] + ki * bre_ref[...]


def _s5_kernel(dest_ref, x_ref, slab_ref, eo_ref, g_ref, win_ref, bin_ref, cout_ref,
               d_ref, abr_ref, abi_ref, wglu_ref, o_ref, u_ref, z_ref, xr_ref, xi_ref, a8r_ref,
               a8i_ref, asr_ref, asi_ref, carr_ref, cari_ref, buf_ref, sem, *, chunk):
    ts, d_model = x_ref.shape
    tile = pl.program_id(0)
    last = pl.num_programs(0) - 1
    slot = tile & 1
    seg_len = ts // SUBLANES
    ns = xr_ref.shape[1]
    _, nb_in, kin, nin = bin_ref.shape
    _, nb_out, kout, nout = cout_ref.shape

    @pl.when(tile == 0)
    def _():
        _combine_start(dest_ref, eo_ref, buf_ref, sem, 0, 0)
        carr_ref[...] = jnp.zeros_like(carr_ref)
        cari_ref[...] = jnp.zeros_like(cari_ref)
        ar, ai = abr_ref[...], abi_ref[...]
        a8r_ref[...] = jnp.broadcast_to(ar, a8r_ref.shape)
        a8i_ref[...] = jnp.broadcast_to(ai, a8i_ref.shape)

        def power(j, cur):
            pr, pi = cur
            return pr * ar - pi * ai, pr * ai + pi * ar

        lr, li = lax.fori_loop(0, seg_len - 1, power, (ar, ai))

        def seg_powers(s, cur):
            pr, pi = cur
            asr_ref[pl.ds(s, 1), :] = pr
            asi_ref[pl.ds(s, 1), :] = pi
            return pr * lr - pi * li, pr * li + pi * lr

        lax.fori_loop(0, SUBLANES, seg_powers, (lr, li))

    _combine_wait(eo_ref, buf_ref, sem, slot)
    x = _combined_tile(x_ref[...], slab_ref[...], buf_ref, slot)
    o_ref[...] = x
    _combine_start(dest_ref, eo_ref, buf_ref, sem, jnp.minimum(tile + 1, last), 1 - slot)
    xn = _rms(x, g_ref[...]).astype(BF16)
    xp = jnp.dot(_perm_matrix(ts, seg_len, False), xn, preferred_element_type=F32).astype(BF16)
    u = jnp.dot(xp, win_ref[...], preferred_element_type=F32)
    u_ref[...] = u
    ub = u.astype(BF16)
    for blk in range(nb_in):
        uc = ub[:, blk * kin:(blk + 1) * kin]
        xr_ref[:, blk * nin:(blk + 1) * nin] = jnp.dot(uc, bin_ref[0, blk], preferred_element_type=F32)
        xi_ref[:, blk * nin:(blk + 1) * nin] = jnp.dot(uc, bin_ref[1, blk], preferred_element_type=F32)

    sub8 = _sublane_id((SUBLANES, chunk))
    for ch in range(ns // chunk):
        cs = slice(ch * chunk, (ch + 1) * chunk)
        ar, ai = a8r_ref[:, cs], a8i_ref[:, cs]

        def step(j, carry, cs=cs, ar=ar, ai=ai):
            sr, si = carry
            rows = pl.ds(pl.multiple_of(j * SUBLANES, SUBLANES), SUBLANES)
            nr = ar * sr - ai * si + xr_ref[rows, cs]
            ni = ar * si + ai * sr + xi_ref[rows, cs]
            xr_ref[rows, cs] = nr
            xi_ref[rows, cs] = ni
            return nr, ni

        zero = jnp.zeros((SUBLANES, chunk), F32)
        er, ei = lax.fori_loop(0, seg_len, step, (zero, zero), unroll=SCAN_UNROLL)
        for d in (1, 2, 4):
            keep = sub8 >= d
            pr, pi = asr_ref[d - 1:d, cs], asi_ref[d - 1:d, cs]
            sr = jnp.where(keep, pltpu.roll(er, d, 0), 0.0)
            si = jnp.where(keep, pltpu.roll(ei, d, 0), 0.0)
            er, ei = er + pr * sr - pi * si, ei + pr * si + pi * sr
        cr, ci = carr_ref[:, cs], cari_ref[:, cs]
        pr, pi = asr_ref[:, cs], asi_ref[:, cs]
        er, ei = er + pr * cr - pi * ci, ei + pr * ci + pi * cr
        inr = jnp.where(sub8 >= 1, pltpu.roll(er, 1, 0), cr)
        ini = jnp.where(sub8 >= 1, pltpu.roll(ei, 1, 0), ci)
        carr_ref[:, cs] = er[SUBLANES - 1:SUBLANES, :]
        cari_ref[:, cs] = ei[SUBLANES - 1:SUBLANES, :]

        def fix(j, carry, cs=cs, ar=ar, ai=ai):
            fr, fi = carry
            fr, fi = ar * fr - ai * fi, ar * fi + ai * fr
            rows = pl.ds(pl.multiple_of(j * SUBLANES, SUBLANES), SUBLANES)
            xr_ref[rows, cs] = xr_ref[rows, cs] + fr
            xi_ref[rows, cs] = xi_ref[rows, cs] + fi
            return fr, fi

        lax.fori_loop(0, seg_len, fix, (inr, ini), unroll=SCAN_UNROLL)

    for blk in range(nb_out):
        ks = slice(blk * kout, (blk + 1) * kout)
        ys = slice(blk * nout, (blk + 1) * nout)
        y = jnp.dot(xr_ref[:, ks].astype(BF16), cout_ref[0, blk], preferred_element_type=F32)
        y = y + jnp.dot(xi_ref[:, ks].astype(BF16), cout_ref[1, blk], preferred_element_type=F32)
        y = y + d_ref[:, ys] * u_ref[:, ys]
        z_ref[:, ys] = jax.nn.gelu(y).astype(BF16)

    zn = jnp.dot(_perm_matrix(ts, seg_len, True), z_ref[...], preferred_element_type=F32).astype(BF16)
    vg = jnp.dot(zn, wglu_ref[...], preferred_element_type=F32)
    o_ref[...] = o_ref[...] + vg[:, :d_model] * jax.nn.sigmoid(vg[:, d_model:])

    @pl.when(tile == last)
    def _():
        _combine_wait(eo_ref, buf_ref, sem, 1 - slot)


def _s5_layer(pending, norm_g, w_in, lam_re, lam_im, log_dt, b_re, b_im, c_re, c_im, d_skip, w_glu):
    x, slab, dest, eo = pending
    t, d = x.shape
    groups, p, k = b_re.shape
    d_ssm = groups * k
    ns = groups * p
    ts = SCAN_TILE
    seg_len = ts // SUBLANES

    lane = lambda v: v.reshape(1, ns).astype(F32)
    rows_k = lambda b: jnp.transpose(b, (2, 0, 1)).reshape(k, ns).astype(F32)
    abr, abi, bb = pl.pallas_call(
        _s5_disc_kernel,
        in_specs=[_const_spec((1, ns))] * 3 + [_const_spec((k, ns))] * 2,
        out_specs=[pl.BlockSpec((1, ns), lambda i: (0, 0))] * 2 + [pl.BlockSpec((2, k, ns), lambda i: (0, 0, 0))],
        out_shape=[jax.ShapeDtypeStruct((1, ns), F32)] * 2 + [jax.ShapeDtypeStruct((2, k, ns), F32)],
        grid=(1,),
        compiler_params=_cparams(("arbitrary",)),
        name="s5_disc",
    )(lane(lam_re), lane(lam_im), lane(jnp.repeat(log_dt, p)), rows_k(b_re), rows_k(b_im))

    gb = 256 // k
    nb = groups // gb
    eye = jnp.eye(gb, dtype=F32)

    def in_blocks(b2):
        m = jnp.einsum('rkcgp,gh->rcgkhp', b2.reshape(2, k, nb, gb, p), eye)
        return m.reshape(2, nb, gb * k, gb * p).astype(BF16)

    def out_blocks(c2):
        m = jnp.einsum('rcgkp,gh->rcgphk', c2.reshape(2, nb, gb, k, p).astype(F32), eye)
        return m.reshape(2, nb, gb * p, gb * k).astype(BF16)

    chunk = 512
    return pl.pallas_call(
        functools.partial(_s5_kernel, chunk=chunk),
        grid_spec=pltpu.PrefetchScalarGridSpec(
            num_scalar_prefetch=1, grid=(t // ts,),
            in_specs=[pl.BlockSpec((ts, d), lambda i, dst: (i, 0)),
                      pl.BlockSpec((ts, ROUTE_LANES), lambda i, dst: (i, 0)),
                      pl.BlockSpec(memory_space=pl.ANY),
                      _const_spec((1, d)), _const_spec((d, d_ssm)),
                      _const_spec((2, nb, gb * k, gb * p)), _const_spec((2, nb, gb * p, gb * k)),
                      _const_spec((1, d_ssm)), _const_spec((1, ns)), _const_spec((1, ns)),
                      _const_spec((d_ssm, 2 * d))],
            out_specs=pl.BlockSpec((ts, d), lambda i, dst: (i, 0)),
            scratch_shapes=[pltpu.VMEM((ts, d_ssm), F32), pltpu.VMEM((ts, d_ssm), BF16),
                            pltpu.VMEM((ts, ns), F32), pltpu.VMEM((ts, ns), F32),
                            pltpu.VMEM((SUBLANES, ns), F32), pltpu.VMEM((SUBLANES, ns), F32),
                            pltpu.VMEM((SUBLANES, ns), F32), pltpu.VMEM((SUBLANES, ns), F32),
                            pltpu.VMEM((1, ns), F32), pltpu.VMEM((1, ns), F32),
                            pltpu.VMEM((2, TOP_K, ts, d), F32), pltpu.SemaphoreType.DMA((2, TOP_K))]),
        out_shape=jax.ShapeDtypeStruct((t, d), F32),
        compiler_params=_cparams(("arbitrary",)),
        name="s5_mix",
    )(dest, x, slab, eo, norm_g.reshape(1, d), w_in.astype(BF16), in_blocks(bb),
      out_blocks(jnp.stack([c_re, -c_im])), d_skip.reshape(1, d_ssm).astype(F32),
      abr, abi, w_glu.astype(BF16))


def _route_kernel(x_ref, g_ref, wr_ref, br_ref, slab_ref, er_ref, cnt_ref, base_ref, *, n_groups, epg):
    @pl.when(pl.program_id(0) == 0)
    def _():
        base_ref[...] = jnp.zeros_like(base_ref)

    tr = x_ref.shape[0]
    xn = _rms(x_ref[...], g_ref[...])
    x_hi = xn.astype(BF16)
    x_lo = (xn - x_hi.astype(F32)).astype(BF16)
    hi = jnp.dot(x_hi, wr_ref[...], preferred_element_type=F32)
    lo = jnp.dot(x_lo, wr_ref[:, :ROUTE_LANES], preferred_element_type=F32)
    logits = hi[:, :ROUTE_LANES] + (hi[:, ROUTE_LANES:] + lo) + br_ref[...]
    lane = lax.broadcasted_iota(I32, logits.shape, 1)
    neg = jnp.float32(-3.0e38)

    def first_argmax(v):
        m = jnp.max(v, axis=1, keepdims=True)
        return m, jnp.min(jnp.where(v == m, lane, ROUTE_LANES), axis=1, keepdims=True)

    gmask = lane < n_groups
    gmax, gidx = first_argmax(jnp.where(gmask, logits, neg))
    p_group = 1.0 / jnp.sum(jnp.where(gmask, jnp.exp(logits - gmax), 0.0), axis=1, keepdims=True)
    lo = n_groups + gidx * epg
    el = jnp.where((lane >= lo) & (lane < lo + epg), logits, neg)
    m1, i1 = first_argmax(el)
    m2, i2 = first_argmax(jnp.where(lane == i1, neg, el))
    t2 = jnp.exp(m2 - m1)
    w1 = p_group / (1.0 + t2)
    w2 = p_group * t2 / (1.0 + t2)

    hit1, hit2 = lane == i1, lane == i2
    onehot = (hit1 | hit2).astype(BF16)
    r = lax.broadcasted_iota(I32, (tr, tr), 0)
    c = lax.broadcasted_iota(I32, (tr, tr), 1)
    before = jnp.dot((c < r).astype(BF16), onehot, preferred_element_type=F32) + base_ref[0:1, :]
    rank1 = jnp.sum(jnp.where(hit1, before, 0.0), axis=1, keepdims=True)
    rank2 = jnp.sum(jnp.where(hit2, before, 0.0), axis=1, keepdims=True)
    total = base_ref[0:1, :] + jnp.sum(onehot.astype(F32), axis=0, keepdims=True)
    base_ref[...] = jnp.broadcast_to(total, base_ref.shape)
    cnt_ref[...] = jnp.broadcast_to(total, cnt_ref.shape)

    vals = (w1, w2, (i1 - n_groups).astype(F32), (i2 - n_groups).astype(F32), rank1, rank2)
    slab = jnp.zeros(logits.shape, F32)
    for k, v in enumerate(vals):
        slab = jnp.where(lane == k, v, slab)
    slab_ref[...] = slab
    er_ref[...] = jnp.transpose(slab)[0:SUBLANES, :]


def _row_gather_start(src_ref, dst_ref, sem, index_of, rows):
    for r in range(rows):
        pltpu.make_async_copy(src_ref.at[pl.ds(index_of(r), 1)], dst_ref.at[pl.ds(r, 1)], sem).start()


def _row_gather_wait(src_ref, dst_ref, sem):
    pltpu.make_async_copy(src_ref.at[pl.ds(0, dst_ref.shape[0])], dst_ref, sem).wait()


def _gmm_kernel(exp_ref, nexp_ref, wslot_ref, padlo_ref, padhi_ref, meta_ref, dest_ref, x_ref, g_ref,
                w1_ref, w3_ref, w2_ref, o_ref, xbuf_ref, w1f_ref, w3f_ref, w2f_ref,
                w1b_ref, w3b_ref, w2b_ref, tok_ref, sem, wsem, *, layer):
    n = pl.program_id(0)
    n_real = meta_ref[0]
    prev = jnp.maximum(n - 1, 0)
    nslots = xbuf_ref.shape[0]
    slot = lax.rem(n, nslots)
    rows = xbuf_ref.shape[1]
    new_expert = (n == 0) | (exp_ref[n] != exp_ref[prev])

    def gather(block, into):
        base = block * rows
        _row_gather_start(x_ref, xbuf_ref.at[into], sem.at[into], lambda r: tok_ref[base + r], rows)

    def weight_copies(expert, into):
        return [pltpu.make_async_copy(src.at[layer, expert], dst.at[into], wsem.at[into, j])
                for j, (src, dst) in enumerate(((w1_ref, w1f_ref), (w3_ref, w3f_ref), (w2_ref, w2f_ref)))]

    @pl.when(n == 0)
    def _():
        for cp in weight_copies(exp_ref[0], wslot_ref[0]):
            cp.start()

        n_tok = dest_ref.shape[0] // TOP_K

        def pad_expert(e, carry):
            def pad(j, c):
                tok_ref[j] = lax.rem(j, n_tok)
                return c
            return lax.fori_loop(padlo_ref[e], padhi_ref[e], pad, carry)

        lax.fori_loop(0, padlo_ref.shape[0], pad_expert, 0)

        def invert(t, carry):
            for k in range(TOP_K):
                tok_ref[dest_ref[k * n_tok + t]] = t
            return carry

        lax.fori_loop(0, n_tok, invert, 0, unroll=8)
        for ahead in range(GMM_AHEAD):
            gather(ahead, ahead)

    @pl.when(new_expert)
    def _():
        ws = wslot_ref[n]
        for cp in weight_copies(exp_ref[n], ws):
            cp.wait()
        w1b_ref[...] = w1f_ref[ws].astype(BF16)
        w3b_ref[...] = w3f_ref[ws].astype(BF16)
        w2b_ref[...] = w2f_ref[ws].astype(BF16)

        @pl.when(nexp_ref[n] != exp_ref[n])
        def _():
            for cp in weight_copies(nexp_ref[n], 1 - ws):
                cp.start()

    @pl.when(n < n_real)
    def _():
        _row_gather_wait(x_ref, xbuf_ref.at[slot], sem.at[slot])
        xb = _rms(xbuf_ref[slot], g_ref[...]).astype(BF16)
        gather(jnp.minimum(n + GMM_AHEAD, n_real - 1), lax.rem(n + GMM_AHEAD, nslots))
        h = jax.nn.silu(jnp.dot(xb, w1b_ref[...], preferred_element_type=F32))
        h = (h * jnp.dot(xb, w3b_ref[...], preferred_element_type=F32)).astype(BF16)
        o_ref[...] = jnp.dot(h, w2b_ref[...], preferred_element_type=F32)

    @pl.when(n >= n_real)
    def _():
        o_ref[...] = jnp.zeros_like(o_ref)

    @pl.when(n == n_real - 1)
    def _():
        for ahead in range(1, GMM_AHEAD + 1):
            other = lax.rem(n + ahead, nslots)
            _row_gather_wait(x_ref, xbuf_ref.at[other], sem.at[other])


def _combine_start(dest_ref, eo_ref, buf_ref, sem, tile, into):
    rows = buf_ref.shape[2]
    n_tok = dest_ref.shape[0] // TOP_K
    for k in range(TOP_K):
        base = k * n_tok + tile * rows
        _row_gather_start(eo_ref, buf_ref.at[into, k], sem.at[into, k],
                          lambda r, base=base: dest_ref[base + r], rows)


def _combine_wait(eo_ref, buf_ref, sem, slot):
    for k in range(TOP_K):
        _row_gather_wait(eo_ref, buf_ref.at[slot, k], sem.at[slot, k])


def _combined_tile(x, slab, buf_ref, slot):
    return x + slab[:, 0:1] * buf_ref[slot, 0] + slab[:, 1:2] * buf_ref[slot, 1]


def _combine_kernel(dest_ref, x_ref, slab_ref, g_ref, eo_ref, o_ref, buf_ref, sem):
    i = pl.program_id(0)
    last = pl.num_programs(0) - 1
    nslots = buf_ref.shape[0]
    slot = lax.rem(i, nslots)

    @pl.when(i == 0)
    def _():
        for ahead in range(COMBINE_AHEAD):
            _combine_start(dest_ref, eo_ref, buf_ref, sem, jnp.minimum(ahead, last), ahead)

    _combine_wait(eo_ref, buf_ref, sem, slot)
    y = _combined_tile(x_ref[...], slab_ref[...], buf_ref, slot)
    _combine_start(dest_ref, eo_ref, buf_ref, sem, jnp.minimum(i + COMBINE_AHEAD, last),
                   lax.rem(i + COMBINE_AHEAD, nslots))
    o_ref[...] = _rms(y, g_ref[...])

    @pl.when(i == last)
    def _():
        for ahead in range(1, COMBINE_AHEAD + 1):
            _combine_wait(eo_ref, buf_ref, sem, lax.rem(i + ahead, nslots))


def _gmm_schedule(counts, n_rows, block):
    n_exp = counts.shape[0]
    n_items = n_rows // block + n_exp
    per_exp = (counts + block - 1) // block
    blk_end = jnp.cumsum(per_exp)
    starts = (blk_end - per_exp) * block
    n_real = blk_end[-1]
    n = jnp.arange(n_items, dtype=I32)
    ids = jnp.arange(n_exp, dtype=I32)
    last_used = jnp.max(jnp.where(per_exp > 0, ids, 0))
    e = jnp.minimum(jnp.sum(blk_end[None, :] <= n[:, None], axis=1), last_used).astype(I32)
    later = jnp.where((per_exp > 0)[None, :] & (ids[None, :] > e[:, None]), ids[None, :], n_exp)
    nxt = jnp.min(later, axis=1)
    nxt = jnp.where(nxt == n_exp, e, nxt).astype(I32)
    wslot = (jnp.sum(((per_exp > 0)[None, :] & (ids[None, :] < e[:, None])).astype(I32), axis=1) & 1)
    pad_lo = (starts + counts).astype(I32)
    pad_hi = (blk_end * block).astype(I32)
    meta = jnp.stack([n_real]).astype(I32)
    return starts, (e, nxt, wslot.astype(I32), pad_lo, pad_hi, meta)


def _moe_layer(x, norm_g, wg, bg, we, be, w1, w3, w2, layer):
    t, d = x.shape
    n_groups, _, epg = we.shape
    n_exp = n_groups * epg
    f = w1.shape[-1]
    tr = min(ROUTER_TILE, t)
    n_rows = t * TOP_K
    g_row = norm_g.reshape(1, d)

    wr = jnp.concatenate([wg, jnp.transpose(we, (1, 0, 2)).reshape(d, n_exp)], axis=1)
    wr = jnp.pad(wr, ((0, 0), (0, ROUTE_LANES - wr.shape[1]))).astype(F32)
    wr_hi = wr.astype(BF16)
    wr = jnp.concatenate([wr_hi, (wr - wr_hi.astype(F32)).astype(BF16)], axis=1)
    br = jnp.pad(jnp.concatenate([bg, be.reshape(-1)]), (0, ROUTE_LANES - n_groups - n_exp))

    slab, er, cnt = pl.pallas_call(
        functools.partial(_route_kernel, n_groups=n_groups, epg=epg),
        grid=(t // tr,),
        in_specs=[pl.BlockSpec((tr, d), lambda i: (i, 0)), _const_spec((1, d)),
                  _const_spec((d, 2 * ROUTE_LANES)), _const_spec((1, ROUTE_LANES))],
        out_specs=[pl.BlockSpec((tr, ROUTE_LANES), lambda i: (i, 0)),
                   pl.BlockSpec((SUBLANES, tr), lambda i: (0, i)),
                   pl.BlockSpec((SUBLANES, ROUTE_LANES), lambda i: (0, 0))],
        out_shape=[jax.ShapeDtypeStruct((t, ROUTE_LANES), F32),
                   jax.ShapeDtypeStruct((SUBLANES, t), F32),
                   jax.ShapeDtypeStruct((SUBLANES, ROUTE_LANES), F32)],
        scratch_shapes=[pltpu.VMEM((SUBLANES, ROUTE_LANES), F32)],
        compiler_params=_cparams(("arbitrary",)),
        name="moe_route",
    )(x, g_row, wr, br.reshape(1, ROUTE_LANES).astype(F32))

    counts = cnt[0, n_groups:n_groups + n_exp].astype(I32)
    starts, items = _gmm_schedule(counts, n_rows, GMM_BLOCK)
    e_idx = er[2:2 + TOP_K].astype(I32)
    start_of = jnp.sum(jnp.where(e_idx[..., None] == jnp.arange(n_exp, dtype=I32), starts, 0), axis=-1)
    dest = (start_of + er[2 + TOP_K:2 + 2 * TOP_K].astype(I32)).reshape(-1)

    n_items = items[0].shape[0]
    any_spec = pl.BlockSpec(memory_space=pl.ANY)
    eo = pl.pallas_call(
        functools.partial(_gmm_kernel, layer=layer),
        grid_spec=pltpu.PrefetchScalarGridSpec(
            num_scalar_prefetch=len(items) + 1, grid=(n_items,),
            in_specs=[any_spec, pl.BlockSpec((1, d), lambda n, *_: (0, 0)),
                      any_spec, any_spec, any_spec],
            out_specs=pl.BlockSpec((GMM_BLOCK, d), lambda n, *_: (n, 0)),
            scratch_shapes=[pltpu.VMEM((GMM_AHEAD + 1, GMM_BLOCK, d), F32),
                            pltpu.VMEM((2, d, f), F32), pltpu.VMEM((2, d, f), F32),
                            pltpu.VMEM((2, f, d), F32),
                            pltpu.VMEM((d, f), BF16), pltpu.VMEM((d, f), BF16), pltpu.VMEM((f, d), BF16),
                            pltpu.SMEM((n_items * GMM_BLOCK,), I32),
                            pltpu.SemaphoreType.DMA((GMM_AHEAD + 1,)),
                            pltpu.SemaphoreType.DMA((2, 3))]),
        out_shape=jax.ShapeDtypeStruct((n_items * GMM_BLOCK, d), F32),
        compiler_params=_cparams(("arbitrary",)),
        name="moe_gmm",
    )(*items, dest, x, g_row, w1, w3, w2)
    return x, slab, dest, eo


def _moe_combine_norm(pending, final_g):
    x, slab, dest, eo = pending
    t, d = x.shape
    tr = ROUTE_TILE
    return pl.pallas_call(
        _combine_kernel,
        grid_spec=pltpu.PrefetchScalarGridSpec(
            num_scalar_prefetch=1, grid=(t // tr,),
            in_specs=[pl.BlockSpec((tr, d), lambda i, dst: (i, 0)),
                      pl.BlockSpec((tr, ROUTE_LANES), lambda i, dst: (i, 0)),
                      pl.BlockSpec((1, d), lambda i, dst: (0, 0)),
                      pl.BlockSpec(memory_space=pl.ANY)],
            out_specs=pl.BlockSpec((tr, d), lambda i, dst: (i, 0)),
            scratch_shapes=[pltpu.VMEM((COMBINE_AHEAD + 1, TOP_K, tr, d), F32),
                            pltpu.SemaphoreType.DMA((COMBINE_AHEAD + 1, TOP_K))]),
        out_shape=jax.ShapeDtypeStruct((t, d), F32),
        compiler_params=_cparams(("arbitrary",)),
        name="moe_combine",
    )(dest, x, slab, final_g.reshape(1, d), eo)


def kernel(x, mix_norm, ffn_norm, final_norm, a_w_in, a_conv_w, a_conv_b, a_gate_a_w, a_gate_a_b, a_gate_x_w, a_gate_x_b, a_lru_L, a_w_out, b_w_in, b_lam_re, b_lam_im, b_log_dt, b_b_re, b_b_im, b_c_re, b_c_im, b_d, b_w_glu, moe_wg, moe_bg, moe_we, moe_be, moe_w1, moe_w3, moe_w2):
    batch, seq, d = x.shape
    depth = mix_norm.shape[0]
    assert depth == 2, "the combine of each MoE layer is fused into the S5 mixer / final norm"
    outs = []
    for b in range(batch):
        h = _rglru_layer(x[b], mix_norm[0], a_w_in[0], a_conv_w[0], a_conv_b[0], a_gate_a_w[0],
                         a_gate_a_b[0], a_gate_x_w[0], a_gate_x_b[0], a_lru_L[0], a_w_out[0])
        pending = _moe_layer(h, ffn_norm[0], moe_wg[0], moe_bg[0], moe_we[0], moe_be[0],
                             moe_w1, moe_w3, moe_w2, 0)
        h = _s5_layer(pending, mix_norm[1], b_w_in[0], b_lam_re[0], b_lam_im[0], b_log_dt[0],
                      b_b_re[0], b_b_im[0], b_c_re[0], b_c_im[0], b_d[0], b_w_glu[0])
        pending = _moe_layer(h, ffn_norm[1], moe_wg[1], moe_bg[1], moe_we[1], moe_be[1],
                             moe_w1, moe_w3, moe_w2, 1)
        outs.append(_moe_combine_norm(pending, final_norm))
    return jnp.stack(outs)
```

```python
import functools
import math

import jax
import jax.numpy as jnp
from jax import lax
from jax.experimental import pallas as pl
from jax.experimental.pallas import tpu as pltpu

F32 = jnp.float32
BF16 = jnp.bfloat16
I32 = jnp.int32

EPS = 1e-6
LRU_C = 8.0
CONV_WIDTH = 4
TOP_K = 2

LANES = 128
SUBLANES = 8
VMEM_LIMIT = 56 * 1024 * 1024

SCAN_TILE = 256
IN_TILE_M = 2048
IN_TILE_N = 256
ROUTER_TILE = 512
ROUTE_TILE = 256
GMM_BLOCK = 256
GMM_AHEAD = 2
COMBINE_AHEAD = 3
SCAN_UNROLL = True
ROUTE_LANES = 128


def _cparams(sem):
    return pltpu.CompilerParams(dimension_semantics=sem, vmem_limit_bytes=VMEM_LIMIT)


def _const_spec(shape):
    nd = len(shape)
    return pl.BlockSpec(shape, lambda *_: (0,) * nd, pipeline_mode=pl.Buffered(1))


def _rms(x, g):
    ms = jnp.mean(x * x, axis=-1, keepdims=True)
    return x * lax.rsqrt(ms + EPS) * g


def _perm_matrix(ts, seg_len, inverse):
    r = lax.broadcasted_iota(I32, (ts, ts), 1 if inverse else 0)
    n = lax.broadcasted_iota(I32, (ts, ts), 0 if inverse else 1)
    return (n == (r & (SUBLANES - 1)) * seg_len + (r >> 3)).astype(BF16)


def _sublane_id(shape):
    return lax.broadcasted_iota(I32, shape, 0) & (SUBLANES - 1)


def _lru_in_kernel(x_ref, g_ref, w_ref, o_ref, xp_ref, *, ts):
    @pl.when(pl.program_id(1) == 0)
    def _():
        p = _perm_matrix(ts, ts // SUBLANES, False)
        for s in range(x_ref.shape[0] // ts):
            rows = slice(s * ts, (s + 1) * ts)
            xn = _rms(x_ref[rows, :], g_ref[...]).astype(BF16)
            xp_ref[rows, :] = jnp.dot(p, xn, preferred_element_type=F32).astype(BF16)

    o_ref[...] = jnp.dot(xp_ref[...], w_ref[...], preferred_element_type=F32)


def _lru_rec_kernel(gate_ref, rec_ref, x_ref, cw_ref, cb_ref, wg_ref, ba_ref, bx_ref, lam_ref,
                    wout_ref, o_ref, ext_ref, a_ref, b_ref, tail_ref, hcar_ref, *, chunk):
    ts, c = rec_ref.shape
    seg_len = ts // SUBLANES
    halo = (CONV_WIDTH - 1) * SUBLANES
    nblk, bw, _ = wg_ref.shape

    @pl.when(pl.program_id(0) == 0)
    def _():
        tail_ref[...] = jnp.zeros_like(tail_ref)
        hcar_ref[...] = jnp.zeros_like(hcar_ref)

    tail = rec_ref[ts - halo:ts, :]
    sub = _sublane_id((halo, c))
    ext_ref[0:halo, :] = jnp.where(sub == 0, pltpu.roll(tail_ref[...], halo - (SUBLANES - 1), 0),
                                   pltpu.roll(tail, 1, 0))
    ext_ref[halo:halo + ts, :] = rec_ref[...]
    tail_ref[...] = tail

    z = -lam_ref[...]
    softplus = jnp.maximum(z, 0.0) + jnp.log1p(jnp.exp(-jnp.abs(z)))

    for blk in range(nblk):
        cs = slice(blk * bw, (blk + 1) * bw)
        conv = cb_ref[:, cs] + cw_ref[0:1, cs] * ext_ref[0:ts, cs]
        for k in range(1, CONV_WIDTH):
            conv = conv + cw_ref[k:k + 1, cs] * ext_ref[k * SUBLANES:k * SUBLANES + ts, cs]
        g = jnp.dot(conv.astype(BF16), wg_ref[blk], preferred_element_type=F32)
        r = jax.nn.sigmoid(g[:, :bw] + ba_ref[:, cs])
        ig = jax.nn.sigmoid(g[:, bw:] + bx_ref[:, cs])
        log_a = (-LRU_C * r) * softplus[:, cs]
        a_ref[:, cs] = jnp.exp(log_a)
        th = jnp.tanh(log_a)
        b_ref[:, cs] = jnp.sqrt(-2.0 * th / (1.0 - th)) * (ig * conv)

    sub8 = _sublane_id((SUBLANES, chunk))
    for ch in range(c // chunk):
        cs = slice(ch * chunk, (ch + 1) * chunk)

        def step(j, carry, cs=cs):
            h, pp = carry
            rows = pl.ds(pl.multiple_of(j * SUBLANES, SUBLANES), SUBLANES)
            a = a_ref[rows, cs]
            h = a * h + b_ref[rows, cs]
            pp = a * pp
            b_ref[rows, cs] = h
            a_ref[rows, cs] = pp
            return h, pp

        hc, pc = lax.fori_loop(0, seg_len, step,
                               (jnp.zeros((SUBLANES, chunk), F32), jnp.ones((SUBLANES, chunk), F32)),
                               unroll=SCAN_UNROLL)
        for d in (1, 2, 4):
            keep = sub8 >= d
            hs = jnp.where(keep, pltpu.roll(hc, d, 0), 0.0)
            ps = jnp.where(keep, pltpu.roll(pc, d, 0), 1.0)
            hc = pc * hs + hc
            pc = pc * ps
        cin = hcar_ref[:, cs]
        end = hc + pc * cin
        seg_in = jnp.where(sub8 >= 1, pltpu.roll(end, 1, 0), cin)
        hcar_ref[:, cs] = end[SUBLANES - 1:SUBLANES, :]

        def fix(j, carry, cs=cs, seg_in=seg_in):
            rows = pl.ds(pl.multiple_of(j * SUBLANES, SUBLANES), SUBLANES)
            b_ref[rows, cs] = b_ref[rows, cs] + a_ref[rows, cs] * seg_in
            return carry

        lax.fori_loop(0, seg_len, fix, 0, unroll=SCAN_UNROLL)

    y = (b_ref[...] * jax.nn.gelu(gate_ref[...])).astype(BF16)
    y = jnp.dot(_perm_matrix(ts, seg_len, True), y, preferred_element_type=F32).astype(BF16)
    o_ref[...] = x_ref[...] + jnp.dot(y, wout_ref[...], preferred_element_type=F32)


def _block_diag(w, per_block):
    h, d, _ = w.shape
    nb = h // per_block
    eye = jnp.eye(per_block, dtype=w.dtype)
    m = jnp.einsum('bhij,hk->bhikj', w.reshape(nb, per_block, d, d), eye)
    return m.reshape(nb, per_block * d, per_block * d)


def _rglru_layer(x, norm_g, w_in, conv_w, conv_b, wa, ba, wx, bx, lru_l, w_out):
    t, d = x.shape
    c = w_out.shape[0]
    heads, hd, _ = wa.shape
    per_block = LANES // math.gcd(hd, LANES)
    ts = SCAN_TILE
    tm = min(IN_TILE_M, t)
    tn = IN_TILE_N

    proj = pl.pallas_call(
        functools.partial(_lru_in_kernel, ts=ts),
        grid=(t // tm, (2 * c) // tn),
        in_specs=[pl.BlockSpec((tm, d), lambda i, j: (i, 0)),
                  pl.BlockSpec((1, d), lambda i, j: (0, 0)),
                  pl.BlockSpec((d, tn), lambda i, j: (0, j))],
        out_specs=pl.BlockSpec((tm, tn), lambda i, j: (i, j)),
        out_shape=jax.ShapeDtypeStruct((t, 2 * c), F32),
        scratch_shapes=[pltpu.VMEM((tm, d), BF16)],
        compiler_params=_cparams(("parallel", "arbitrary")),
        name="lru_in",
    )(x, norm_g.reshape(1, d), w_in.astype(BF16))

    wg = jnp.concatenate([_block_diag(wa, per_block), _block_diag(wx, per_block)], axis=-1).astype(BF16)
    nblk, bw, _ = wg.shape
    halo = (CONV_WIDTH - 1) * SUBLANES
    row = lambda v: v.reshape(1, c)
    return pl.pallas_call(
        functools.partial(_lru_rec_kernel, chunk=bw * 2),
        grid=(t // ts,),
        in_specs=[pl.BlockSpec((ts, c), lambda i: (i, 0)),
                  pl.BlockSpec((ts, c), lambda i: (i, 1)),
                  pl.BlockSpec((ts, d), lambda i: (i, 0)),
                  _const_spec((CONV_WIDTH, c)), _const_spec((1, c)),
                  _const_spec((nblk, bw, 2 * bw)),
                  _const_spec((1, c)), _const_spec((1, c)), _const_spec((1, c)),
                  _const_spec((c, d))],
        out_specs=pl.BlockSpec((ts, d), lambda i: (i, 0)),
        out_shape=jax.ShapeDtypeStruct((t, d), F32),
        scratch_shapes=[pltpu.VMEM((ts + halo, c), F32), pltpu.VMEM((ts, c), F32),
                        pltpu.VMEM((ts, c), F32), pltpu.VMEM((halo, c), F32),
                        pltpu.VMEM((1, c), F32)],
        compiler_params=_cparams(("arbitrary",)),
        name="lru_rec",
    )(proj, proj, x, conv_w, row(conv_b), wg, row(ba), row(bx), row(lru_l), w_out.astype(BF16))


def _s5_disc_kernel(lre_ref, lim_ref, dt_ref, bre_ref, bim_ref, abr_ref, abi_ref, bb_ref):
    lre = jnp.minimum(lre_ref[...], -1e-4)
    lim = lim_ref[...]
    dt = jnp.exp(dt_ref[...])
    mag = jnp.exp(lre * dt)
    abr = mag * jnp.cos(lim * dt)
    abi = mag * jnp.sin(lim * dt)
    den = lre * lre + lim * lim
    kr = ((abr - 1.0) * lre + abi * lim) / den
    ki = (abi * lre - (abr - 1.0) * lim) / den
    abr_ref[...] = abr
    abi_ref[...] = abi
    bb_ref[0] = kr * bre_ref[...] - ki * bim_ref[...]
    bb_ref[1] = kr * bim_ref[...] + ki * bre_ref[...]


def _s5_kernel(dest_ref, x_ref, slab_ref, eo_ref, g_ref, win_ref, bin_ref, cout_ref,
               d_ref, abr_ref, abi_ref, wglu_ref, o_ref, u_ref, z_ref, xr_ref, xi_ref, a8r_ref,
               a8i_ref, asr_ref, asi_ref, carr_ref, cari_ref, buf_ref, sem, *, chunk):
    ts, d_model = x_ref.shape
    tile = pl.program_id(0)
    last = pl.num_programs(0) - 1
    slot = tile & 1
    seg_len = ts // SUBLANES
    ns = xr_ref.shape[1]
    _, nb_in, kin, nin = bin_ref.shape
    _, nb_out, kout, nout = cout_ref.shape

    @pl.when(tile == 0)
    def _():
        _combine_start(dest_ref, eo_ref, buf_ref, sem, 0, 0)
        carr_ref[...] = jnp.zeros_like(carr_ref)
        cari_ref[...] = jnp.zeros_like(cari_ref)
        ar, ai = abr_ref[...], abi_ref[...]
        a8r_ref[...] = jnp.broadcast_to(ar, a8r_ref.shape)
        a8i_ref[...] = jnp.broadcast_to(ai, a8i_ref.shape)

        def power(j, cur):
            pr, pi = cur
            return pr * ar - pi * ai, pr * ai + pi * ar

        lr, li = lax.fori_loop(0, seg_len - 1, power, (ar, ai))

        def seg_powers(s, cur):
            pr, pi = cur
            asr_ref[pl.ds(s, 1), :] = pr
            asi_ref[pl.ds(s, 1), :] = pi
            return pr * lr - pi * li, pr * li + pi * lr

        lax.fori_loop(0, SUBLANES, seg_powers, (lr, li))

    _combine_wait(eo_ref, buf_ref, sem, slot)
    x = _combined_tile(x_ref[...], slab_ref[...], buf_ref, slot)
    o_ref[...] = x
    _combine_start(dest_ref, eo_ref, buf_ref, sem, jnp.minimum(tile + 1, last), 1 - slot)
    xn = _rms(x, g_ref[...]).astype(BF16)
    xp = jnp.dot(_perm_matrix(ts, seg_len, False), xn, preferred_element_type=F32).astype(BF16)
    u = jnp.dot(xp, win_ref[...], preferred_element_type=F32)
    u_ref[...] = u
    ub = u.astype(BF16)
    for blk in range(nb_in):
        uc = ub[:, blk * kin:(blk + 1) * kin]
        xr_ref[:, blk * nin:(blk + 1) * nin] = jnp.dot(uc, bin_ref[0, blk], preferred_element_type=F32)
        xi_ref[:, blk * nin:(blk + 1) * nin] = jnp.dot(uc, bin_ref[1, blk], preferred_element_type=F32)

    sub8 = _sublane_id((SUBLANES, chunk))
    for ch in range(ns // chunk):
        cs = slice(ch * chunk, (ch + 1) * chunk)
        ar, ai = a8r_ref[:, cs], a8i_ref[:, cs]

        def step(j, carry, cs=cs, ar=ar, ai=ai):
            sr, si = carry
            rows = pl.ds(pl.multiple_of(j * SUBLANES, SUBLANES), SUBLANES)
            nr = ar * sr - ai * si + xr_ref[rows, cs]
            ni = ar * si + ai * sr + xi_ref[rows, cs]
            xr_ref[rows, cs] = nr
            xi_ref[rows, cs] = ni
            return nr, ni

        zero = jnp.zeros((SUBLANES, chunk), F32)
        er, ei = lax.fori_loop(0, seg_len, step, (zero, zero), unroll=SCAN_UNROLL)
        for d in (1, 2, 4):
            keep = sub8 >= d
            pr, pi = asr_ref[d - 1:d, cs], asi_ref[d - 1:d, cs]
            sr = jnp.where(keep, pltpu.roll(er, d, 0), 0.0)
            si = jnp.where(keep, pltpu.roll(ei, d, 0), 0.0)
            er, ei = er + pr * sr - pi * si, ei + pr * si + pi * sr
        cr, ci = carr_ref[:, cs], cari_ref[:, cs]
        pr, pi = asr_ref[:, cs], asi_ref[:, cs]
        er, ei = er + pr * cr - pi * ci, ei + pr * ci + pi * cr
        inr = jnp.where(sub8 >= 1, pltpu.roll(er, 1, 0), cr)
        ini = jnp.where(sub8 >= 1, pltpu.roll(ei, 1, 0), ci)
        carr_ref[:, cs] = er[SUBLANES - 1:SUBLANES, :]
        cari_ref[:, cs] = ei[SUBLANES - 1:SUBLANES, :]

        def fix(j, carry, cs=cs, ar=ar, ai=ai):
            fr, fi = carry
            fr, fi = ar * fr - ai * fi, ar * fi + ai * fr
            rows = pl.ds(pl.multiple_of(j * SUBLANES, SUBLANES), SUBLANES)
            xr_ref[rows, cs] = xr_ref[rows, cs] + fr
            xi_ref[rows, cs] = xi_ref[rows, cs] + fi
            return fr, fi

        lax.fori_loop(0, seg_len, fix, (inr, ini), unroll=SCAN_UNROLL)

    for blk in range(nb_out):
        ks = slice(blk * kout, (blk + 1) * kout)
        ys = slice(blk * nout, (blk + 1) * nout)
        y = jnp.dot(xr_ref[:, ks].astype(BF16), cout_ref[0, blk], preferred_element_type=F32)
        y = y + jnp.dot(xi_ref[:, ks].astype(BF16), cout_ref[1, blk], preferred_element_type=F32)
        y = y + d_ref[:, ys] * u_ref[:, ys]
        z_ref[:, ys] = jax.nn.gelu(y).astype(BF16)

    zn = jnp.dot(_perm_matrix(ts, seg_len, True), z_ref[...], preferred_element_type=F32).astype(BF16)
    vg = jnp.dot(zn, wglu_ref[...], preferred_element_type=F32)
    o_ref[...] = o_ref[...] + vg[:, :d_model] * jax.nn.sigmoid(vg[:, d_model:])

    @pl.when(tile == last)
    def _():
        _combine_wait(eo_ref, buf_ref, sem, 1 - slot)


def _s5_layer(pending, norm_g, w_in, lam_re, lam_im, log_dt, b_re, b_im, c_re, c_im, d_skip, w_glu):
    x, slab, dest, eo = pending
    t, d = x.shape
    groups, p, k = b_re.shape
    d_ssm = groups * k
    ns = groups * p
    ts = SCAN_TILE
    seg_len = ts // SUBLANES

    lane = lambda v: v.reshape(1, ns).astype(F32)
    rows_k = lambda b: jnp.transpose(b, (2, 0, 1)).reshape(k, ns).astype(F32)
    abr, abi, bb = pl.pallas_call(
        _s5_disc_kernel,
        in_specs=[_const_spec((1, ns))] * 3 + [_const_spec((k, ns))] * 2,
        out_specs=[pl.BlockSpec((1, ns), lambda i: (0, 0))] * 2 + [pl.BlockSpec((2, k, ns), lambda i: (0, 0, 0))],
        out_shape=[jax.ShapeDtypeStruct((1, ns), F32)] * 2 + [jax.ShapeDtypeStruct((2, k, ns), F32)],
        grid=(1,),
        compiler_params=_cparams(("arbitrary",)),
        name="s5_disc",
    )(lane(lam_re), lane(lam_im), lane(jnp.repeat(log_dt, p)), rows_k(b_re), rows_k(b_im))

    gb = 256 // k
    nb = groups // gb
    eye = jnp.eye(gb, dtype=F32)

    def in_blocks(b2):
        m = jnp.einsum('rkcgp,gh->rcgkhp', b2.reshape(2, k, nb, gb, p), eye)
        return m.reshape(2, nb, gb * k, gb * p).astype(BF16)

    def out_blocks(c2):
        m = jnp.einsum('rcgkp,gh->rcgphk', c2.reshape(2, nb, gb, k, p).astype(F32), eye)
        return m.reshape(2, nb, gb * p, gb * k).astype(BF16)

    chunk = 512
    return pl.pallas_call(
        functools.partial(_s5_kernel, chunk=chunk),
        grid_spec=pltpu.PrefetchScalarGridSpec(
            num_scalar_prefetch=1, grid=(t // ts,),
            in_specs=[pl.BlockSpec((ts, d), lambda i, dst: (i, 0)),
                      pl.BlockSpec((ts, ROUTE_LANES), lambda i, dst: (i, 0)),
                      pl.BlockSpec(memory_space=pl.ANY),
                      _const_spec((1, d)), _const_spec((d, d_ssm)),
                      _const_spec((2, nb, gb * k, gb * p)), _const_spec((2, nb, gb * p, gb * k)),
                      _const_spec((1, d_ssm)), _const_spec((1, ns)), _const_spec((1, ns)),
                      _const_spec((d_ssm, 2 * d))],
            out_specs=pl.BlockSpec((ts, d), lambda i, dst: (i, 0)),
            scratch_shapes=[pltpu.VMEM((ts, d_ssm), F32), pltpu.VMEM((ts, d_ssm), BF16),
                            pltpu.VMEM((ts, ns), F32), pltpu.VMEM((ts, ns), F32),
                            pltpu.VMEM((SUBLANES, ns), F32), pltpu.VMEM((SUBLANES, ns), F32),
                            pltpu.VMEM((SUBLANES, ns), F32), pltpu.VMEM((SUBLANES, ns), F32),
                            pltpu.VMEM((1, ns), F32), pltpu.VMEM((1, ns), F32),
                            pltpu.VMEM((2, TOP_K, ts, d), F32), pltpu.SemaphoreType.DMA((2, TOP_K))]),
        out_shape=jax.ShapeDtypeStruct((t, d), F32),
        compiler_params=_cparams(("arbitrary",)),
        name="s5_mix",
    )(dest, x, slab, eo, norm_g.reshape(1, d), w_in.astype(BF16), in_blocks(bb),
      out_blocks(jnp.stack([c_re, -c_im])), d_skip.reshape(1, d_ssm).astype(F32),
      abr, abi, w_glu.astype(BF16))


def _route_kernel(x_ref, g_ref, wr_ref, br_ref, slab_ref, er_ref, cnt_ref, base_ref, *, n_groups, epg):
    @pl.when(pl.program_id(0) == 0)
    def _():
        base_ref[...] = jnp.zeros_like(base_ref)

    tr = x_ref.shape[0]
    xn = _rms(x_ref[...], g_ref[...])
    x_hi = xn.astype(BF16)
    x_lo = (xn - x_hi.astype(F32)).astype(BF16)
    hi = jnp.dot(x_hi, wr_ref[...], preferred_element_type=F32)
    lo = jnp.dot(x_lo, wr_ref[:, :ROUTE_LANES], preferred_element_type=F32)
    logits = hi[:, :ROUTE_LANES] + (hi[:, ROUTE_LANES:] + lo) + br_ref[...]
    lane = lax.broadcasted_iota(I32, logits.shape, 1)
    neg = jnp.float32(-3.0e38)

    def first_argmax(v):
        m = jnp.max(v, axis=1, keepdims=True)
        return m, jnp.min(jnp.where(v == m, lane, ROUTE_LANES), axis=1, keepdims=True)

    gmask = lane < n_groups
    gmax, gidx = first_argmax(jnp.where(gmask, logits, neg))
    p_group = 1.0 / jnp.sum(jnp.where(gmask, jnp.exp(logits - gmax), 0.0), axis=1, keepdims=True)
    lo = n_groups + gidx * epg
    el = jnp.where((lane >= lo) & (lane < lo + epg), logits, neg)
    m1, i1 = first_argmax(el)
    m2, i2 = first_argmax(jnp.where(lane == i1, neg, el))
    t2 = jnp.exp(m2 - m1)
    w1 = p_group / (1.0 + t2)
    w2 = p_group * t2 / (1.0 + t2)

    hit1, hit2 = lane == i1, lane == i2
    onehot = (hit1 | hit2).astype(BF16)
    r = lax.broadcasted_iota(I32, (tr, tr), 0)
    c = lax.broadcasted_iota(I32, (tr, tr), 1)
    before = jnp.dot((c < r).astype(BF16), onehot, preferred_element_type=F32) + base_ref[0:1, :]
    rank1 = jnp.sum(jnp.where(hit1, before, 0.0), axis=1, keepdims=True)
    rank2 = jnp.sum(jnp.where(hit2, before, 0.0), axis=1, keepdims=True)
    total = base_ref[0:1, :] + jnp.sum(onehot.astype(F32), axis=0, keepdims=True)
    base_ref[...] = jnp.broadcast_to(total, base_ref.shape)
    cnt_ref[...] = jnp.broadcast_to(total, cnt_ref.shape)

    vals = (w1, w2, (i1 - n_groups).astype(F32), (i2 - n_groups).astype(F32), rank1, rank2)
    slab = jnp.zeros(logits.shape, F32)
    for k, v in enumerate(vals):
        slab = jnp.where(lane == k, v, slab)
    slab_ref[...] = slab
    er_ref[...] = jnp.transpose(slab)[0:SUBLANES, :]


def _row_gather_start(src_ref, dst_ref, sem, index_of, rows):
    for r in range(rows):
        pltpu.make_async_copy(src_ref.at[pl.ds(index_of(r), 1)], dst_ref.at[pl.ds(r, 1)], sem).start()


def _row_gather_wait(src_ref, dst_ref, sem):
    pltpu.make_async_copy(src_ref.at[pl.ds(0, dst_ref.shape[0])], dst_ref, sem).wait()


def _gmm_kernel(exp_ref, nexp_ref, wslot_ref, padlo_ref, padhi_ref, meta_ref, dest_ref, x_ref, g_ref,
                w1_ref, w3_ref, w2_ref, o_ref, xbuf_ref, w1f_ref, w3f_ref, w2f_ref,
                w1b_ref, w3b_ref, w2b_ref, tok_ref, sem, wsem, *, layer):
    n = pl.program_id(0)
    n_real = meta_ref[0]
    prev = jnp.maximum(n - 1, 0)
    nslots = xbuf_ref.shape[0]
    slot = lax.rem(n, nslots)
    rows = xbuf_ref.shape[1]
    new_expert = (n == 0) | (exp_ref[n] != exp_ref[prev])

    def gather(block, into):
        base = block * rows
        _row_gather_start(x_ref, xbuf_ref.at[into], sem.at[into], lambda r: tok_ref[base + r], rows)

    def weight_copies(expert, into):
        return [pltpu.make_async_copy(src.at[layer, expert], dst.at[into], wsem.at[into, j])
                for j, (src, dst) in enumerate(((w1_ref, w1f_ref), (w3_ref, w3f_ref), (w2_ref, w2f_ref)))]

    @pl.when(n == 0)
    def _():
        for cp in weight_copies(exp_ref[0], wslot_ref[0]):
            cp.start()

        n_tok = dest_ref.shape[0] // TOP_K

        def pad_expert(e, carry):
            def pad(j, c):
                tok_ref[j] = lax.rem(j, n_tok)
                return c
            return lax.fori_loop(padlo_ref[e], padhi_ref[e], pad, carry)

        lax.fori_loop(0, padlo_ref.shape[0], pad_expert, 0)

        def invert(t, carry):
            for k in range(TOP_K):
                tok_ref[dest_ref[k * n_tok + t]] = t
            return carry

        lax.fori_loop(0, n_tok, invert, 0, unroll=8)
        for ahead in range(GMM_AHEAD):
            gather(ahead, ahead)

    @pl.when(new_expert)
    def _():
        ws = wslot_ref[n]
        for cp in weight_copies(exp_ref[n], ws):
            cp.wait()
        w1b_ref[...] = w1f_ref[ws].astype(BF16)
        w3b_ref[...] = w3f_ref[ws].astype(BF16)
        w2b_ref[...] = w2f_ref[ws].astype(BF16)

        @pl.when(nexp_ref[n] != exp_ref[n])
        def _():
            for cp in weight_copies(nexp_ref[n], 1 - ws):
                cp.start()

    @pl.when(n < n_real)
    def _():
        _row_gather_wait(x_ref, xbuf_ref.at[slot], sem.at[slot])
        xb = _rms(xbuf_ref[slot], g_ref[...]).astype(BF16)
        gather(jnp.minimum(n + GMM_AHEAD, n_real - 1), lax.rem(n + GMM_AHEAD, nslots))
        h = jax.nn.silu(jnp.dot(xb, w1b_ref[...], preferred_element_type=F32))
        h = (h * jnp.dot(xb, w3b_ref[...], preferred_element_type=F32)).astype(BF16)
        o_ref[...] = jnp.dot(h, w2b_ref[...], preferred_element_type=F32)

    @pl.when(n >= n_real)
    def _():
        o_ref[...] = jnp.zeros_like(o_ref)

    @pl.when(n == n_real - 1)
    def _():
        for ahead in range(1, GMM_AHEAD + 1):
            other = lax.rem(n + ahead, nslots)
            _row_gather_wait(x_ref, xbuf_ref.at[other], sem.at[other])


def _combine_start(dest_ref, eo_ref, buf_ref, sem, tile, into):
    rows = buf_ref.shape[2]
    n_tok = dest_ref.shape[0] // TOP_K
    for k in range(TOP_K):
        base = k * n_tok + tile * rows
        _row_gather_start(eo_ref, buf_ref.at[into, k], sem.at[into, k],
                          lambda r, base=base: dest_ref[base + r], rows)


def _combine_wait(eo_ref, buf_ref, sem, slot):
    for k in range(TOP_K):
        _row_gather_wait(eo_ref, buf_ref.at[slot, k], sem.at[slot, k])


def _combined_tile(x, slab, buf_ref, slot):
    return x + slab[:, 0:1] * buf_ref[slot, 0] + slab[:, 1:2] * buf_ref[slot, 1]


def _combine_kernel(dest_ref, x_ref, slab_ref, g_ref, eo_ref, o_ref, buf_ref, sem):
    i = pl.program_id(0)
    last = pl.num_programs(0) - 1
    nslots = buf_ref.shape[0]
    slot = lax.rem(i, nslots)

    @pl.when(i == 0)
    def _():
        for ahead in range(COMBINE_AHEAD):
            _combine_start(dest_ref, eo_ref, buf_ref, sem, jnp.minimum(ahead, last), ahead)

    _combine_wait(eo_ref, buf_ref, sem, slot)
    y = _combined_tile(x_ref[...], slab_ref[...], buf_ref, slot)
    _combine_start(dest_ref, eo_ref, buf_ref, sem, jnp.minimum(i + COMBINE_AHEAD, last),
                   lax.rem(i + COMBINE_AHEAD, nslots))
    o_ref[...] = _rms(y, g_ref[...])

    @pl.when(i == last)
    def _():
        for ahead in range(1, COMBINE_AHEAD + 1):
            _combine_wait(eo_ref, buf_ref, sem, lax.rem(i + ahead, nslots))


def _gmm_schedule(counts, n_rows, block):
    n_exp = counts.shape[0]
    n_items = n_rows // block + n_exp
    per_exp = (counts + block - 1) // block
    blk_end = jnp.cumsum(per_exp)
    starts = (blk_end - per_exp) * block
    n_real = blk_end[-1]
    n = jnp.arange(n_items, dtype=I32)
    ids = jnp.arange(n_exp, dtype=I32)
    last_used = jnp.max(jnp.where(per_exp > 0, ids, 0))
    e = jnp.minimum(jnp.sum(blk_end[None, :] <= n[:, None], axis=1), last_used).astype(I32)
    later = jnp.where((per_exp > 0)[None, :] & (ids[None, :] > e[:, None]), ids[None, :], n_exp)
    nxt = jnp.min(later, axis=1)
    nxt = jnp.where(nxt == n_exp, e, nxt).astype(I32)
    wslot = (jnp.sum(((per_exp > 0)[None, :] & (ids[None, :] < e[:, None])).astype(I32), axis=1) & 1)
    pad_lo = (starts + counts).astype(I32)
    pad_hi = (blk_end * block).astype(I32)
    meta = jnp.stack([n_real]).astype(I32)
    return starts, (e, nxt, wslot.astype(I32), pad_lo, pad_hi, meta)


def _moe_layer(x, norm_g, wg, bg, we, be, w1, w3, w2, layer):
    t, d = x.shape
    n_groups, _, epg = we.shape
    n_exp = n_groups * epg
    f = w1.shape[-1]
    tr = min(ROUTER_TILE, t)
    n_rows = t * TOP_K
    g_row = norm_g.reshape(1, d)

    wr = jnp.concatenate([wg, jnp.transpose(we, (1, 0, 2)).reshape(d, n_exp)], axis=1)
    wr = jnp.pad(wr, ((0, 0), (0, ROUTE_LANES - wr.shape[1]))).astype(F32)
    wr_hi = wr.astype(BF16)
    wr = jnp.concatenate([wr_hi, (wr - wr_hi.astype(F32)).astype(BF16)], axis=1)
    br = jnp.pad(jnp.concatenate([bg, be.reshape(-1)]), (0, ROUTE_LANES - n_groups - n_exp))

    slab, er, cnt = pl.pallas_call(
        functools.partial(_route_kernel, n_groups=n_groups, epg=epg),
        grid=(t // tr,),
        in_specs=[pl.BlockSpec((tr, d), lambda i: (i, 0)), _const_spec((1, d)),
                  _const_spec((d, 2 * ROUTE_LANES)), _const_spec((1, ROUTE_LANES))],
        out_specs=[pl.BlockSpec((tr, ROUTE_LANES), lambda i: (i, 0)),
                   pl.BlockSpec((SUBLANES, tr), lambda i: (0, i)),
                   pl.BlockSpec((SUBLANES, ROUTE_LANES), lambda i: (0, 0))],
        out_shape=[jax.ShapeDtypeStruct((t, ROUTE_LANES), F32),
                   jax.ShapeDtypeStruct((SUBLANES, t), F32),
                   jax.ShapeDtypeStruct((SUBLANES, ROUTE_LANES), F32)],
        scratch_shapes=[pltpu.VMEM((SUBLANES, ROUTE_LANES), F32)],
        compiler_params=_cparams(("arbitrary",)),
        name="moe_route",
    )(x, g_row, wr, br.reshape(1, ROUTE_LANES).astype(F32))

    counts = cnt[0, n_groups:n_groups + n_exp].astype(I32)
    starts, items = _gmm_schedule(counts, n_rows, GMM_BLOCK)
    e_idx = er[2:2 + TOP_K].astype(I32)
    start_of = jnp.sum(jnp.where(e_idx[..., None] == jnp.arange(n_exp, dtype=I32), starts, 0), axis=-1)
    dest = (start_of + er[2 + TOP_K:2 + 2 * TOP_K].astype(I32)).reshape(-1)

    n_items = items[0].shape[0]
    any_spec = pl.BlockSpec(memory_space=pl.ANY)
    eo = pl.pallas_call(
        functools.partial(_gmm_kernel, layer=layer),
        grid_spec=pltpu.PrefetchScalarGridSpec(
            num_scalar_prefetch=len(items) + 1, grid=(n_items,),
            in_specs=[any_spec, pl.BlockSpec((1, d), lambda n, *_: (0, 0)),
                      any_spec, any_spec, any_spec],
            out_specs=pl.BlockSpec((GMM_BLOCK, d), lambda n, *_: (n, 0)),
            scratch_shapes=[pltpu.VMEM((GMM_AHEAD + 1, GMM_BLOCK, d), F32),
                            pltpu.VMEM((2, d, f), F32), pltpu.VMEM((2, d, f), F32),
                            pltpu.VMEM((2, f, d), F32),
                            pltpu.VMEM((d, f), BF16), pltpu.VMEM((d, f), BF16), pltpu.VMEM((f, d), BF16),
                            pltpu.SMEM((n_items * GMM_BLOCK,), I32),
                            pltpu.SemaphoreType.DMA((GMM_AHEAD + 1,)),
                            pltpu.SemaphoreType.DMA((2, 3))]),
        out_shape=jax.ShapeDtypeStruct((n_items * GMM_BLOCK, d), F32),
        compiler_params=_cparams(("arbitrary",)),
        name="moe_gmm",
    )(*items, dest, x, g_row, w1, w3, w2)
    return x, slab, dest, eo


def _moe_combine_norm(pending, final_g):
    x, slab, dest, eo = pending
    t, d = x.shape
    tr = ROUTE_TILE
    return pl.pallas_call(
        _combine_kernel,
        grid_spec=pltpu.PrefetchScalarGridSpec(
            num_scalar_prefetch=1, grid=(t // tr,),
            in_specs=[pl.BlockSpec((tr, d), lambda i, dst: (i, 0)),
                      pl.BlockSpec((tr, ROUTE_LANES), lambda i, dst: (i, 0)),
                      pl.BlockSpec((1, d), lambda i, dst: (0, 0)),
                      pl.BlockSpec(memory_space=pl.ANY)],
            out_specs=pl.BlockSpec((tr, d), lambda i, dst: (i, 0)),
            scratch_shapes=[pltpu.VMEM((COMBINE_AHEAD + 1, TOP_K, tr, d), F32),
                            pltpu.SemaphoreType.DMA((COMBINE_AHEAD + 1, TOP_K))]),
        out_shape=jax.ShapeDtypeStruct((t, d), F32),
        compiler_params=_cparams(("arbitrary",)),
        name="moe_combine",
    )(dest, x, slab, final_g.reshape(1, d), eo)


def kernel(x, mix_norm, ffn_norm, final_norm, a_w_in, a_conv_w, a_conv_b, a_gate_a_w, a_gate_a_b, a_gate_x_w, a_gate_x_b, a_lru_L, a_w_out, b_w_in, b_lam_re, b_lam_im, b_log_dt, b_b_re, b_b_im, b_c_re, b_c_im, b_d, b_w_glu, moe_wg, moe_bg, moe_we, moe_be, moe_w1, moe_w3, moe_w2):
    batch, seq, d = x.shape
    depth = mix_norm.shape[0]
    assert depth == 2, "the combine of each MoE layer is fused into the S5 mixer / final norm"
    outs = []
    for b in range(batch):
        h = _rglru_layer(x[b], mix_norm[0], a_w_in[0], a_conv_w[0], a_conv_b[0], a_gate_a_w[0],
                         a_gate_a_b[0], a_gate_x_w[0], a_gate_x_b[0], a_lru_L[0], a_w_out[0])
        pending = _moe_layer(h, ffn_norm[0], moe_wg[0], moe_bg[0], moe_we[0], moe_be[0],
                             moe_w1, moe_w3, moe_w2, 0)
        h = _s5_layer(pending, mix_norm[1], b_w_in[0], b_lam_re[0], b_lam_im[0], b_log_dt[0],
                      b_b_re[0], b_b_im[0], b_c_re[0], b_c_im[0], b_d[0], b_w_glu[0])
        pending = _moe_layer(h, ffn_norm[1], moe_wg[1], moe_bg[1], moe_we[1], moe_be[1],
                             moe_w1, moe_w3, moe_w2, 1)
        outs.append(_moe_combine_norm(pending, final_norm))
    return jnp.stack(outs)
```

```python
import functools
import math

import jax
import jax.numpy as jnp
from jax import lax
from jax.experimental import pallas as pl
from jax.experimental.pallas import tpu as pltpu

F32 = jnp.float32
BF16 = jnp.bfloat16
I32 = jnp.int32

EPS = 1e-6
LRU_C = 8.0
CONV_WIDTH = 4
TOP_K = 2

LANES = 128
SUBLANES = 8
MXU_DIM = 256
VMEM_LIMIT = 56 * 1024 * 1024

SCAN_TILE = 256
IN_TILE_M = 1024
IN_TILE_N = 1280
ROUTER_TILE = 512
ROUTE_TILE = 256
GMM_BLOCK = 256
GMM_AHEAD = 2
COMBINE_AHEAD = 2
SCAN_UNROLL = True
ROUTE_LANES = 128


def _cparams(sem):
    return pltpu.CompilerParams(dimension_semantics=sem, vmem_limit_bytes=VMEM_LIMIT)


def _const_spec(shape):
    nd = len(shape)
    return pl.BlockSpec(shape, lambda *_: (0,) * nd, pipeline_mode=pl.Buffered(1))


def _rms(x, g):
    ms = jnp.mean(x * x, axis=-1, keepdims=True)
    return x * lax.rsqrt(ms + EPS) * g


def _perm_matrix(ts, seg_len, inverse):
    r = lax.broadcasted_iota(I32, (ts, ts), 1 if inverse else 0)
    n = lax.broadcasted_iota(I32, (ts, ts), 0 if inverse else 1)
    return (n == (r % SUBLANES) * seg_len + r // SUBLANES).astype(BF16)


def _sublane_id(shape):
    return lax.broadcasted_iota(I32, shape, 0) & (SUBLANES - 1)


def _lru_in_kernel(x_ref, g_ref, w_ref, o_ref, xp_ref, *, ts):
    @pl.when(pl.program_id(1) == 0)
    def _():
        p = _perm_matrix(ts, ts // SUBLANES, False)
        for s in range(x_ref.shape[0] // ts):
            rows = slice(s * ts, (s + 1) * ts)
            xn = _rms(x_ref[rows, :], g_ref[...]).astype(BF16)
            xp_ref[rows, :] = jnp.dot(p, xn, preferred_element_type=F32).astype(BF16)

    o_ref[...] = jnp.dot(xp_ref[...], w_ref[...], preferred_element_type=F32)


def _lru_rec_kernel(gate_ref, rec_ref, x_ref, cw_ref, cb_ref, wg_ref, ba_ref, bx_ref, lam_ref,
                    wout_ref, o_ref, ext_ref, a_ref, b_ref, tail_ref, hcar_ref, *, chunk):
    ts, c = rec_ref.shape
    seg_len = ts // SUBLANES
    halo = (CONV_WIDTH - 1) * SUBLANES
    nblk, bw, _ = wg_ref.shape

    @pl.when(pl.program_id(0) == 0)
    def _():
        tail_ref[...] = jnp.zeros_like(tail_ref)
        hcar_ref[...] = jnp.zeros_like(hcar_ref)

    tail = rec_ref[ts - halo:ts, :]
    sub = _sublane_id((halo, c))
    ext_ref[0:halo, :] = jnp.where(sub == 0, pltpu.roll(tail_ref[...], halo - (SUBLANES - 1), 0),
                                   pltpu.roll(tail, 1, 0))
    ext_ref[halo:halo + ts, :] = rec_ref[...]
    tail_ref[...] = tail

    z = -lam_ref[...]
    softplus = jnp.maximum(z, 0.0) + jnp.log1p(jnp.exp(-jnp.abs(z)))

    for blk in range(nblk):
        cs = slice(blk * bw, (blk + 1) * bw)
        conv = cb_ref[:, cs] + cw_ref[0:1, cs] * ext_ref[0:ts, cs]
        for k in range(1, CONV_WIDTH):
            conv = conv + cw_ref[k:k + 1, cs] * ext_ref[k * SUBLANES:k * SUBLANES + ts, cs]
        g = jnp.dot(conv.astype(BF16), wg_ref[blk], preferred_element_type=F32)
        r = jax.nn.sigmoid(g[:, :bw] + ba_ref[:, cs])
        ig = jax.nn.sigmoid(g[:, bw:] + bx_ref[:, cs])
        log_a = (-LRU_C * r) * softplus[:, cs]
        a_ref[:, cs] = jnp.exp(log_a)
        th = jnp.tanh(log_a)
        b_ref[:, cs] = jnp.sqrt(-2.0 * th / (1.0 - th)) * (ig * conv)

    sub8 = _sublane_id((SUBLANES, chunk))
    for ch in range(c // chunk):
        cs = slice(ch * chunk, (ch + 1) * chunk)

        def step(j, carry, cs=cs):
            h, pp = carry
            rows = pl.ds(pl.multiple_of(j * SUBLANES, SUBLANES), SUBLANES)
            a = a_ref[rows, cs]
            h = a * h + b_ref[rows, cs]
            pp = a * pp
            b_ref[rows, cs] = h
            a_ref[rows, cs] = pp
            return h, pp

        hc, pc = lax.fori_loop(0, seg_len, step,
                               (jnp.zeros((SUBLANES, chunk), F32), jnp.ones((SUBLANES, chunk), F32)),
                               unroll=SCAN_UNROLL)
        for d in (1, 2, 4):
            keep = sub8 >= d
            hs = jnp.where(keep, pltpu.roll(hc, d, 0), 0.0)
            ps = jnp.where(keep, pltpu.roll(pc, d, 0), 1.0)
            hc = pc * hs + hc
            pc = pc * ps
        cin = hcar_ref[:, cs]
        end = hc + pc * cin
        seg_in = jnp.where(sub8 >= 1, pltpu.roll(end, 1, 0), cin)
        hcar_ref[:, cs] = end[SUBLANES - 1:SUBLANES, :]

        def fix(j, carry, cs=cs, seg_in=seg_in):
            rows = pl.ds(pl.multiple_of(j * SUBLANES, SUBLANES), SUBLANES)
            b_ref[rows, cs] = b_ref[rows, cs] + a_ref[rows, cs] * seg_in
            return carry

        lax.fori_loop(0, seg_len, fix, 0, unroll=SCAN_UNROLL)

    y = (b_ref[...] * jax.nn.gelu(gate_ref[...])).astype(BF16)
    y = jnp.dot(_perm_matrix(ts, seg_len, True), y, preferred_element_type=F32).astype(BF16)
    o_ref[...] = x_ref[...] + jnp.dot(y, wout_ref[...], preferred_element_type=F32)


def _block_diag(w, per_block):
    h, d, _ = w.shape
    nb = h // per_block
    eye = jnp.eye(per_block, dtype=w.dtype)
    m = jnp.einsum('bhij,hk->bhikj', w.reshape(nb, per_block, d, d), eye)
    return m.reshape(nb, per_block * d, per_block * d)


def _rglru_layer(x, norm_g, w_in, conv_w, conv_b, wa, ba, wx, bx, lru_l, w_out):
    t, d = x.shape
    c = w_out.shape[0]
    hd = wa.shape[1]
    per_block = LANES // math.gcd(hd, LANES)
    ts = SCAN_TILE
    tm = min(IN_TILE_M, t)
    tn = IN_TILE_N

    proj = pl.pallas_call(
        functools.partial(_lru_in_kernel, ts=ts),
        grid=(t // tm, (2 * c) // tn),
        in_specs=[pl.BlockSpec((tm, d), lambda i, j: (i, 0)),
                  pl.BlockSpec((1, d), lambda i, j: (0, 0)),
                  pl.BlockSpec((d, tn), lambda i, j: (0, j))],
        out_specs=pl.BlockSpec((tm, tn), lambda i, j: (i, j)),
        out_shape=jax.ShapeDtypeStruct((t, 2 * c), F32),
        scratch_shapes=[pltpu.VMEM((tm, d), BF16)],
        compiler_params=_cparams(("parallel", "arbitrary")),
        name="lru_in",
    )(x, norm_g.reshape(1, d), w_in.astype(BF16))

    wg = jnp.concatenate([_block_diag(wa, per_block), _block_diag(wx, per_block)], axis=-1).astype(BF16)
    nblk, bw, _ = wg.shape
    halo = (CONV_WIDTH - 1) * SUBLANES
    row = lambda v: v.reshape(1, c)
    return pl.pallas_call(
        functools.partial(_lru_rec_kernel, chunk=bw * 2),
        grid=(t // ts,),
        in_specs=[pl.BlockSpec((ts, c), lambda i: (i, 0)),
                  pl.BlockSpec((ts, c), lambda i: (i, 1)),
                  pl.BlockSpec((ts, d), lambda i: (i, 0)),
                  _const_spec((CONV_WIDTH, c)), _const_spec((1, c)),
                  _const_spec((nblk, bw, 2 * bw)),
                  _const_spec((1, c)), _const_spec((1, c)), _const_spec((1, c)),
                  _const_spec((c, d))],
        out_specs=pl.BlockSpec((ts, d), lambda i: (i, 0)),
        out_shape=jax.ShapeDtypeStruct((t, d), F32),
        scratch_shapes=[pltpu.VMEM((ts + halo, c), F32), pltpu.VMEM((ts, c), F32),
                        pltpu.VMEM((ts, c), F32), pltpu.VMEM((halo, c), F32),
                        pltpu.VMEM((1, c), F32)],
        compiler_params=_cparams(("arbitrary",)),
        name="lru_rec",
    )(proj, proj, x, conv_w, row(conv_b), wg, row(ba), row(bx), row(lru_l), w_out.astype(BF16))


def _s5_disc_kernel(lre_ref, lim_ref, dt_ref, bre_ref, bim_ref, abr_ref, abi_ref, bb_ref):
    lre = jnp.minimum(lre_ref[...], -1e-4)
    lim = lim_ref[...]
    dt = jnp.exp(dt_ref[...])
    mag = jnp.exp(lre * dt)
    abr = mag * jnp.cos(lim * dt)
    abi = mag * jnp.sin(lim * dt)
    den = lre * lre + lim * lim
    kr = ((abr - 1.0) * lre + abi * lim) / den
    ki = (abi * lre - (abr - 1.0) * lim) / den
    abr_ref[...] = abr
    abi_ref[...] = abi
    bb_ref[0] = kr * bre_ref[...] - ki * bim_ref[...]
    bb_ref[1] = kr * bim_ref[...] + ki * bre_ref[...]


def _s5_kernel(dest_ref, x_ref, slab_ref, eo_ref, g_ref, win_ref, bin_ref, cout_ref,
               d_ref, abr_ref, abi_ref, wglu_ref, o_ref, u_ref, z_ref, xr_ref, xi_ref, a8r_ref,
               a8i_ref, asr_ref, asi_ref, carr_ref, cari_ref, buf_ref, sem, *, chunk):
    ts, d_model = x_ref.shape
    tile = pl.program_id(0)
    last = pl.num_programs(0) - 1
    slot = tile & 1
    seg_len = ts // SUBLANES
    ns = xr_ref.shape[1]
    _, nb_in, kin, nin = bin_ref.shape
    _, nb_out, kout, nout = cout_ref.shape

    @pl.when(tile == 0)
    def _():
        _combine_start(dest_ref, eo_ref, buf_ref, sem, 0, 0)
        carr_ref[...] = jnp.zeros_like(carr_ref)
        cari_ref[...] = jnp.zeros_like(cari_ref)
        ar, ai = abr_ref[...], abi_ref[...]
        a8r_ref[...] = jnp.broadcast_to(ar, a8r_ref.shape)
        a8i_ref[...] = jnp.broadcast_to(ai, a8i_ref.shape)

        def power(j, cur):
            pr, pi = cur
            return pr * ar - pi * ai, pr * ai + pi * ar

        lr, li = lax.fori_loop(0, seg_len - 1, power, (ar, ai))

        def seg_powers(s, cur):
            pr, pi = cur
            asr_ref[pl.ds(s, 1), :] = pr
            asi_ref[pl.ds(s, 1), :] = pi
            return pr * lr - pi * li, pr * li + pi * lr

        lax.fori_loop(0, SUBLANES, seg_powers, (lr, li))

    _combine_wait(eo_ref, buf_ref, sem, slot)
    x = _combined_tile(x_ref[...], slab_ref[...], buf_ref, slot)
    o_ref[...] = x
    _combine_start(dest_ref, eo_ref, buf_ref, sem, jnp.minimum(tile + 1, last), 1 - slot)
    xn = _rms(x, g_ref[...]).astype(BF16)
    xp = jnp.dot(_perm_matrix(ts, seg_len, False), xn, preferred_element_type=F32).astype(BF16)
    u = jnp.dot(xp, win_ref[...], preferred_element_type=F32)
    u_ref[...] = u
    ub = u.astype(BF16)
    for blk in range(nb_in):
        uc = ub[:, blk * kin:(blk + 1) * kin]
        xr_ref[:, blk * nin:(blk + 1) * nin] = jnp.dot(uc, bin_ref[0, blk], preferred_element_type=F32)
        xi_ref[:, blk * nin:(blk + 1) * nin] = jnp.dot(uc, bin_ref[1, blk], preferred_element_type=F32)

    sub8 = _sublane_id((SUBLANES, chunk))
    for ch in range(ns // chunk):
        cs = slice(ch * chunk, (ch + 1) * chunk)
        ar, ai = a8r_ref[:, cs], a8i_ref[:, cs]

        def step(j, carry, cs=cs, ar=ar, ai=ai):
            sr, si = carry
            rows = pl.ds(pl.multiple_of(j * SUBLANES, SUBLANES), SUBLANES)
            nr = ar * sr - ai * si + xr_ref[rows, cs]
            ni = ar * si + ai * sr + xi_ref[rows, cs]
            xr_ref[rows, cs] = nr
            xi_ref[rows, cs] = ni
            return nr, ni

        zero = jnp.zeros((SUBLANES, chunk), F32)
        er, ei = lax.fori_loop(0, seg_len, step, (zero, zero), unroll=SCAN_UNROLL)
        for d in (1, 2, 4):
            keep = sub8 >= d
            pr, pi = asr_ref[d - 1:d, cs], asi_ref[d - 1:d, cs]
            sr = jnp.where(keep, pltpu.roll(er, d, 0), 0.0)
            si = jnp.where(keep, pltpu.roll(ei, d, 0), 0.0)
            er, ei = er + pr * sr - pi * si, ei + pr * si + pi * sr
        cr, ci = carr_ref[:, cs], cari_ref[:, cs]
        pr, pi = asr_ref[:, cs], asi_ref[:, cs]
        er, ei = er + pr * cr - pi * ci, ei + pr * ci + pi * cr
        inr = jnp.where(sub8 >= 1, pltpu.roll(er, 1, 0), cr)
        ini = jnp.where(sub8 >= 1, pltpu.roll(ei, 1, 0), ci)
        carr_ref[:, cs] = er[SUBLANES - 1:SUBLANES, :]
        cari_ref[:, cs] = ei[SUBLANES - 1:SUBLANES, :]

        def fix(j, carry, cs=cs, ar=ar, ai=ai):
            fr, fi = carry
            fr, fi = ar * fr - ai * fi, ar * fi + ai * fr
            rows = pl.ds(pl.multiple_of(j * SUBLANES, SUBLANES), SUBLANES)
            xr_ref[rows, cs] = xr_ref[rows, cs] + fr
            xi_ref[rows, cs] = xi_ref[rows, cs] + fi
            return fr, fi

        lax.fori_loop(0, seg_len, fix, (inr, ini), unroll=SCAN_UNROLL)

    for blk in range(nb_out):
        ks = slice(blk * kout, (blk + 1) * kout)
        ys = slice(blk * nout, (blk + 1) * nout)
        y = jnp.dot(xr_ref[:, ks].astype(BF16), cout_ref[0, blk], preferred_element_type=F32)
        y = y + jnp.dot(xi_ref[:, ks].astype(BF16), cout_ref[1, blk], preferred_element_type=F32)
        y = y + d_ref[:, ys] * u_ref[:, ys]
        z_ref[:, ys] = jax.nn.gelu(y).astype(BF16)

    zn = jnp.dot(_perm_matrix(ts, seg_len, True), z_ref[...], preferred_element_type=F32).astype(BF16)
    vg = jnp.dot(zn, wglu_ref[...], preferred_element_type=F32)
    o_ref[...] = o_ref[...] + vg[:, :d_model] * jax.nn.sigmoid(vg[:, d_model:])

    @pl.when(tile == last)
    def _():
        _combine_wait(eo_ref, buf_ref, sem, 1 - slot)


def _s5_layer(pending, norm_g, w_in, lam_re, lam_im, log_dt, b_re, b_im, c_re, c_im, d_skip, w_glu):
    x, slab, dest, eo = pending
    t, d = x.shape
    groups, p, k = b_re.shape
    d_ssm = groups * k
    ns = groups * p
    ts = SCAN_TILE
    seg_len = ts // SUBLANES

    lane = lambda v: v.reshape(1, ns).astype(F32)
    rows_k = lambda b: jnp.transpose(b, (2, 0, 1)).reshape(k, ns).astype(F32)
    abr, abi, bb = pl.pallas_call(
        _s5_disc_kernel,
        in_specs=[_const_spec((1, ns))] * 3 + [_const_spec((k, ns))] * 2,
        out_specs=[pl.BlockSpec((1, ns), lambda i: (0, 0))] * 2 + [pl.BlockSpec((2, k, ns), lambda i: (0, 0, 0))],
        out_shape=[jax.ShapeDtypeStruct((1, ns), F32)] * 2 + [jax.ShapeDtypeStruct((2, k, ns), F32)],
        grid=(1,),
        compiler_params=_cparams(("arbitrary",)),
        name="s5_disc",
    )(lane(lam_re), lane(lam_im), lane(jnp.repeat(log_dt, p)), rows_k(b_re), rows_k(b_im))

    gb = MXU_DIM // k
    nb = groups // gb
    eye = jnp.eye(gb, dtype=F32)

    def in_blocks(b2):
        m = jnp.einsum('rkcgp,gh->rcgkhp', b2.reshape(2, k, nb, gb, p), eye)
        return m.reshape(2, nb, gb * k, gb * p).astype(BF16)

    def out_blocks(c2):
        m = jnp.einsum('rcgkp,gh->rcgphk', c2.reshape(2, nb, gb, k, p).astype(F32), eye)
        return m.reshape(2, nb, gb * p, gb * k).astype(BF16)

    chunk = 512
    return pl.pallas_call(
        functools.partial(_s5_kernel, chunk=chunk),
        grid_spec=pltpu.PrefetchScalarGridSpec(
            num_scalar_prefetch=1, grid=(t // ts,),
            in_specs=[pl.BlockSpec((ts, d), lambda i, dst: (i, 0)),
                      pl.BlockSpec((ts, ROUTE_LANES), lambda i, dst: (i, 0)),
                      pl.BlockSpec(memory_space=pl.ANY),
                      _const_spec((1, d)), _const_spec((d, d_ssm)),
                      _const_spec((2, nb, gb * k, gb * p)), _const_spec((2, nb, gb * p, gb * k)),
                      _const_spec((1, d_ssm)), _const_spec((1, ns)), _const_spec((1, ns)),
                      _const_spec((d_ssm, 2 * d))],
            out_specs=pl.BlockSpec((ts, d), lambda i, dst: (i, 0)),
            scratch_shapes=[pltpu.VMEM((ts, d_ssm), F32), pltpu.VMEM((ts, d_ssm), BF16),
                            pltpu.VMEM((ts, ns), F32), pltpu.VMEM((ts, ns), F32),
                            pltpu.VMEM((SUBLANES, ns), F32), pltpu.VMEM((SUBLANES, ns), F32),
                            pltpu.VMEM((SUBLANES, ns), F32), pltpu.VMEM((SUBLANES, ns), F32),
                            pltpu.VMEM((1, ns), F32), pltpu.VMEM((1, ns), F32),
                            pltpu.VMEM((2, TOP_K, ts, d), F32), pltpu.SemaphoreType.DMA((2, TOP_K))]),
        out_shape=jax.ShapeDtypeStruct((t, d), F32),
        compiler_params=_cparams(("arbitrary",)),
        name="s5_mix",
    )(dest, x, slab, eo, norm_g.reshape(1, d), w_in.astype(BF16), in_blocks(bb),
      out_blocks(jnp.stack([c_re, -c_im])), d_skip.reshape(1, d_ssm).astype(F32),
      abr, abi, w_glu.astype(BF16))


def _route_kernel(x_ref, g_ref, wr_ref, br_ref, slab_ref, er_ref, cnt_ref, base_ref, *, n_groups, epg):
    @pl.when(pl.program_id(0) == 0)
    def _():
        base_ref[...] = jnp.zeros_like(base_ref)

    tr = x_ref.shape[0]
    xn = _rms(x_ref[...], g_ref[...])
    x_hi = xn.astype(BF16)
    x_lo = (xn - x_hi.astype(F32)).astype(BF16)
    hi = jnp.dot(x_hi, wr_ref[...], preferred_element_type=F32)
    lo = jnp.dot(x_lo, wr_ref[:, :ROUTE_LANES], preferred_element_type=F32)
    logits = hi[:, :ROUTE_LANES] + (hi[:, ROUTE_LANES:] + lo) + br_ref[...]
    lane = lax.broadcasted_iota(I32, logits.shape, 1)
    neg = jnp.float32(-3.0e38)

    def first_argmax(v):
        m = jnp.max(v, axis=1, keepdims=True)
        return m, jnp.min(jnp.where(v == m, lane, ROUTE_LANES), axis=1, keepdims=True)

    gmask = lane < n_groups
    gmax, gidx = first_argmax(jnp.where(gmask, logits, neg))
    p_group = 1.0 / jnp.sum(jnp.where(gmask, jnp.exp(logits - gmax), 0.0), axis=1, keepdims=True)
    lo = n_groups + gidx * epg
    el = jnp.where((lane >= lo) & (lane < lo + epg), logits, neg)
    m1, i1 = first_argmax(el)
    m2, i2 = first_argmax(jnp.where(lane == i1, neg, el))
    t2 = jnp.exp(m2 - m1)
    w1 = p_group / (1.0 + t2)
    w2 = p_group * t2 / (1.0 + t2)

    hit1, hit2 = lane == i1, lane == i2
    onehot = (hit1 | hit2).astype(BF16)
    r = lax.broadcasted_iota(I32, (tr, tr), 0)
    c = lax.broadcasted_iota(I32, (tr, tr), 1)
    before = jnp.dot((c < r).astype(BF16), onehot, preferred_element_type=F32) + base_ref[0:1, :]
    rank1 = jnp.sum(jnp.where(hit1, before, 0.0), axis=1, keepdims=True)
    rank2 = jnp.sum(jnp.where(hit2, before, 0.0), axis=1, keepdims=True)
    total = base_ref[0:1, :] + jnp.sum(onehot.astype(F32), axis=0, keepdims=True)
    base_ref[...] = jnp.broadcast_to(total, base_ref.shape)
    cnt_ref[...] = jnp.broadcast_to(total, cnt_ref.shape)

    vals = (w1, w2, (i1 - n_groups).astype(F32), (i2 - n_groups).astype(F32), rank1, rank2)
    slab = jnp.zeros(logits.shape, F32)
    for k, v in enumerate(vals):
        slab = jnp.where(lane == k, v, slab)
    slab_ref[...] = slab
    er_ref[...] = jnp.transpose(slab)[0:SUBLANES, :]


def _row_gather_start(src_ref, dst_ref, sem, index_of, rows):
    for r in range(rows):
        pltpu.make_async_copy(src_ref.at[pl.ds(index_of(r), 1)], dst_ref.at[pl.ds(r, 1)], sem).start()


def _row_gather_wait(src_ref, dst_ref, sem):
    pltpu.make_async_copy(src_ref.at[pl.ds(0, dst_ref.shape[0])], dst_ref, sem).wait()


def _gmm_kernel(exp_ref, nexp_ref, wslot_ref, padlo_ref, padhi_ref, meta_ref, dest_ref, x_ref, g_ref,
                w1_ref, w3_ref, w2_ref, o_ref, xbuf_ref, w1f_ref, w3f_ref, w2f_ref,
                w1b_ref, w3b_ref, w2b_ref, tok_ref, sem, wsem, *, layer):
    n = pl.program_id(0)
    n_real = meta_ref[0]
    prev = jnp.maximum(n - 1, 0)
    nslots = xbuf_ref.shape[0]
    slot = lax.rem(n, nslots)
    rows = xbuf_ref.shape[1]
    new_expert = (n == 0) | (exp_ref[n] != exp_ref[prev])

    def gather(block, into):
        base = block * rows
        _row_gather_start(x_ref, xbuf_ref.at[into], sem.at[into], lambda r: tok_ref[base + r], rows)

    def weight_copies(expert, into):
        return [pltpu.make_async_copy(src.at[layer, expert], dst.at[into], wsem.at[into, j])
                for j, (src, dst) in enumerate(((w1_ref, w1f_ref), (w3_ref, w3f_ref), (w2_ref, w2f_ref)))]

    @pl.when(n == 0)
    def _():
        for cp in weight_copies(exp_ref[0], wslot_ref[0]):
            cp.start()

        n_tok = dest_ref.shape[0] // TOP_K

        def pad_expert(e, carry):
            def pad(j, c):
                tok_ref[j] = lax.rem(j, n_tok)
                return c
            return lax.fori_loop(padlo_ref[e], padhi_ref[e], pad, carry)

        lax.fori_loop(0, padlo_ref.shape[0], pad_expert, 0)

        def invert(t, carry):
            for k in range(TOP_K):
                tok_ref[dest_ref[k * n_tok + t]] = t
            return carry

        lax.fori_loop(0, n_tok, invert, 0, unroll=8)
        for ahead in range(GMM_AHEAD):
            gather(ahead, ahead)

    @pl.when(new_expert)
    def _():
        ws = wslot_ref[n]
        for cp in weight_copies(exp_ref[n], ws):
            cp.wait()
        w1b_ref[...] = w1f_ref[ws].astype(BF16)
        w3b_ref[...] = w3f_ref[ws].astype(BF16)
        w2b_ref[...] = w2f_ref[ws].astype(BF16)

        @pl.when(nexp_ref[n] != exp_ref[n])
        def _():
            for cp in weight_copies(nexp_ref[n], 1 - ws):
                cp.start()

    @pl.when(n < n_real)
    def _():
        _row_gather_wait(x_ref, xbuf_ref.at[slot], sem.at[slot])
        xb = _rms(xbuf_ref[slot], g_ref[...]).astype(BF16)
        gather(jnp.minimum(n + GMM_AHEAD, n_real - 1), lax.rem(n + GMM_AHEAD, nslots))
        h = jax.nn.silu(jnp.dot(xb, w1b_ref[...], preferred_element_type=F32))
        h = (h * jnp.dot(xb, w3b_ref[...], preferred_element_type=F32)).astype(BF16)
        o_ref[...] = jnp.dot(h, w2b_ref[...], preferred_element_type=F32)

    @pl.when(n >= n_real)
    def _():
        o_ref[...] = jnp.zeros_like(o_ref)

    @pl.when(n == n_real - 1)
    def _():
        for ahead in range(1, GMM_AHEAD + 1):
            other = lax.rem(n + ahead, nslots)
            _row_gather_wait(x_ref, xbuf_ref.at[other], sem.at[other])


def _combine_start(dest_ref, eo_ref, buf_ref, sem, tile, into):
    rows = buf_ref.shape[2]
    n_tok = dest_ref.shape[0] // TOP_K
    for k in range(TOP_K):
        base = k * n_tok + tile * rows
        _row_gather_start(eo_ref, buf_ref.at[into, k], sem.at[into, k],
                          lambda r, base=base: dest_ref[base + r], rows)


def _combine_wait(eo_ref, buf_ref, sem, slot):
    for k in range(TOP_K):
        _row_gather_wait(eo_ref, buf_ref.at[slot, k], sem.at[slot, k])


def _combined_tile(x, slab, buf_ref, slot):
    return x + slab[:, 0:1] * buf_ref[slot, 0] + slab[:, 1:2] * buf_ref[slot, 1]


def _combine_kernel(dest_ref, x_ref, slab_ref, g_ref, eo_ref, o_ref, buf_ref, sem):
    i = pl.program_id(0)
    last = pl.num_programs(0) - 1
    nslots = buf_ref.shape[0]
    slot = lax.rem(i, nslots)

    @pl.when(i == 0)
    def _():
        for ahead in range(COMBINE_AHEAD):
            _combine_start(dest_ref, eo_ref, buf_ref, sem, jnp.minimum(ahead, last), ahead)

    _combine_wait(eo_ref, buf_ref, sem, slot)
    y = _combined_tile(x_ref[...], slab_ref[...], buf_ref, slot)
    _combine_start(dest_ref, eo_ref, buf_ref, sem, jnp.minimum(i + COMBINE_AHEAD, last),
                   lax.rem(i + COMBINE_AHEAD, nslots))
    o_ref[...] = _rms(y, g_ref[...])

    @pl.when(i == last)
    def _():
        for ahead in range(1, COMBINE_AHEAD + 1):
            _combine_wait(eo_ref, buf_ref, sem, lax.rem(i + ahead, nslots))


def _gmm_schedule(counts, n_rows, block):
    n_exp = counts.shape[0]
    n_items = n_rows // block + n_exp
    per_exp = (counts + block - 1) // block
    blk_end = jnp.cumsum(per_exp)
    starts = (blk_end - per_exp) * block
    n_real = blk_end[-1]
    n = jnp.arange(n_items, dtype=I32)
    ids = jnp.arange(n_exp, dtype=I32)
    last_used = jnp.max(jnp.where(per_exp > 0, ids, 0))
    e = jnp.minimum(jnp.sum(blk_end[None, :] <= n[:, None], axis=1), last_used).astype(I32)
    later = jnp.where((per_exp > 0)[None, :] & (ids[None, :] > e[:, None]), ids[None, :], n_exp)
    nxt = jnp.min(later, axis=1)
    nxt = jnp.where(nxt == n_exp, e, nxt).astype(I32)
    wslot = (jnp.sum(((per_exp > 0)[None, :] & (ids[None, :] < e[:, None])).astype(I32), axis=1) & 1)
    pad_lo = (starts + counts).astype(I32)
    pad_hi = (blk_end * block).astype(I32)
    meta = jnp.stack([n_real]).astype(I32)
    return starts, (e, nxt, wslot.astype(I32), pad_lo, pad_hi, meta)


def _moe_layer(x, norm_g, wg, bg, we, be, w1, w3, w2, layer):
    t, d = x.shape
    n_groups, _, epg = we.shape
    n_exp = n_groups * epg
    f = w1.shape[-1]
    tr = min(ROUTER_TILE, t)
    n_rows = t * TOP_K
    g_row = norm_g.reshape(1, d)

    wr = jnp.concatenate([wg, jnp.transpose(we, (1, 0, 2)).reshape(d, n_exp)], axis=1)
    wr = jnp.pad(wr, ((0, 0), (0, ROUTE_LANES - wr.shape[1]))).astype(F32)
    wr_hi = wr.astype(BF16)
    wr = jnp.concatenate([wr_hi, (wr - wr_hi.astype(F32)).astype(BF16)], axis=1)
    br = jnp.pad(jnp.concatenate([bg, be.reshape(-1)]), (0, ROUTE_LANES - n_groups - n_exp))

    slab, er, cnt = pl.pallas_call(
        functools.partial(_route_kernel, n_groups=n_groups, epg=epg),
        grid=(t // tr,),
        in_specs=[pl.BlockSpec((tr, d), lambda i: (i, 0)), _const_spec((1, d)),
                  _const_spec((d, 2 * ROUTE_LANES)), _const_spec((1, ROUTE_LANES))],
        out_specs=[pl.BlockSpec((tr, ROUTE_LANES), lambda i: (i, 0)),
                   pl.BlockSpec((SUBLANES, tr), lambda i: (0, i)),
                   pl.BlockSpec((SUBLANES, ROUTE_LANES), lambda i: (0, 0))],
        out_shape=[jax.ShapeDtypeStruct((t, ROUTE_LANES), F32),
                   jax.ShapeDtypeStruct((SUBLANES, t), F32),
                   jax.ShapeDtypeStruct((SUBLANES, ROUTE_LANES), F32)],
        scratch_shapes=[pltpu.VMEM((SUBLANES, ROUTE_LANES), F32)],
        compiler_params=_cparams(("arbitrary",)),
        name="moe_route",
    )(x, g_row, wr, br.reshape(1, ROUTE_LANES).astype(F32))

    counts = cnt[0, n_groups:n_groups + n_exp].astype(I32)
    starts, items = _gmm_schedule(counts, n_rows, GMM_BLOCK)
    e_idx = er[2:2 + TOP_K].astype(I32)
    start_of = jnp.sum(jnp.where(e_idx[..., None] == jnp.arange(n_exp, dtype=I32), starts, 0), axis=-1)
    dest = (start_of + er[2 + TOP_K:2 + 2 * TOP_K].astype(I32)).reshape(-1)

    n_items = items[0].shape[0]
    any_spec = pl.BlockSpec(memory_space=pl.ANY)
    eo = pl.pallas_call(
        functools.partial(_gmm_kernel, layer=layer),
        grid_spec=pltpu.PrefetchScalarGridSpec(
            num_scalar_prefetch=len(items) + 1, grid=(n_items,),
            in_specs=[any_spec, pl.BlockSpec((1, d), lambda n, *_: (0, 0)),
                      any_spec, any_spec, any_spec],
            out_specs=pl.BlockSpec((GMM_BLOCK, d), lambda n, *_: (n, 0)),
            scratch_shapes=[pltpu.VMEM((GMM_AHEAD + 1, GMM_BLOCK, d), F32),
                            pltpu.VMEM((2, d, f), F32), pltpu.VMEM((2, d, f), F32),
                            pltpu.VMEM((2, f, d), F32),
                            pltpu.VMEM((d, f), BF16), pltpu.VMEM((d, f), BF16), pltpu.VMEM((f, d), BF16),
                            pltpu.SMEM((n_items * GMM_BLOCK,), I32),
                            pltpu.SemaphoreType.DMA((GMM_AHEAD + 1,)),
                            pltpu.SemaphoreType.DMA((2, 3))]),
        out_shape=jax.ShapeDtypeStruct((n_items * GMM_BLOCK, d), F32),
        compiler_params=_cparams(("arbitrary",)),
        name="moe_gmm",
    )(*items, dest, x, g_row, w1, w3, w2)
    return x, slab, dest, eo


def _moe_combine_norm(pending, final_g):
    x, slab, dest, eo = pending
    t, d = x.shape
    tr = ROUTE_TILE
    return pl.pallas_call(
        _combine_kernel,
        grid_spec=pltpu.PrefetchScalarGridSpec(
            num_scalar_prefetch=1, grid=(t // tr,),
            in_specs=[pl.BlockSpec((tr, d), lambda i, dst: (i, 0)),
                      pl.BlockSpec((tr, ROUTE_LANES), lambda i, dst: (i, 0)),
                      pl.BlockSpec((1, d), lambda i, dst: (0, 0)),
                      pl.BlockSpec(memory_space=pl.ANY)],
            out_specs=pl.BlockSpec((tr, d), lambda i, dst: (i, 0)),
            scratch_shapes=[pltpu.VMEM((COMBINE_AHEAD + 1, TOP_K, tr, d), F32),
                            pltpu.SemaphoreType.DMA((COMBINE_AHEAD + 1, TOP_K))]),
        out_shape=jax.ShapeDtypeStruct((t, d), F32),
        compiler_params=_cparams(("arbitrary",)),
        name="moe_combine",
    )(dest, x, slab, final_g.reshape(1, d), eo)


def kernel(x, mix_norm, ffn_norm, final_norm, a_w_in, a_conv_w, a_conv_b, a_gate_a_w, a_gate_a_b, a_gate_x_w, a_gate_x_b, a_lru_L, a_w_out, b_w_in, b_lam_re, b_lam_im, b_log_dt, b_b_re, b_b_im, b_c_re, b_c_im, b_d, b_w_glu, moe_wg, moe_bg, moe_we, moe_be, moe_w1, moe_w3, moe_w2):
    batch, seq, d = x.shape
    depth = mix_norm.shape[0]
    assert depth == 2, "the combine of each MoE layer is fused into the S5 mixer / final norm"
    outs = []
    for b in range(batch):
        h = _rglru_layer(x[b], mix_norm[0], a_w_in[0], a_conv_w[0], a_conv_b[0], a_gate_a_w[0],
                         a_gate_a_b[0], a_gate_x_w[0], a_gate_x_b[0], a_lru_L[0], a_w_out[0])
        pending = _moe_layer(h, ffn_norm[0], moe_wg[0], moe_bg[0], moe_we[0], moe_be[0],
                             moe_w1, moe_w3, moe_w2, 0)
        h = _s5_layer(pending, mix_norm[1], b_w_in[0], b_lam_re[0], b_lam_im[0], b_log_dt[0],
                      b_b_re[0], b_b_im[0], b_c_re[0], b_c_im[0], b_d[0], b_w_glu[0])
        pending = _moe_layer(h, ffn_norm[1], moe_wg[1], moe_bg[1], moe_we[1], moe_be[1],
                             moe_w1, moe_w3, moe_w2, 1)
        outs.append(_moe_combine_norm(pending, final_norm))
    return jnp.stack(outs)
```

```python
import functools
import math

import jax
import jax.numpy as jnp
from jax import lax
from jax.experimental import pallas as pl
from jax.experimental.pallas import tpu as pltpu

F32 = jnp.float32
BF16 = jnp.bfloat16
I32 = jnp.int32

EPS = 1e-6
LRU_C = 8.0
CONV_WIDTH = 4
TOP_K = 2

LANES = 128
SUBLANES = 8
MXU_DIM = 256
VMEM_LIMIT = 56 * 1024 * 1024

SCAN_TILE = 256
IN_TILE_M = 1024
IN_TILE_N = 1280
ROUTER_TILE = 512
ROUTE_TILE = 256
GMM_BLOCK = 256
GMM_AHEAD = 2
COMBINE_AHEAD = 2
SCAN_UNROLL = True
ROUTE_LANES = 128


def _cparams(sem):
    return pltpu.CompilerParams(dimension_semantics=sem, vmem_limit_bytes=VMEM_LIMIT)


def _const_spec(shape):
    nd = len(shape)
    return pl.BlockSpec(shape, lambda *_: (0,) * nd, pipeline_mode=pl.Buffered(1))


def _rms(x, g):
    ms = jnp.mean(x * x, axis=-1, keepdims=True)
    return x * lax.rsqrt(ms + EPS) * g


_GELU_C1 = math.sqrt(2.0 / math.pi)
_GELU_C2 = _GELU_C1 * 0.044715


def _gelu_tanh(x):
    hx = 0.5 * x
    return hx + hx * jnp.tanh(x * (_GELU_C1 + _GELU_C2 * (x * x)))


def _perm_matrix(ts, seg_len, inverse):
    r = lax.broadcasted_iota(I32, (ts, ts), 1 if inverse else 0)
    n = lax.broadcasted_iota(I32, (ts, ts), 0 if inverse else 1)
    return (n == (r % SUBLANES) * seg_len + r // SUBLANES).astype(BF16)


def _sublane_id(shape):
    return lax.broadcasted_iota(I32, shape, 0) & (SUBLANES - 1)


def _lru_in_kernel(x_ref, g_ref, w_ref, o_ref, xp_ref, *, ts):
    @pl.when(pl.program_id(1) == 0)
    def _():
        p = _perm_matrix(ts, ts // SUBLANES, False)
        for s in range(x_ref.shape[0] // ts):
            rows = slice(s * ts, (s + 1) * ts)
            xn = _rms(x_ref[rows, :], g_ref[...]).astype(BF16)
            xp_ref[rows, :] = jnp.dot(p, xn, preferred_element_type=F32).astype(BF16)

    o_ref[...] = jnp.dot(xp_ref[...], w_ref[...], preferred_element_type=F32)


def _lru_rec_kernel(gate_ref, rec_ref, x_ref, cw_ref, cb_ref, wg_ref, ba_ref, bx_ref, lam_ref,
                    wout_ref, o_ref, ext_ref, a_ref, b_ref, tail_ref, hcar_ref, *, chunk):
    ts, c = rec_ref.shape
    seg_len = ts // SUBLANES
    halo = (CONV_WIDTH - 1) * SUBLANES
    nblk, bw, _ = wg_ref.shape

    @pl.when(pl.program_id(0) == 0)
    def _():
        tail_ref[...] = jnp.zeros_like(tail_ref)
        hcar_ref[...] = jnp.zeros_like(hcar_ref)

    tail = rec_ref[ts - halo:ts, :]
    sub = _sublane_id((halo, c))
    ext_ref[0:halo, :] = jnp.where(sub == 0, pltpu.roll(tail_ref[...], halo - (SUBLANES - 1), 0),
                                   pltpu.roll(tail, 1, 0))
    ext_ref[halo:halo + ts, :] = rec_ref[...]
    tail_ref[...] = tail

    z = -lam_ref[...]
    softplus = jnp.maximum(z, 0.0) + jnp.log1p(jnp.exp(-jnp.abs(z)))

    for blk in range(nblk):
        cs = slice(blk * bw, (blk + 1) * bw)
        conv = cb_ref[:, cs] + cw_ref[0:1, cs] * ext_ref[0:ts, cs]
        for k in range(1, CONV_WIDTH):
            conv = conv + cw_ref[k:k + 1, cs] * ext_ref[k * SUBLANES:k * SUBLANES + ts, cs]
        g = jnp.dot(conv.astype(BF16), wg_ref[blk], preferred_element_type=F32)
        r = jax.nn.sigmoid(g[:, :bw] + ba_ref[:, cs])
        ig = jax.nn.sigmoid(g[:, bw:] + bx_ref[:, cs])
        log_a = (-LRU_C * r) * softplus[:, cs]
        a_ref[:, cs] = jnp.exp(log_a)
        th = jnp.tanh(log_a)
        p, s = -2.0 * th, 1.0 - th
        mult = jnp.where(p > 0.0, p * lax.rsqrt(p * s), 0.0)
        b_ref[:, cs] = mult * (ig * conv)

    sub8 = _sublane_id((SUBLANES, chunk))
    for ch in range(c // chunk):
        cs = slice(ch * chunk, (ch + 1) * chunk)

        def step(j, carry, cs=cs):
            h, pp = carry
            rows = pl.ds(pl.multiple_of(j * SUBLANES, SUBLANES), SUBLANES)
            a = a_ref[rows, cs]
            h = a * h + b_ref[rows, cs]
            pp = a * pp
            b_ref[rows, cs] = h
            a_ref[rows, cs] = pp
            return h, pp

        hc, pc = lax.fori_loop(0, seg_len, step,
                               (jnp.zeros((SUBLANES, chunk), F32), jnp.ones((SUBLANES, chunk), F32)),
                               unroll=SCAN_UNROLL)
        for d in (1, 2, 4):
            keep = sub8 >= d
            hs = jnp.where(keep, pltpu.roll(hc, d, 0), 0.0)
            ps = jnp.where(keep, pltpu.roll(pc, d, 0), 1.0)
            hc = pc * hs + hc
            pc = pc * ps
        cin = hcar_ref[:, cs]
        end = hc + pc * cin
        seg_in = jnp.where(sub8 >= 1, pltpu.roll(end, 1, 0), cin)
        hcar_ref[:, cs] = end[SUBLANES - 1:SUBLANES, :]

        def fix(j, carry, cs=cs, seg_in=seg_in):
            rows = pl.ds(pl.multiple_of(j * SUBLANES, SUBLANES), SUBLANES)
            b_ref[rows, cs] = b_ref[rows, cs] + a_ref[rows, cs] * seg_in
            return carry

        lax.fori_loop(0, seg_len, fix, 0, unroll=SCAN_UNROLL)

    y = (b_ref[...] * _gelu_tanh(gate_ref[...])).astype(BF16)
    y = jnp.dot(_perm_matrix(ts, seg_len, True), y, preferred_element_type=F32).astype(BF16)
    o_ref[...] = x_ref[...] + jnp.dot(y, wout_ref[...], preferred_element_type=F32)


def _block_diag(w, per_block):
    h, d, _ = w.shape
    nb = h // per_block
    eye = jnp.eye(per_block, dtype=w.dtype)
    m = jnp.einsum('bhij,hk->bhikj', w.reshape(nb, per_block, d, d), eye)
    return m.reshape(nb, per_block * d, per_block * d)


def _rglru_layer(x, norm_g, w_in, conv_w, conv_b, wa, ba, wx, bx, lru_l, w_out):
    t, d = x.shape
    c = w_out.shape[0]
    hd = wa.shape[1]
    per_block = LANES // math.gcd(hd, LANES)
    ts = SCAN_TILE
    tm = min(IN_TILE_M, t)
    tn = IN_TILE_N

    proj = pl.pallas_call(
        functools.partial(_lru_in_kernel, ts=ts),
        grid=(t // tm, (2 * c) // tn),
        in_specs=[pl.BlockSpec((tm, d), lambda i, j: (i, 0)),
                  pl.BlockSpec((1, d), lambda i, j: (0, 0)),
                  pl.BlockSpec((d, tn), lambda i, j: (0, j))],
        out_specs=pl.BlockSpec((tm, tn), lambda i, j: (i, j)),
        out_shape=jax.ShapeDtypeStruct((t, 2 * c), F32),
        scratch_shapes=[pltpu.VMEM((tm, d), BF16)],
        compiler_params=_cparams(("parallel", "arbitrary")),
        name="lru_in",
    )(x, norm_g.reshape(1, d), w_in.astype(BF16))

    wg = jnp.concatenate([_block_diag(wa, per_block), _block_diag(wx, per_block)], axis=-1).astype(BF16)
    nblk, bw, _ = wg.shape
    halo = (CONV_WIDTH - 1) * SUBLANES
    row = lambda v: v.reshape(1, c)
    return pl.pallas_call(
        functools.partial(_lru_rec_kernel, chunk=bw * 2),
        grid=(t // ts,),
        in_specs=[pl.BlockSpec((ts, c), lambda i: (i, 0)),
                  pl.BlockSpec((ts, c), lambda i: (i, 1)),
                  pl.BlockSpec((ts, d), lambda i: (i, 0)),
                  _const_spec((CONV_WIDTH, c)), _const_spec((1, c)),
                  _const_spec((nblk, bw, 2 * bw)),
                  _const_spec((1, c)), _const_spec((1, c)), _const_spec((1, c)),
                  _const_spec((c, d))],
        out_specs=pl.BlockSpec((ts, d), lambda i: (i, 0)),
        out_shape=jax.ShapeDtypeStruct((t, d), F32),
        scratch_shapes=[pltpu.VMEM((ts + halo, c), F32), pltpu.VMEM((ts, c), F32),
                        pltpu.VMEM((ts, c), F32), pltpu.VMEM((halo, c), F32),
                        pltpu.VMEM((1, c), F32)],
        compiler_params=_cparams(("arbitrary",)),
        name="lru_rec",
    )(proj, proj, x, conv_w, row(conv_b), wg, row(ba), row(bx), row(lru_l), w_out.astype(BF16))


def _s5_disc_kernel(lre_ref, lim_ref, dt_ref, bre_ref, bim_ref, abr_ref, abi_ref, bb_ref):
    lre = jnp.minimum(lre_ref[...], -1e-4)
    lim = lim_ref[...]
    dt = jnp.exp(dt_ref[...])
    mag = jnp.exp(lre * dt)
    abr = mag * jnp.cos(lim * dt)
    abi = mag * jnp.sin(lim * dt)
    den = lre * lre + lim * lim
    kr = ((abr - 1.0) * lre + abi * lim) / den
    ki = (abi * lre - (abr - 1.0) * lim) / den
    abr_ref[...] = abr
    abi_ref[...] = abi
    bb_ref[0] = kr * bre_ref[...] - ki * bim_ref[...]
    bb_ref[1] = kr * bim_ref[...] + ki * bre_ref[...]


def _s5_kernel(dest_ref, x_ref, slab_ref, eo_ref, g_ref, win_ref, bin_ref, cout_ref,
               d_ref, abr_ref, abi_ref, wglu_ref, o_ref, u_ref, z_ref, xr_ref, xi_ref, a8r_ref,
               a8i_ref, asr_ref, asi_ref, carr_ref, cari_ref, buf_ref, sem, *, chunk):
    ts, d_model = x_ref.shape
    tile = pl.program_id(0)
    last = pl.num_programs(0) - 1
    slot = tile & 1
    seg_len = ts // SUBLANES
    ns = xr_ref.shape[1]
    _, nb_in, kin, nin = bin_ref.shape
    _, nb_out, kout, nout = cout_ref.shape

    @pl.when(tile == 0)
    def _():
        _combine_start(dest_ref, eo_ref, buf_ref, sem, 0, 0)
        carr_ref[...] = jnp.zeros_like(carr_ref)
        cari_ref[...] = jnp.zeros_like(cari_ref)
        ar, ai = abr_ref[...], abi_ref[...]
        a8r_ref[...] = jnp.broadcast_to(ar, a8r_ref.shape)
        a8i_ref[...] = jnp.broadcast_to(ai, a8i_ref.shape)

        def power(j, cur):
            pr, pi = cur
            return pr * ar - pi * ai, pr * ai + pi * ar

        lr, li = lax.fori_loop(0, seg_len - 1, power, (ar, ai))

        def seg_powers(s, cur):
            pr, pi = cur
            asr_ref[pl.ds(s, 1), :] = pr
            asi_ref[pl.ds(s, 1), :] = pi
            return pr * lr - pi * li, pr * li + pi * lr

        lax.fori_loop(0, SUBLANES, seg_powers, (lr, li))

    _combine_wait(eo_ref, buf_ref, sem, slot)
    x = _combined_tile(x_ref[...], slab_ref[...], buf_ref, slot)
    o_ref[...] = x
    _combine_start(dest_ref, eo_ref, buf_ref, sem, jnp.minimum(tile + 1, last), 1 - slot)
    xn = _rms(x, g_ref[...]).astype(BF16)
    xp = jnp.dot(_perm_matrix(ts, seg_len, False), xn, preferred_element_type=F32).astype(BF16)
    u = jnp.dot(xp, win_ref[...], preferred_element_type=F32)
    u_ref[...] = u
    ub = u.astype(BF16)
    for blk in range(nb_in):
        uc = ub[:, blk * kin:(blk + 1) * kin]
        xr_ref[:, blk * nin:(blk + 1) * nin] = jnp.dot(uc, bin_ref[0, blk], preferred_element_type=F32)
        xi_ref[:, blk * nin:(blk + 1) * nin] = jnp.dot(uc, bin_ref[1, blk], preferred_element_type=F32)

    sub8 = _sublane_id((SUBLANES, chunk))
    for ch in range(ns // chunk):
        cs = slice(ch * chunk, (ch + 1) * chunk)
        ar, ai = a8r_ref[:, cs], a8i_ref[:, cs]

        def step(j, carry, cs=cs, ar=ar, ai=ai):
            sr, si = carry
            rows = pl.ds(pl.multiple_of(j * SUBLANES, SUBLANES), SUBLANES)
            nr = ar * sr - ai * si + xr_ref[rows, cs]
            ni = ar * si + ai * sr + xi_ref[rows, cs]
            xr_ref[rows, cs] = nr
            xi_ref[rows, cs] = ni
            return nr, ni

        zero = jnp.zeros((SUBLANES, chunk), F32)
        er, ei = lax.fori_loop(0, seg_len, step, (zero, zero), unroll=SCAN_UNROLL)
        for d in (1, 2, 4):
            keep = sub8 >= d
            pr, pi = asr_ref[d - 1:d, cs], asi_ref[d - 1:d, cs]
            sr = jnp.where(keep, pltpu.roll(er, d, 0), 0.0)
            si = jnp.where(keep, pltpu.roll(ei, d, 0), 0.0)
            er, ei = er + pr * sr - pi * si, ei + pr * si + pi * sr
        cr, ci = carr_ref[:, cs], cari_ref[:, cs]
        pr, pi = asr_ref[:, cs], asi_ref[:, cs]
        er, ei = er + pr * cr - pi * ci, ei + pr * ci + pi * cr
        inr = jnp.where(sub8 >= 1, pltpu.roll(er, 1, 0), cr)
        ini = jnp.where(sub8 >= 1, pltpu.roll(ei, 1, 0), ci)
        carr_ref[:, cs] = er[SUBLANES - 1:SUBLANES, :]
        cari_ref[:, cs] = ei[SUBLANES - 1:SUBLANES, :]

        def fix(j, carry, cs=cs, ar=ar, ai=ai):
            fr, fi = carry
            fr, fi = ar * fr - ai * fi, ar * fi + ai * fr
            rows = pl.ds(pl.multiple_of(j * SUBLANES, SUBLANES), SUBLANES)
            xr_ref[rows, cs] = xr_ref[rows, cs] + fr
            xi_ref[rows, cs] = xi_ref[rows, cs] + fi
            return fr, fi

        lax.fori_loop(0, seg_len, fix, (inr, ini), unroll=SCAN_UNROLL)

    for blk in range(nb_out):
        ks = slice(blk * kout, (blk + 1) * kout)
        ys = slice(blk * nout, (blk + 1) * nout)
        y = jnp.dot(xr_ref[:, ks].astype(BF16), cout_ref[0, blk], preferred_element_type=F32)
        y = y + jnp.dot(xi_ref[:, ks].astype(BF16), cout_ref[1, blk], preferred_element_type=F32)
        y = y + d_ref[:, ys] * u_ref[:, ys]
        z_ref[:, ys] = _gelu_tanh(y).astype(BF16)

    zn = jnp.dot(_perm_matrix(ts, seg_len, True), z_ref[...], preferred_element_type=F32).astype(BF16)
    vg = jnp.dot(zn, wglu_ref[...], preferred_element_type=F32)
    o_ref[...] = o_ref[...] + vg[:, :d_model] * jax.nn.sigmoid(vg[:, d_model:])

    @pl.when(tile == last)
    def _():
        _combine_wait(eo_ref, buf_ref, sem, 1 - slot)


def _s5_layer(pending, norm_g, w_in, lam_re, lam_im, log_dt, b_re, b_im, c_re, c_im, d_skip, w_glu):
    x, slab, dest, eo = pending
    t, d = x.shape
    groups, p, k = b_re.shape
    d_ssm = groups * k
    ns = groups * p
    ts = SCAN_TILE
    seg_len = ts // SUBLANES

    lane = lambda v: v.reshape(1, ns).astype(F32)
    rows_k = lambda b: jnp.transpose(b, (2, 0, 1)).reshape(k, ns).astype(F32)
    abr, abi, bb = pl.pallas_call(
        _s5_disc_kernel,
        in_specs=[_const_spec((1, ns))] * 3 + [_const_spec((k, ns))] * 2,
        out_specs=[pl.BlockSpec((1, ns), lambda i: (0, 0))] * 2 + [pl.BlockSpec((2, k, ns), lambda i: (0, 0, 0))],
        out_shape=[jax.ShapeDtypeStruct((1, ns), F32)] * 2 + [jax.ShapeDtypeStruct((2, k, ns), F32)],
        grid=(1,),
        compiler_params=_cparams(("arbitrary",)),
        name="s5_disc",
    )(lane(lam_re), lane(lam_im), lane(jnp.repeat(log_dt, p)), rows_k(b_re), rows_k(b_im))

    gb = MXU_DIM // k
    nb = groups // gb
    eye = jnp.eye(gb, dtype=F32)

    def in_blocks(b2):
        m = jnp.einsum('rkcgp,gh->rcgkhp', b2.reshape(2, k, nb, gb, p), eye)
        return m.reshape(2, nb, gb * k, gb * p).astype(BF16)

    def out_blocks(c2):
        m = jnp.einsum('rcgkp,gh->rcgphk', c2.reshape(2, nb, gb, k, p).astype(F32), eye)
        return m.reshape(2, nb, gb * p, gb * k).astype(BF16)

    chunk = 512
    return pl.pallas_call(
        functools.partial(_s5_kernel, chunk=chunk),
        grid_spec=pltpu.PrefetchScalarGridSpec(
            num_scalar_prefetch=1, grid=(t // ts,),
            in_specs=[pl.BlockSpec((ts, d), lambda i, dst: (i, 0)),
                      pl.BlockSpec((ts, ROUTE_LANES), lambda i, dst: (i, 0)),
                      pl.BlockSpec(memory_space=pl.ANY),
                      _const_spec((1, d)), _const_spec((d, d_ssm)),
                      _const_spec((2, nb, gb * k, gb * p)), _const_spec((2, nb, gb * p, gb * k)),
                      _const_spec((1, d_ssm)), _const_spec((1, ns)), _const_spec((1, ns)),
                      _const_spec((d_ssm, 2 * d))],
            out_specs=pl.BlockSpec((ts, d), lambda i, dst: (i, 0)),
            scratch_shapes=[pltpu.VMEM((ts, d_ssm), F32), pltpu.VMEM((ts, d_ssm), BF16),
                            pltpu.VMEM((ts, ns), F32), pltpu.VMEM((ts, ns), F32),
                            pltpu.VMEM((SUBLANES, ns), F32), pltpu.VMEM((SUBLANES, ns), F32),
                            pltpu.VMEM((SUBLANES, ns), F32), pltpu.VMEM((SUBLANES, ns), F32),
                            pltpu.VMEM((1, ns), F32), pltpu.VMEM((1, ns), F32),
                            pltpu.VMEM((2, TOP_K, ts, d), F32), pltpu.SemaphoreType.DMA((2, TOP_K))]),
        out_shape=jax.ShapeDtypeStruct((t, d), F32),
        compiler_params=_cparams(("arbitrary",)),
        name="s5_mix",
    )(dest, x, slab, eo, norm_g.reshape(1, d), w_in.astype(BF16), in_blocks(bb),
      out_blocks(jnp.stack([c_re, -c_im])), d_skip.reshape(1, d_ssm).astype(F32),
      abr, abi, w_glu.astype(BF16))


def _route_kernel(x_ref, g_ref, wr_ref, br_ref, slab_ref, er_ref, cnt_ref, base_ref, *, n_groups, epg):
    @pl.when(pl.program_id(0) == 0)
    def _():
        base_ref[...] = jnp.zeros_like(base_ref)

    tr = x_ref.shape[0]
    xn = _rms(x_ref[...], g_ref[...])
    x_hi = xn.astype(BF16)
    x_lo = (xn - x_hi.astype(F32)).astype(BF16)
    hi = jnp.dot(x_hi, wr_ref[...], preferred_element_type=F32)
    lo = jnp.dot(x_lo, wr_ref[:, :ROUTE_LANES], preferred_element_type=F32)
    logits = hi[:, :ROUTE_LANES] + (hi[:, ROUTE_LANES:] + lo) + br_ref[...]
    lane = lax.broadcasted_iota(I32, logits.shape, 1)
    neg = jnp.float32(-3.0e38)

    def first_argmax(v):
        m = jnp.max(v, axis=1, keepdims=True)
        return m, jnp.min(jnp.where(v == m, lane, ROUTE_LANES), axis=1, keepdims=True)

    gmask = lane < n_groups
    gmax, gidx = first_argmax(jnp.where(gmask, logits, neg))
    p_group = 1.0 / jnp.sum(jnp.where(gmask, jnp.exp(logits - gmax), 0.0), axis=1, keepdims=True)
    lo = n_groups + gidx * epg
    el = jnp.where((lane >= lo) & (lane < lo + epg), logits, neg)
    m1, i1 = first_argmax(el)
    m2, i2 = first_argmax(jnp.where(lane == i1, neg, el))
    t2 = jnp.exp(m2 - m1)
    w1 = p_group / (1.0 + t2)
    w2 = p_group * t2 / (1.0 + t2)

    hit1, hit2 = lane == i1, lane == i2
    onehot = (hit1 | hit2).astype(BF16)
    r = lax.broadcasted_iota(I32, (tr, tr), 0)
    c = lax.broadcasted_iota(I32, (tr, tr), 1)
    before = jnp.dot((c < r).astype(BF16), onehot, preferred_element_type=F32) + base_ref[0:1, :]
    rank1 = jnp.sum(jnp.where(hit1, before, 0.0), axis=1, keepdims=True)
    rank2 = jnp.sum(jnp.where(hit2, before, 0.0), axis=1, keepdims=True)
    total = base_ref[0:1, :] + jnp.sum(onehot.astype(F32), axis=0, keepdims=True)
    base_ref[...] = jnp.broadcast_to(total, base_ref.shape)
    cnt_ref[...] = jnp.broadcast_to(total, cnt_ref.shape)

    vals = (w1, w2, (i1 - n_groups).astype(F32), (i2 - n_groups).astype(F32), rank1, rank2)
    slab = jnp.zeros(logits.shape, F32)
    for k, v in enumerate(vals):
        slab = jnp.where(lane == k, v, slab)
    slab_ref[...] = slab
    er_ref[...] = jnp.transpose(slab)[0:SUBLANES, :]


def _row_gather_start(src_ref, dst_ref, sem, index_of, rows):
    for r in range(rows):
        pltpu.make_async_copy(src_ref.at[pl.ds(index_of(r), 1)], dst_ref.at[pl.ds(r, 1)], sem).start()


def _row_gather_wait(src_ref, dst_ref, sem):
    pltpu.make_async_copy(src_ref.at[pl.ds(0, dst_ref.shape[0])], dst_ref, sem).wait()


def _gmm_kernel(exp_ref, nexp_ref, wslot_ref, padlo_ref, padhi_ref, meta_ref, dest_ref, x_ref, g_ref,
                w1_ref, w3_ref, w2_ref, o_ref, xbuf_ref, w1f_ref, w3f_ref, w2f_ref,
                w1b_ref, w3b_ref, w2b_ref, tok_ref, sem, wsem, *, layer):
    n = pl.program_id(0)
    n_real = meta_ref[0]
    prev = jnp.maximum(n - 1, 0)
    nslots = xbuf_ref.shape[0]
    slot = lax.rem(n, nslots)
    rows = xbuf_ref.shape[1]
    new_expert = (n == 0) | (exp_ref[n] != exp_ref[prev])

    def gather(block, into):
        base = block * rows
        _row_gather_start(x_ref, xbuf_ref.at[into], sem.at[into], lambda r: tok_ref[base + r], rows)

    def weight_copies(expert, into):
        return [pltpu.make_async_copy(src.at[layer, expert], dst.at[into], wsem.at[into, j])
                for j, (src, dst) in enumerate(((w1_ref, w1f_ref), (w3_ref, w3f_ref), (w2_ref, w2f_ref)))]

    @pl.when(n == 0)
    def _():
        for cp in weight_copies(exp_ref[0], wslot_ref[0]):
            cp.start()

        n_tok = dest_ref.shape[0] // TOP_K

        def pad_expert(e, carry):
            def pad(j, c):
                tok_ref[j] = lax.rem(j, n_tok)
                return c
            return lax.fori_loop(padlo_ref[e], padhi_ref[e], pad, carry)

        lax.fori_loop(0, padlo_ref.shape[0], pad_expert, 0)

        def invert(t, carry):
            for k in range(TOP_K):
                tok_ref[dest_ref[k * n_tok + t]] = t
            return carry

        lax.fori_loop(0, n_tok, invert, 0, unroll=8)
        for ahead in range(GMM_AHEAD):
            gather(ahead, ahead)

    @pl.when(new_expert)
    def _():
        ws = wslot_ref[n]
        for cp in weight_copies(exp_ref[n], ws):
            cp.wait()
        w1b_ref[...] = w1f_ref[ws].astype(BF16)
        w3b_ref[...] = w3f_ref[ws].astype(BF16)
        w2b_ref[...] = w2f_ref[ws].astype(BF16)

        @pl.when(nexp_ref[n] != exp_ref[n])
        def _():
            for cp in weight_copies(nexp_ref[n], 1 - ws):
                cp.start()

    @pl.when(n < n_real)
    def _():
        _row_gather_wait(x_ref, xbuf_ref.at[slot], sem.at[slot])
        xb = _rms(xbuf_ref[slot], g_ref[...]).astype(BF16)
        gather(jnp.minimum(n + GMM_AHEAD, n_real - 1), lax.rem(n + GMM_AHEAD, nslots))
        h = jax.nn.silu(jnp.dot(xb, w1b_ref[...], preferred_element_type=F32))
        h = (h * jnp.dot(xb, w3b_ref[...], preferred_element_type=F32)).astype(BF16)
        o_ref[...] = jnp.dot(h, w2b_ref[...], preferred_element_type=F32)

    @pl.when(n >= n_real)
    def _():
        o_ref[...] = jnp.zeros_like(o_ref)

    @pl.when(n == n_real - 1)
    def _():
        for ahead in range(1, GMM_AHEAD + 1):
            other = lax.rem(n + ahead, nslots)
            _row_gather_wait(x_ref, xbuf_ref.at[other], sem.at[other])


def _combine_start(dest_ref, eo_ref, buf_ref, sem, tile, into):
    rows = buf_ref.shape[2]
    n_tok = dest_ref.shape[0] // TOP_K
    for k in range(TOP_K):
        base = k * n_tok + tile * rows
        _row_gather_start(eo_ref, buf_ref.at[into, k], sem.at[into, k],
                          lambda r, base=base: dest_ref[base + r], rows)


def _combine_wait(eo_ref, buf_ref, sem, slot):
    for k in range(TOP_K):
        _row_gather_wait(eo_ref, buf_ref.at[slot, k], sem.at[slot, k])


def _combined_tile(x, slab, buf_ref, slot):
    return x + slab[:, 0:1] * buf_ref[slot, 0] + slab[:, 1:2] * buf_ref[slot, 1]


def _combine_kernel(dest_ref, x_ref, slab_ref, g_ref, eo_ref, o_ref, buf_ref, sem):
    i = pl.program_id(0)
    last = pl.num_programs(0) - 1
    nslots = buf_ref.shape[0]
    slot = lax.rem(i, nslots)

    @pl.when(i == 0)
    def _():
        for ahead in range(COMBINE_AHEAD):
            _combine_start(dest_ref, eo_ref, buf_ref, sem, jnp.minimum(ahead, last), ahead)

    _combine_wait(eo_ref, buf_ref, sem, slot)
    y = _combined_tile(x_ref[...], slab_ref[...], buf_ref, slot)
    _combine_start(dest_ref, eo_ref, buf_ref, sem, jnp.minimum(i + COMBINE_AHEAD, last),
                   lax.rem(i + COMBINE_AHEAD, nslots))
    o_ref[...] = _rms(y, g_ref[...])

    @pl.when(i == last)
    def _():
        for ahead in range(1, COMBINE_AHEAD + 1):
            _combine_wait(eo_ref, buf_ref, sem, lax.rem(i + ahead, nslots))


def _gmm_schedule(counts, n_rows, block):
    n_exp = counts.shape[0]
    n_items = n_rows // block + n_exp
    per_exp = (counts + block - 1) // block
    blk_end = jnp.cumsum(per_exp)
    starts = (blk_end - per_exp) * block
    n_real = blk_end[-1]
    n = jnp.arange(n_items, dtype=I32)
    ids = jnp.arange(n_exp, dtype=I32)
    last_used = jnp.max(jnp.where(per_exp > 0, ids, 0))
    e = jnp.minimum(jnp.sum(blk_end[None, :] <= n[:, None], axis=1), last_used).astype(I32)
    later = jnp.where((per_exp > 0)[None, :] & (ids[None, :] > e[:, None]), ids[None, :], n_exp)
    nxt = jnp.min(later, axis=1)
    nxt = jnp.where(nxt == n_exp, e, nxt).astype(I32)
    wslot = (jnp.sum(((per_exp > 0)[None, :] & (ids[None, :] < e[:, None])).astype(I32), axis=1) & 1)
    pad_lo = (starts + counts).astype(I32)
    pad_hi = (blk_end * block).astype(I32)
    meta = jnp.stack([n_real]).astype(I32)
    return starts, (e, nxt, wslot.astype(I32), pad_lo, pad_hi, meta)


def _moe_layer(x, norm_g, wg, bg, we, be, w1, w3, w2, layer):
    t, d = x.shape
    n_groups, _, epg = we.shape
    n_exp = n_groups * epg
    f = w1.shape[-1]
    tr = min(ROUTER_TILE, t)
    n_rows = t * TOP_K
    g_row = norm_g.reshape(1, d)

    wr = jnp.concatenate([wg, jnp.transpose(we, (1, 0, 2)).reshape(d, n_exp)], axis=1)
    wr = jnp.pad(wr, ((0, 0), (0, ROUTE_LANES - wr.shape[1]))).astype(F32)
    wr_hi = wr.astype(BF16)
    wr = jnp.concatenate([wr_hi, (wr - wr_hi.astype(F32)).astype(BF16)], axis=1)
    br = jnp.pad(jnp.concatenate([bg, be.reshape(-1)]), (0, ROUTE_LANES - n_groups - n_exp))

    slab, er, cnt = pl.pallas_call(
        functools.partial(_route_kernel, n_groups=n_groups, epg=epg),
        grid=(t // tr,),
        in_specs=[pl.BlockSpec((tr, d), lambda i: (i, 0)), _const_spec((1, d)),
                  _const_spec((d, 2 * ROUTE_LANES)), _const_spec((1, ROUTE_LANES))],
        out_specs=[pl.BlockSpec((tr, ROUTE_LANES), lambda i: (i, 0)),
                   pl.BlockSpec((SUBLANES, tr), lambda i: (0, i)),
                   pl.BlockSpec((SUBLANES, ROUTE_LANES), lambda i: (0, 0))],
        out_shape=[jax.ShapeDtypeStruct((t, ROUTE_LANES), F32),
                   jax.ShapeDtypeStruct((SUBLANES, t), F32),
                   jax.ShapeDtypeStruct((SUBLANES, ROUTE_LANES), F32)],
        scratch_shapes=[pltpu.VMEM((SUBLANES, ROUTE_LANES), F32)],
        compiler_params=_cparams(("arbitrary",)),
        name="moe_route",
    )(x, g_row, wr, br.reshape(1, ROUTE_LANES).astype(F32))

    counts = cnt[0, n_groups:n_groups + n_exp].astype(I32)
    starts, items = _gmm_schedule(counts, n_rows, GMM_BLOCK)
    e_idx = er[2:2 + TOP_K].astype(I32)
    start_of = jnp.sum(jnp.where(e_idx[..., None] == jnp.arange(n_exp, dtype=I32), starts, 0), axis=-1)
    dest = (start_of + er[2 + TOP_K:2 + 2 * TOP_K].astype(I32)).reshape(-1)

    n_items = items[0].shape[0]
    any_spec = pl.BlockSpec(memory_space=pl.ANY)
    eo = pl.pallas_call(
        functools.partial(_gmm_kernel, layer=layer),
        grid_spec=pltpu.PrefetchScalarGridSpec(
            num_scalar_prefetch=len(items) + 1, grid=(n_items,),
            in_specs=[any_spec, pl.BlockSpec((1, d), lambda n, *_: (0, 0)),
                      any_spec, any_spec, any_spec],
            out_specs=pl.BlockSpec((GMM_BLOCK, d), lambda n, *_: (n, 0)),
            scratch_shapes=[pltpu.VMEM((GMM_AHEAD + 1, GMM_BLOCK, d), F32),
                            pltpu.VMEM((2, d, f), F32), pltpu.VMEM((2, d, f), F32),
                            pltpu.VMEM((2, f, d), F32),
                            pltpu.VMEM((d, f), BF16), pltpu.VMEM((d, f), BF16), pltpu.VMEM((f, d), BF16),
                            pltpu.SMEM((n_items * GMM_BLOCK,), I32),
                            pltpu.SemaphoreType.DMA((GMM_AHEAD + 1,)),
                            pltpu.SemaphoreType.DMA((2, 3))]),
        out_shape=jax.ShapeDtypeStruct((n_items * GMM_BLOCK, d), F32),
        compiler_params=_cparams(("arbitrary",)),
        name="moe_gmm",
    )(*items, dest, x, g_row, w1, w3, w2)
    return x, slab, dest, eo


def _moe_combine_norm(pending, final_g):
    x, slab, dest, eo = pending
    t, d = x.shape
    tr = ROUTE_TILE
    return pl.pallas_call(
        _combine_kernel,
        grid_spec=pltpu.PrefetchScalarGridSpec(
            num_scalar_prefetch=1, grid=(t // tr,),
            in_specs=[pl.BlockSpec((tr, d), lambda i, dst: (i, 0)),
                      pl.BlockSpec((tr, ROUTE_LANES), lambda i, dst: (i, 0)),
                      pl.BlockSpec((1, d), lambda i, dst: (0, 0)),
                      pl.BlockSpec(memory_space=pl.ANY)],
            out_specs=pl.BlockSpec((tr, d), lambda i, dst: (i, 0)),
            scratch_shapes=[pltpu.VMEM((COMBINE_AHEAD + 1, TOP_K, tr, d), F32),
                            pltpu.SemaphoreType.DMA((COMBINE_AHEAD + 1, TOP_K))]),
        out_shape=jax.ShapeDtypeStruct((t, d), F32),
        compiler_params=_cparams(("arbitrary",)),
        name="moe_combine",
    )(dest, x, slab, final_g.reshape(1, d), eo)


def kernel(x, mix_norm, ffn_norm, final_norm, a_w_in, a_conv_w, a_conv_b, a_gate_a_w, a_gate_a_b, a_gate_x_w, a_gate_x_b, a_lru_L, a_w_out, b_w_in, b_lam_re, b_lam_im, b_log_dt, b_b_re, b_b_im, b_c_re, b_c_im, b_d, b_w_glu, moe_wg, moe_bg, moe_we, moe_be, moe_w1, moe_w3, moe_w2):
    batch, seq, d = x.shape
    depth = mix_norm.shape[0]
    assert depth == 2, "the combine of each MoE layer is fused into the S5 mixer / final norm"
    outs = []
    for b in range(batch):
        h = _rglru_layer(x[b], mix_norm[0], a_w_in[0], a_conv_w[0], a_conv_b[0], a_gate_a_w[0],
                         a_gate_a_b[0], a_gate_x_w[0], a_gate_x_b[0], a_lru_L[0], a_w_out[0])
        pending = _moe_layer(h, ffn_norm[0], moe_wg[0], moe_bg[0], moe_we[0], moe_be[0],
                             moe_w1, moe_w3, moe_w2, 0)
        h = _s5_layer(pending, mix_norm[1], b_w_in[0], b_lam_re[0], b_lam_im[0], b_log_dt[0],
                      b_b_re[0], b_b_im[0], b_c_re[0], b_c_im[0], b_d[0], b_w_glu[0])
        pending = _moe_layer(h, ffn_norm[1], moe_wg[1], moe_bg[1], moe_we[1], moe_be[1],
                             moe_w1, moe_w3, moe_w2, 1)
        outs.append(_moe_combine_norm(pending, final_norm))
    return jnp.stack(outs)
```

```python
import functools
import math

import jax
import jax.numpy as jnp
from jax import lax
from jax.experimental import pallas as pl
from jax.experimental.pallas import tpu as pltpu

F32 = jnp.float32
BF16 = jnp.bfloat16
I32 = jnp.int32

EPS = 1e-6
LRU_C = 8.0
CONV_WIDTH = 4
TOP_K = 2

LANES = 128
SUBLANES = 8
MXU_DIM = 256
VMEM_LIMIT = 56 * 1024 * 1024

SCAN_TILE = 256
IN_TILE_M = 1024
IN_TILE_N = 1280
ROUTER_TILE = 512
ROUTE_TILE = 256
GMM_BLOCK = 256
GMM_AHEAD = 2
COMBINE_AHEAD = 2
SCAN_UNROLL = True
ROUTE_LANES = 128


def _cparams(sem):
    return pltpu.CompilerParams(dimension_semantics=sem, vmem_limit_bytes=VMEM_LIMIT)


def _const_spec(shape):
    nd = len(shape)
    return pl.BlockSpec(shape, lambda *_: (0,) * nd, pipeline_mode=pl.Buffered(1))


def _rms(x, g):
    ms = jnp.mean(x * x, axis=-1, keepdims=True)
    return x * lax.rsqrt(ms + EPS) * g


_GELU_C1 = math.sqrt(2.0 / math.pi)
_GELU_C2 = _GELU_C1 * 0.044715


def _gelu_tanh(x):
    hx = 0.5 * x
    return hx + hx * jnp.tanh(x * (_GELU_C1 + _GELU_C2 * (x * x)))


def _perm_matrix(ts, seg_len, inverse):
    r = lax.broadcasted_iota(I32, (ts, ts), 1 if inverse else 0)
    n = lax.broadcasted_iota(I32, (ts, ts), 0 if inverse else 1)
    return (n == (r % SUBLANES) * seg_len + r // SUBLANES).astype(BF16)


def _sublane_id(shape):
    return lax.broadcasted_iota(I32, shape, 0) & (SUBLANES - 1)


def _lru_in_kernel(x_ref, g_ref, w_ref, o_ref, xp_ref, *, ts):
    @pl.when(pl.program_id(1) == 0)
    def _():
        p = _perm_matrix(ts, ts // SUBLANES, False)
        for s in range(x_ref.shape[0] // ts):
            rows = slice(s * ts, (s + 1) * ts)
            xn = _rms(x_ref[rows, :], g_ref[...]).astype(BF16)
            xp_ref[rows, :] = jnp.dot(p, xn, preferred_element_type=F32).astype(BF16)

    o_ref[...] = jnp.dot(xp_ref[...], w_ref[...], preferred_element_type=F32)


def _lru_rec_kernel(gate_ref, rec_ref, x_ref, cw_ref, cb_ref, wg_ref, ba_ref, bx_ref, lam_ref,
                    wout_ref, o_ref, ext_ref, a_ref, b_ref, tail_ref, hcar_ref, *, chunk):
    ts, c = rec_ref.shape
    seg_len = ts // SUBLANES
    halo = (CONV_WIDTH - 1) * SUBLANES
    nblk, bw, _ = wg_ref.shape

    @pl.when(pl.program_id(0) == 0)
    def _():
        tail_ref[...] = jnp.zeros_like(tail_ref)
        hcar_ref[...] = jnp.zeros_like(hcar_ref)

    tail = rec_ref[ts - halo:ts, :]
    sub = _sublane_id((halo, c))
    ext_ref[0:halo, :] = jnp.where(sub == 0, pltpu.roll(tail_ref[...], halo - (SUBLANES - 1), 0),
                                   pltpu.roll(tail, 1, 0))
    ext_ref[halo:halo + ts, :] = rec_ref[...]
    tail_ref[...] = tail

    z = -lam_ref[...]
    softplus = jnp.maximum(z, 0.0) + jnp.log1p(jnp.exp(-jnp.abs(z)))

    for blk in range(nblk):
        cs = slice(blk * bw, (blk + 1) * bw)
        conv = cb_ref[:, cs] + cw_ref[0:1, cs] * ext_ref[0:ts, cs]
        for k in range(1, CONV_WIDTH):
            conv = conv + cw_ref[k:k + 1, cs] * ext_ref[k * SUBLANES:k * SUBLANES + ts, cs]
        g = jnp.dot(conv.astype(BF16), wg_ref[blk], preferred_element_type=F32)
        r = jax.nn.sigmoid(g[:, :bw] + ba_ref[:, cs])
        ig = jax.nn.sigmoid(g[:, bw:] + bx_ref[:, cs])
        log_a = (-LRU_C * r) * softplus[:, cs]
        a_ref[:, cs] = jnp.exp(log_a)
        th = jnp.tanh(log_a)
        p, s = -2.0 * th, 1.0 - th
        mult = jnp.where(p > 0.0, p * lax.rsqrt(p * s), 0.0)
        b_ref[:, cs] = mult * (ig * conv)

    sub8 = _sublane_id((SUBLANES, chunk))
    for ch in range(c // chunk):
        cs = slice(ch * chunk, (ch + 1) * chunk)

        def step(j, carry, cs=cs):
            h, pp = carry
            rows = pl.ds(pl.multiple_of(j * SUBLANES, SUBLANES), SUBLANES)
            a = a_ref[rows, cs]
            h = a * h + b_ref[rows, cs]
            pp = a * pp
            b_ref[rows, cs] = h
            a_ref[rows, cs] = pp
            return h, pp

        hc, pc = lax.fori_loop(0, seg_len, step,
                               (jnp.zeros((SUBLANES, chunk), F32), jnp.ones((SUBLANES, chunk), F32)),
                               unroll=SCAN_UNROLL)
        for d in (1, 2, 4):
            keep = sub8 >= d
            hs = jnp.where(keep, pltpu.roll(hc, d, 0), 0.0)
            ps = jnp.where(keep, pltpu.roll(pc, d, 0), 1.0)
            hc = pc * hs + hc
            pc = pc * ps
        cin = hcar_ref[:, cs]
        end = hc + pc * cin
        seg_in = jnp.where(sub8 >= 1, pltpu.roll(end, 1, 0), cin)
        hcar_ref[:, cs] = end[SUBLANES - 1:SUBLANES, :]

        def fix(j, carry, cs=cs, seg_in=seg_in):
            rows = pl.ds(pl.multiple_of(j * SUBLANES, SUBLANES), SUBLANES)
            b_ref[rows, cs] = b_ref[rows, cs] + a_ref[rows, cs] * seg_in
            return carry

        lax.fori_loop(0, seg_len, fix, 0, unroll=SCAN_UNROLL)

    y = (b_ref[...] * _gelu_tanh(gate_ref[...])).astype(BF16)
    y = jnp.dot(_perm_matrix(ts, seg_len, True), y, preferred_element_type=F32).astype(BF16)
    o_ref[...] = x_ref[...] + jnp.dot(y, wout_ref[...], preferred_element_type=F32)


def _block_diag(w, per_block):
    h, d, _ = w.shape
    nb = h // per_block
    eye = jnp.eye(per_block, dtype=w.dtype)
    m = jnp.einsum('bhij,hk->bhikj', w.reshape(nb, per_block, d, d), eye)
    return m.reshape(nb, per_block * d, per_block * d)


def _rglru_layer(x, norm_g, w_in, conv_w, conv_b, wa, ba, wx, bx, lru_l, w_out):
    t, d = x.shape
    c = w_out.shape[0]
    hd = wa.shape[1]
    per_block = LANES // math.gcd(hd, LANES)
    ts = SCAN_TILE
    tm = min(IN_TILE_M, t)
    tn = IN_TILE_N

    proj = pl.pallas_call(
        functools.partial(_lru_in_kernel, ts=ts),
        grid=(t // tm, (2 * c) // tn),
        in_specs=[pl.BlockSpec((tm, d), lambda i, j: (i, 0)),
                  pl.BlockSpec((1, d), lambda i, j: (0, 0)),
                  pl.BlockSpec((d, tn), lambda i, j: (0, j))],
        out_specs=pl.BlockSpec((tm, tn), lambda i, j: (i, j)),
        out_shape=jax.ShapeDtypeStruct((t, 2 * c), F32),
        scratch_shapes=[pltpu.VMEM((tm, d), BF16)],
        compiler_params=_cparams(("parallel", "arbitrary")),
        name="lru_in",
    )(x, norm_g.reshape(1, d), w_in.astype(BF16))

    wg = jnp.concatenate([_block_diag(wa, per_block), _block_diag(wx, per_block)], axis=-1).astype(BF16)
    nblk, bw, _ = wg.shape
    halo = (CONV_WIDTH - 1) * SUBLANES
    row = lambda v: v.reshape(1, c)
    return pl.pallas_call(
        functools.partial(_lru_rec_kernel, chunk=bw * 2),
        grid=(t // ts,),
        in_specs=[pl.BlockSpec((ts, c), lambda i: (i, 0)),
                  pl.BlockSpec((ts, c), lambda i: (i, 1)),
                  pl.BlockSpec((ts, d), lambda i: (i, 0)),
                  _const_spec((CONV_WIDTH, c)), _const_spec((1, c)),
                  _const_spec((nblk, bw, 2 * bw)),
                  _const_spec((1, c)), _const_spec((1, c)), _const_spec((1, c)),
                  _const_spec((c, d))],
        out_specs=pl.BlockSpec((ts, d), lambda i: (i, 0)),
        out_shape=jax.ShapeDtypeStruct((t, d), F32),
        scratch_shapes=[pltpu.VMEM((ts + halo, c), F32), pltpu.VMEM((ts, c), F32),
                        pltpu.VMEM((ts, c), F32), pltpu.VMEM((halo, c), F32),
                        pltpu.VMEM((1, c), F32)],
        compiler_params=_cparams(("arbitrary",)),
        name="lru_rec",
    )(proj, proj, x, conv_w, row(conv_b), wg, row(ba), row(bx), row(lru_l), w_out.astype(BF16))


def _s5_disc_kernel(lre_ref, lim_ref, dt_ref, bre_ref, bim_ref, abr_ref, abi_ref, bb_ref):
    lre = jnp.minimum(lre_ref[...], -1e-4)
    lim = lim_ref[...]
    dt = jnp.exp(dt_ref[...])
    mag = jnp.exp(lre * dt)
    abr = mag * jnp.cos(lim * dt)
    abi = mag * jnp.sin(lim * dt)
    den = lre * lre + lim * lim
    kr = ((abr - 1.0) * lre + abi * lim) / den
    ki = (abi * lre - (abr - 1.0) * lim) / den
    abr_ref[...] = abr
    abi_ref[...] = abi
    bb_ref[0] = kr * bre_ref[...] - ki * bim_ref[...]
    bb_ref[1] = kr * bim_ref[...] + ki * bre_ref[...]


def _s5_kernel(dest_ref, x_ref, slab_ref, eo_ref, g_ref, win_ref, bin_ref, cout_ref,
               d_ref, abr_ref, abi_ref, wglu_ref, o_ref, u_ref, z_ref, xr_ref, xi_ref, a8r_ref,
               a8i_ref, asr_ref, asi_ref, carr_ref, cari_ref, buf_ref, sem, *, chunk):
    ts, d_model = x_ref.shape
    tile = pl.program_id(0)
    last = pl.num_programs(0) - 1
    slot = tile & 1
    seg_len = ts // SUBLANES
    ns = xr_ref.shape[1]
    _, nb_in, kin, nin = bin_ref.shape
    _, nb_out, kout, nout = cout_ref.shape

    @pl.when(tile == 0)
    def _():
        _combine_start(dest_ref, eo_ref, buf_ref, sem, 0, 0)
        carr_ref[...] = jnp.zeros_like(carr_ref)
        cari_ref[...] = jnp.zeros_like(cari_ref)
        ar, ai = abr_ref[...], abi_ref[...]
        a8r_ref[...] = jnp.broadcast_to(ar, a8r_ref.shape)
        a8i_ref[...] = jnp.broadcast_to(ai, a8i_ref.shape)

        def power(j, cur):
            pr, pi = cur
            return pr * ar - pi * ai, pr * ai + pi * ar

        lr, li = lax.fori_loop(0, seg_len - 1, power, (ar, ai))

        def seg_powers(s, cur):
            pr, pi = cur
            asr_ref[pl.ds(s, 1), :] = pr
            asi_ref[pl.ds(s, 1), :] = pi
            return pr * lr - pi * li, pr * li + pi * lr

        lax.fori_loop(0, SUBLANES, seg_powers, (lr, li))

    _combine_wait(eo_ref, buf_ref, sem, slot)
    x = _combined_tile(x_ref[...], slab_ref[...], buf_ref, slot)
    o_ref[...] = x
    _combine_start(dest_ref, eo_ref, buf_ref, sem, jnp.minimum(tile + 1, last), 1 - slot)
    xn = _rms(x, g_ref[...]).astype(BF16)
    xp = jnp.dot(_perm_matrix(ts, seg_len, False), xn, preferred_element_type=F32).astype(BF16)
    u = jnp.dot(xp, win_ref[...], preferred_element_type=F32)
    u_ref[...] = u
    ub = u.astype(BF16)
    for blk in range(nb_in):
        uc = ub[:, blk * kin:(blk + 1) * kin]
        xr_ref[:, blk * nin:(blk + 1) * nin] = jnp.dot(uc, bin_ref[0, blk], preferred_element_type=F32)
        xi_ref[:, blk * nin:(blk + 1) * nin] = jnp.dot(uc, bin_ref[1, blk], preferred_element_type=F32)

    sub8 = _sublane_id((SUBLANES, chunk))
    for ch in range(ns // chunk):
        cs = slice(ch * chunk, (ch + 1) * chunk)
        ar, ai = a8r_ref[:, cs], a8i_ref[:, cs]

        def step(j, carry, cs=cs, ar=ar, ai=ai):
            sr, si = carry
            rows = pl.ds(pl.multiple_of(j * SUBLANES, SUBLANES), SUBLANES)
            nr = ar * sr - ai * si + xr_ref[rows, cs]
            ni = ar * si + ai * sr + xi_ref[rows, cs]
            xr_ref[rows, cs] = nr
            xi_ref[rows, cs] = ni
            return nr, ni

        zero = jnp.zeros((SUBLANES, chunk), F32)
        er, ei = lax.fori_loop(0, seg_len, step, (zero, zero), unroll=SCAN_UNROLL)
        for d in (1, 2, 4):
            keep = sub8 >= d
            pr, pi = asr_ref[d - 1:d, cs], asi_ref[d - 1:d, cs]
            sr = jnp.where(keep, pltpu.roll(er, d, 0), 0.0)
            si = jnp.where(keep, pltpu.roll(ei, d, 0), 0.0)
            er, ei = er + pr * sr - pi * si, ei + pr * si + pi * sr
        cr, ci = carr_ref[:, cs], cari_ref[:, cs]
        pr, pi = asr_ref[:, cs], asi_ref[:, cs]
        er, ei = er + pr * cr - pi * ci, ei + pr * ci + pi * cr
        inr = jnp.where(sub8 >= 1, pltpu.roll(er, 1, 0), cr)
        ini = jnp.where(sub8 >= 1, pltpu.roll(ei, 1, 0), ci)
        carr_ref[:, cs] = er[SUBLANES - 1:SUBLANES, :]
        cari_ref[:, cs] = ei[SUBLANES - 1:SUBLANES, :]

        def fix(j, carry, cs=cs, ar=ar, ai=ai):
            fr, fi = carry
            fr, fi = ar * fr - ai * fi, ar * fi + ai * fr
            rows = pl.ds(pl.multiple_of(j * SUBLANES, SUBLANES), SUBLANES)
            xr_ref[rows, cs] = xr_ref[rows, cs] + fr
            xi_ref[rows, cs] = xi_ref[rows, cs] + fi
            return fr, fi

        lax.fori_loop(0, seg_len, fix, (inr, ini), unroll=SCAN_UNROLL)

    for blk in range(nb_out):
        ks = slice(blk * kout, (blk + 1) * kout)
        ys = slice(blk * nout, (blk + 1) * nout)
        y = jnp.dot(xr_ref[:, ks].astype(BF16), cout_ref[0, blk], preferred_element_type=F32)
        y = y + jnp.dot(xi_ref[:, ks].astype(BF16), cout_ref[1, blk], preferred_element_type=F32)
        y = y + d_ref[:, ys] * u_ref[:, ys]
        z_ref[:, ys] = _gelu_tanh(y).astype(BF16)

    zn = jnp.dot(_perm_matrix(ts, seg_len, True), z_ref[...], preferred_element_type=F32).astype(BF16)
    vg = jnp.dot(zn, wglu_ref[...], preferred_element_type=F32)
    o_ref[...] = o_ref[...] + vg[:, :d_model] * jax.nn.sigmoid(vg[:, d_model:])

    @pl.when(tile == last)
    def _():
        _combine_wait(eo_ref, buf_ref, sem, 1 - slot)


def _s5_layer(pending, norm_g, w_in, lam_re, lam_im, log_dt, b_re, b_im, c_re, c_im, d_skip, w_glu):
    x, slab, dest, eo = pending
    t, d = x.shape
    groups, p, k = b_re.shape
    d_ssm = groups * k
    ns = groups * p
    ts = SCAN_TILE
    seg_len = ts // SUBLANES

    lane = lambda v: v.reshape(1, ns).astype(F32)
    rows_k = lambda b: jnp.transpose(b, (2, 0, 1)).reshape(k, ns).astype(F32)
    abr, abi, bb = pl.pallas_call(
        _s5_disc_kernel,
        in_specs=[_const_spec((1, ns))] * 3 + [_const_spec((k, ns))] * 2,
        out_specs=[pl.BlockSpec((1, ns), lambda i: (0, 0))] * 2 + [pl.BlockSpec((2, k, ns), lambda i: (0, 0, 0))],
        out_shape=[jax.ShapeDtypeStruct((1, ns), F32)] * 2 + [jax.ShapeDtypeStruct((2, k, ns), F32)],
        grid=(1,),
        compiler_params=_cparams(("arbitrary",)),
        name="s5_disc",
    )(lane(lam_re), lane(lam_im), lane(jnp.repeat(log_dt, p)), rows_k(b_re), rows_k(b_im))

    gb = MXU_DIM // k
    nb = groups // gb
    eye = jnp.eye(gb, dtype=F32)

    def in_blocks(b2):
        m = jnp.einsum('rkcgp,gh->rcgkhp', b2.reshape(2, k, nb, gb, p), eye)
        return m.reshape(2, nb, gb * k, gb * p).astype(BF16)

    def out_blocks(c2):
        m = jnp.einsum('rcgkp,gh->rcgphk', c2.reshape(2, nb, gb, k, p).astype(F32), eye)
        return m.reshape(2, nb, gb * p, gb * k).astype(BF16)

    chunk = 512
    return pl.pallas_call(
        functools.partial(_s5_kernel, chunk=chunk),
        grid_spec=pltpu.PrefetchScalarGridSpec(
            num_scalar_prefetch=1, grid=(t // ts,),
            in_specs=[pl.BlockSpec((ts, d), lambda i, dst: (i, 0)),
                      pl.BlockSpec((ts, ROUTE_LANES), lambda i, dst: (i, 0)),
                      pl.BlockSpec(memory_space=pl.ANY),
                      _const_spec((1, d)), _const_spec((d, d_ssm)),
                      _const_spec((2, nb, gb * k, gb * p)), _const_spec((2, nb, gb * p, gb * k)),
                      _const_spec((1, d_ssm)), _const_spec((1, ns)), _const_spec((1, ns)),
                      _const_spec((d_ssm, 2 * d))],
            out_specs=pl.BlockSpec((ts, d), lambda i, dst: (i, 0)),
            scratch_shapes=[pltpu.VMEM((ts, d_ssm), F32), pltpu.VMEM((ts, d_ssm), BF16),
                            pltpu.VMEM((ts, ns), F32), pltpu.VMEM((ts, ns), F32),
                            pltpu.VMEM((SUBLANES, ns), F32), pltpu.VMEM((SUBLANES, ns), F32),
                            pltpu.VMEM((SUBLANES, ns), F32), pltpu.VMEM((SUBLANES, ns), F32),
                            pltpu.VMEM((1, ns), F32), pltpu.VMEM((1, ns), F32),
                            pltpu.VMEM((2, TOP_K, ts, d), F32), pltpu.SemaphoreType.DMA((2, TOP_K))]),
        out_shape=jax.ShapeDtypeStruct((t, d), F32),
        compiler_params=_cparams(("arbitrary",)),
        name="s5_mix",
    )(dest, x, slab, eo, norm_g.reshape(1, d), w_in.astype(BF16), in_blocks(bb),
      out_blocks(jnp.stack([c_re, -c_im])), d_skip.reshape(1, d_ssm).astype(F32),
      abr, abi, w_glu.astype(BF16))


def _route_kernel(x_ref, g_ref, wr_ref, br_ref, slab_ref, er_ref, cnt_ref, base_ref, *, n_groups, epg):
    @pl.when(pl.program_id(0) == 0)
    def _():
        base_ref[...] = jnp.zeros_like(base_ref)

    tr = x_ref.shape[0]
    xn = _rms(x_ref[...], g_ref[...])
    x_hi = xn.astype(BF16)
    x_lo = (xn - x_hi.astype(F32)).astype(BF16)
    hi = jnp.dot(x_hi, wr_ref[...], preferred_element_type=F32)
    lo = jnp.dot(x_lo, wr_ref[:, :ROUTE_LANES], preferred_element_type=F32)
    logits = hi[:, :ROUTE_LANES] + (hi[:, ROUTE_LANES:] + lo) + br_ref[...]
    lane = lax.broadcasted_iota(I32, logits.shape, 1)
    neg = jnp.float32(-3.0e38)

    def first_argmax(v):
        m = jnp.max(v, axis=1, keepdims=True)
        return m, jnp.min(jnp.where(v == m, lane, ROUTE_LANES), axis=1, keepdims=True)

    gmask = lane < n_groups
    gmax, gidx = first_argmax(jnp.where(gmask, logits, neg))
    p_group = 1.0 / jnp.sum(jnp.where(gmask, jnp.exp(logits - gmax), 0.0), axis=1, keepdims=True)
    lo = n_groups + gidx * epg
    el = jnp.where((lane >= lo) & (lane < lo + epg), logits, neg)
    m1, i1 = first_argmax(el)
    m2, i2 = first_argmax(jnp.where(lane == i1, neg, el))
    t2 = jnp.exp(m2 - m1)
    w1 = p_group / (1.0 + t2)
    w2 = p_group * t2 / (1.0 + t2)

    hit1, hit2 = lane == i1, lane == i2
    onehot = (hit1 | hit2).astype(BF16)
    r = lax.broadcasted_iota(I32, (tr, tr), 0)
    c = lax.broadcasted_iota(I32, (tr, tr), 1)
    before = jnp.dot((c < r).astype(BF16), onehot, preferred_element_type=F32) + base_ref[0:1, :]
    rank1 = jnp.sum(jnp.where(hit1, before, 0.0), axis=1, keepdims=True)
    rank2 = jnp.sum(jnp.where(hit2, before, 0.0), axis=1, keepdims=True)
    total = base_ref[0:1, :] + jnp.sum(onehot.astype(F32), axis=0, keepdims=True)
    base_ref[...] = jnp.broadcast_to(total, base_ref.shape)
    cnt_ref[...] = jnp.broadcast_to(total, cnt_ref.shape)

    vals = (w1, w2, (i1 - n_groups).astype(F32), (i2 - n_groups).astype(F32), rank1, rank2)
    slab = jnp.zeros(logits.shape, F32)
    for k, v in enumerate(vals):
        slab = jnp.where(lane == k, v, slab)
    slab_ref[...] = slab
    er_ref[...] = jnp.transpose(slab)[0:SUBLANES, :]


def _row_gather_start(src_ref, dst_ref, sem, index_of, rows, split_priority=True):
    for r in range(rows):
        copy = pltpu.make_async_copy(src_ref.at[pl.ds(index_of(r), 1)], dst_ref.at[pl.ds(r, 1)], sem)
        copy.start(priority=r % 2 if split_priority else 0)


def _row_gather_wait(src_ref, dst_ref, sem):
    pltpu.make_async_copy(src_ref.at[pl.ds(0, dst_ref.shape[0])], dst_ref, sem).wait()


def _gmm_kernel(exp_ref, nexp_ref, wslot_ref, padlo_ref, padhi_ref, meta_ref, dest_ref, x_ref, g_ref,
                w1_ref, w3_ref, w2_ref, o_ref, xbuf_ref, w1f_ref, w3f_ref, w2f_ref,
                w1b_ref, w3b_ref, w2b_ref, tok_ref, sem, wsem, *, layer):
    n = pl.program_id(0)
    n_real = meta_ref[0]
    prev = jnp.maximum(n - 1, 0)
    nslots = xbuf_ref.shape[0]
    slot = lax.rem(n, nslots)
    rows = xbuf_ref.shape[1]
    new_expert = (n == 0) | (exp_ref[n] != exp_ref[prev])

    def gather(block, into):
        base = block * rows
        _row_gather_start(x_ref, xbuf_ref.at[into], sem.at[into], lambda r: tok_ref[base + r], rows,
                          split_priority=False)

    def weight_copies(expert, into):
        return [pltpu.make_async_copy(src.at[layer, expert], dst.at[into], wsem.at[into, j])
                for j, (src, dst) in enumerate(((w1_ref, w1f_ref), (w3_ref, w3f_ref), (w2_ref, w2f_ref)))]

    @pl.when(n == 0)
    def _():
        for cp in weight_copies(exp_ref[0], wslot_ref[0]):
            cp.start(priority=1)

        n_tok = dest_ref.shape[0] // TOP_K

        def pad_expert(e, carry):
            def pad(j, c):
                tok_ref[j] = lax.rem(j, n_tok)
                return c
            return lax.fori_loop(padlo_ref[e], padhi_ref[e], pad, carry)

        lax.fori_loop(0, padlo_ref.shape[0], pad_expert, 0)

        def invert(t, carry):
            for k in range(TOP_K):
                tok_ref[dest_ref[k * n_tok + t]] = t
            return carry

        lax.fori_loop(0, n_tok, invert, 0, unroll=8)
        for ahead in range(GMM_AHEAD):
            gather(ahead, ahead)

    @pl.when(new_expert)
    def _():
        ws = wslot_ref[n]
        for cp in weight_copies(exp_ref[n], ws):
            cp.wait()
        w1b_ref[...] = w1f_ref[ws].astype(BF16)
        w3b_ref[...] = w3f_ref[ws].astype(BF16)
        w2b_ref[...] = w2f_ref[ws].astype(BF16)

        @pl.when(nexp_ref[n] != exp_ref[n])
        def _():
            for cp in weight_copies(nexp_ref[n], 1 - ws):
                cp.start(priority=1)

    @pl.when(n < n_real)
    def _():
        _row_gather_wait(x_ref, xbuf_ref.at[slot], sem.at[slot])
        xb = _rms(xbuf_ref[slot], g_ref[...]).astype(BF16)
        gather(jnp.minimum(n + GMM_AHEAD, n_real - 1), lax.rem(n + GMM_AHEAD, nslots))
        h = jax.nn.silu(jnp.dot(xb, w1b_ref[...], preferred_element_type=F32))
        h = (h * jnp.dot(xb, w3b_ref[...], preferred_element_type=F32)).astype(BF16)
        o_ref[...] = jnp.dot(h, w2b_ref[...], preferred_element_type=F32)

    @pl.when(n >= n_real)
    def _():
        o_ref[...] = jnp.zeros_like(o_ref)

    @pl.when(n == n_real - 1)
    def _():
        for ahead in range(1, GMM_AHEAD + 1):
            other = lax.rem(n + ahead, nslots)
            _row_gather_wait(x_ref, xbuf_ref.at[other], sem.at[other])


def _combine_start(dest_ref, eo_ref, buf_ref, sem, tile, into):
    rows = buf_ref.shape[2]
    n_tok = dest_ref.shape[0] // TOP_K
    for k in range(TOP_K):
        base = k * n_tok + tile * rows
        _row_gather_start(eo_ref, buf_ref.at[into, k], sem.at[into, k],
                          lambda r, base=base: dest_ref[base + r], rows)


def _combine_wait(eo_ref, buf_ref, sem, slot):
    for k in range(TOP_K):
        _row_gather_wait(eo_ref, buf_ref.at[slot, k], sem.at[slot, k])


def _combined_tile(x, slab, buf_ref, slot):
    return x + slab[:, 0:1] * buf_ref[slot, 0] + slab[:, 1:2] * buf_ref[slot, 1]


def _combine_kernel(dest_ref, x_ref, slab_ref, g_ref, eo_ref, o_ref, buf_ref, sem):
    i = pl.program_id(0)
    last = pl.num_programs(0) - 1
    nslots = buf_ref.shape[0]
    slot = lax.rem(i, nslots)

    @pl.when(i == 0)
    def _():
        for ahead in range(COMBINE_AHEAD):
            _combine_start(dest_ref, eo_ref, buf_ref, sem, jnp.minimum(ahead, last), ahead)

    _combine_wait(eo_ref, buf_ref, sem, slot)
    y = _combined_tile(x_ref[...], slab_ref[...], buf_ref, slot)
    _combine_start(dest_ref, eo_ref, buf_ref, sem, jnp.minimum(i + COMBINE_AHEAD, last),
                   lax.rem(i + COMBINE_AHEAD, nslots))
    o_ref[...] = _rms(y, g_ref[...])

    @pl.when(i == last)
    def _():
        for ahead in range(1, COMBINE_AHEAD + 1):
            _combine_wait(eo_ref, buf_ref, sem, lax.rem(i + ahead, nslots))


def _gmm_schedule(counts, n_rows, block):
    n_exp = counts.shape[0]
    n_items = n_rows // block + n_exp
    per_exp = (counts + block - 1) // block
    blk_end = jnp.cumsum(per_exp)
    starts = (blk_end - per_exp) * block
    n_real = blk_end[-1]
    n = jnp.arange(n_items, dtype=I32)
    ids = jnp.arange(n_exp, dtype=I32)
    last_used = jnp.max(jnp.where(per_exp > 0, ids, 0))
    e = jnp.minimum(jnp.sum(blk_end[None, :] <= n[:, None], axis=1), last_used).astype(I32)
    later = jnp.where((per_exp > 0)[None, :] & (ids[None, :] > e[:, None]), ids[None, :], n_exp)
    nxt = jnp.min(later, axis=1)
    nxt = jnp.where(nxt == n_exp, e, nxt).astype(I32)
    wslot = (jnp.sum(((per_exp > 0)[None, :] & (ids[None, :] < e[:, None])).astype(I32), axis=1) & 1)
    pad_lo = (starts + counts).astype(I32)
    pad_hi = (blk_end * block).astype(I32)
    meta = jnp.stack([n_real]).astype(I32)
    return starts, (e, nxt, wslot.astype(I32), pad_lo, pad_hi, meta)


def _moe_layer(x, norm_g, wg, bg, we, be, w1, w3, w2, layer):
    t, d = x.shape
    n_groups, _, epg = we.shape
    n_exp = n_groups * epg
    f = w1.shape[-1]
    tr = min(ROUTER_TILE, t)
    n_rows = t * TOP_K
    g_row = norm_g.reshape(1, d)

    wr = jnp.concatenate([wg, jnp.transpose(we, (1, 0, 2)).reshape(d, n_exp)], axis=1)
    wr = jnp.pad(wr, ((0, 0), (0, ROUTE_LANES - wr.shape[1]))).astype(F32)
    wr_hi = wr.astype(BF16)
    wr = jnp.concatenate([wr_hi, (wr - wr_hi.astype(F32)).astype(BF16)], axis=1)
    br = jnp.pad(jnp.concatenate([bg, be.reshape(-1)]), (0, ROUTE_LANES - n_groups - n_exp))

    slab, er, cnt = pl.pallas_call(
        functools.partial(_route_kernel, n_groups=n_groups, epg=epg),
        grid=(t // tr,),
        in_specs=[pl.BlockSpec((tr, d), lambda i: (i, 0)), _const_spec((1, d)),
                  _const_spec((d, 2 * ROUTE_LANES)), _const_spec((1, ROUTE_LANES))],
        out_specs=[pl.BlockSpec((tr, ROUTE_LANES), lambda i: (i, 0)),
                   pl.BlockSpec((SUBLANES, tr), lambda i: (0, i)),
                   pl.BlockSpec((SUBLANES, ROUTE_LANES), lambda i: (0, 0))],
        out_shape=[jax.ShapeDtypeStruct((t, ROUTE_LANES), F32),
                   jax.ShapeDtypeStruct((SUBLANES, t), F32),
                   jax.ShapeDtypeStruct((SUBLANES, ROUTE_LANES), F32)],
        scratch_shapes=[pltpu.VMEM((SUBLANES, ROUTE_LANES), F32)],
        compiler_params=_cparams(("arbitrary",)),
        name="moe_route",
    )(x, g_row, wr, br.reshape(1, ROUTE_LANES).astype(F32))

    counts = cnt[0, n_groups:n_groups + n_exp].astype(I32)
    starts, items = _gmm_schedule(counts, n_rows, GMM_BLOCK)
    e_idx = er[2:2 + TOP_K].astype(I32)
    start_of = jnp.sum(jnp.where(e_idx[..., None] == jnp.arange(n_exp, dtype=I32), starts, 0), axis=-1)
    dest = (start_of + er[2 + TOP_K:2 + 2 * TOP_K].astype(I32)).reshape(-1)

    n_items = items[0].shape[0]
    any_spec = pl.BlockSpec(memory_space=pl.ANY)
    eo = pl.pallas_call(
        functools.partial(_gmm_kernel, layer=layer),
        grid_spec=pltpu.PrefetchScalarGridSpec(
            num_scalar_prefetch=len(items) + 1, grid=(n_items,),
            in_specs=[any_spec, pl.BlockSpec((1, d), lambda n, *_: (0, 0)),
                      any_spec, any_spec, any_spec],
            out_specs=pl.BlockSpec((GMM_BLOCK, d), lambda n, *_: (n, 0)),
            scratch_shapes=[pltpu.VMEM((GMM_AHEAD + 1, GMM_BLOCK, d), F32),
                            pltpu.VMEM((2, d, f), F32), pltpu.VMEM((2, d, f), F32),
                            pltpu.VMEM((2, f, d), F32),
                            pltpu.VMEM((d, f), BF16), pltpu.VMEM((d, f), BF16), pltpu.VMEM((f, d), BF16),
                            pltpu.SMEM((n_items * GMM_BLOCK,), I32),
                            pltpu.SemaphoreType.DMA((GMM_AHEAD + 1,)),
                            pltpu.SemaphoreType.DMA((2, 3))]),
        out_shape=jax.ShapeDtypeStruct((n_items * GMM_BLOCK, d), F32),
        compiler_params=_cparams(("arbitrary",)),
        name="moe_gmm",
    )(*items, dest, x, g_row, w1, w3, w2)
    return x, slab, dest, eo


def _moe_combine_norm(pending, final_g):
    x, slab, dest, eo = pending
    t, d = x.shape
    tr = ROUTE_TILE
    return pl.pallas_call(
        _combine_kernel,
        grid_spec=pltpu.PrefetchScalarGridSpec(
            num_scalar_prefetch=1, grid=(t // tr,),
            in_specs=[pl.BlockSpec((tr, d), lambda i, dst: (i, 0)),
                      pl.BlockSpec((tr, ROUTE_LANES), lambda i, dst: (i, 0)),
                      pl.BlockSpec((1, d), lambda i, dst: (0, 0)),
                      pl.BlockSpec(memory_space=pl.ANY)],
            out_specs=pl.BlockSpec((tr, d), lambda i, dst: (i, 0)),
            scratch_shapes=[pltpu.VMEM((COMBINE_AHEAD + 1, TOP_K, tr, d), F32),
                            pltpu.SemaphoreType.DMA((COMBINE_AHEAD + 1, TOP_K))]),
        out_shape=jax.ShapeDtypeStruct((t, d), F32),
        compiler_params=_cparams(("arbitrary",)),
        name="moe_combine",
    )(dest, x, slab, final_g.reshape(1, d), eo)


def kernel(x, mix_norm, ffn_norm, final_norm, a_w_in, a_conv_w, a_conv_b, a_gate_a_w, a_gate_a_b, a_gate_x_w, a_gate_x_b, a_lru_L, a_w_out, b_w_in, b_lam_re, b_lam_im, b_log_dt, b_b_re, b_b_im, b_c_re, b_c_im, b_d, b_w_glu, moe_wg, moe_bg, moe_we, moe_be, moe_w1, moe_w3, moe_w2):
    batch, seq, d = x.shape
    depth = mix_norm.shape[0]
    assert depth == 2, "the combine of each MoE layer is fused into the S5 mixer / final norm"
    outs = []
    for b in range(batch):
        h = _rglru_layer(x[b], mix_norm[0], a_w_in[0], a_conv_w[0], a_conv_b[0], a_gate_a_w[0],
                         a_gate_a_b[0], a_gate_x_w[0], a_gate_x_b[0], a_lru_L[0], a_w_out[0])
        pending = _moe_layer(h, ffn_norm[0], moe_wg[0], moe_bg[0], moe_we[0], moe_be[0],
                             moe_w1, moe_w3, moe_w2, 0)
        h = _s5_layer(pending, mix_norm[1], b_w_in[0], b_lam_re[0], b_lam_im[0], b_log_dt[0],
                      b_b_re[0], b_b_im[0], b_c_re[0], b_c_im[0], b_d[0], b_w_glu[0])
        pending = _moe_layer(h, ffn_norm[1], moe_wg[1], moe_bg[1], moe_we[1], moe_be[1],
                             moe_w1, moe_w3, moe_w2, 1)
        outs.append(_moe_combine_norm(pending, final_norm))
    return jnp.stack(outs)
```
